```python
import math
import jax, jax.numpy as jnp
from jax import lax
import numpy as np

D_MODEL = 1024
BATCH = 16
SEQ = 2048
DEPTH = 2
DEC_BATCH = 128
DEC_SEQ = 8
PAST_LEN = 8192
PAGE_SIZE = 128

A_HEADS = 6
A_HEAD_DIM = 64
A_WIDTH = A_HEADS * A_HEAD_DIM
DECAY_LORA = 64
ICLR_LORA = 64
GATE_LORA = 128
A_COLS = 3 * A_WIDTH + DECAY_LORA + ICLR_LORA + GATE_LORA
B_HEADS = 6
B_NOPE_DIM = 64
B_ROPE_DIM = 32
B_V_DIM = 64
B_WIDTH = B_HEADS * B_V_DIM
KV_RANK = 256
ROPE_BASE = 10000.0
B_COLS = B_HEADS * (B_NOPE_DIM + B_ROPE_DIM) + KV_RANK + B_ROPE_DIM
Q_BLOCK = 128
C_GROUPS = 4
C_GROUP_DIM = 64
C_WIDTH = C_GROUPS * C_GROUP_DIM
POOL_WINDOWS = (2, 4, 8, 16)
POOL_PREFIX = 15
IN_COLS = A_COLS + B_COLS + C_WIDTH
D_FF = 2816
CONV_WIDTH = 3
NORM_EPS = 1e-6
GN_EPS = A_HEAD_DIM * 1e-5

kernel_name = 'hymba_rwkv7_mla_pool_convglu_step'


def rmsnorm(x, g):
    xf = x.astype(jnp.float32)
    y = xf * lax.rsqrt(jnp.mean(xf * xf, axis=-1, keepdims=True) + NORM_EPS)
    return (y * g.astype(jnp.float32)).astype(x.dtype)


def rope_tables(pos):
    inv = ROPE_BASE ** (-jnp.arange(0, B_ROPE_DIM, 2, dtype=jnp.float32) / B_ROPE_DIM)
    ang = pos.astype(jnp.float32)[:, None] * inv[None, :]
    return jnp.cos(ang), jnp.sin(ang)


def apply_rope(x, cos, sin):
    xf = x.astype(jnp.float32)
    h = xf.shape[-1] // 2
    x1, x2 = xf[..., :h], xf[..., h:]
    return jnp.concatenate([x1 * cos - x2 * sin, x1 * sin + x2 * cos], axis=-1).astype(x.dtype)


def rwkv7_mix(pa, shift_prev, wkv_prev, p):
    Bn, T, _ = pa.shape
    prev = jnp.concatenate([shift_prev[:, None].astype(pa.dtype), pa[:, :-1]], axis=1)
    s = (pa + (prev - pa) * p['mu_shift']).astype(jnp.float32)
    o1, o2, o3 = A_WIDTH, 2 * A_WIDTH, 3 * A_WIDTH
    o4 = o3 + DECAY_LORA
    o5 = o4 + ICLR_LORA
    r, k, v = s[..., :o1], s[..., o1:o2], s[..., o2:o3]
    wd, ad, gd = s[..., o3:o4], s[..., o4:o5], s[..., o5:]
    w_log = -jax.nn.softplus(-(p['decay_w0'] + jnp.tanh(wd) @ p['decay_w2'])) - 0.5
    decay = jnp.exp(-jnp.exp(w_log))
    a = jax.nn.sigmoid(p['iclr_a0'] + ad @ p['iclr_a2'])
    g = jax.nn.sigmoid(gd) @ p['gate_g2']
    hd = lambda t: t.reshape(Bn, T, A_HEADS, A_HEAD_DIM)
    kk = hd(k * p['k_k'])
    kk = kk * lax.rsqrt(jnp.maximum(jnp.sum(kk * kk, axis=-1, keepdims=True), 1e-24))
    k = k * (1.0 + (a - 1.0) * p['k_a'])
    rh, kh, vh, wh, ah = hd(r), hd(k), hd(v), hd(decay), hd(a)
    xs = tuple(jnp.moveaxis(t, 1, 0) for t in (rh, wh, kh, vh, -kk, kk * ah))

    def step(S, inp):
        r_t, w_t, k_t, v_t, a_t, b_t = inp
        sa = jnp.einsum('bhij,bhj->bhi', S, a_t)
        S = S * w_t[:, :, None, :] + sa[..., None] * b_t[:, :, None, :] + v_t[..., None] * k_t[:, :, None, :]
        return S, jnp.einsum('bhij,bhj->bhi', S, r_t)

    S_T, ys = lax.scan(step, wkv_prev.astype(jnp.float32), xs)
    y = jnp.moveaxis(ys, 0, 1)
    mu = jnp.mean(y, axis=-1, keepdims=True)
    var = jnp.mean(jnp.square(y - mu), axis=-1, keepdims=True)
    y = ((y - mu) * lax.rsqrt(var + GN_EPS)).reshape(Bn, T, A_WIDTH) * p['lnx_w'] + p['lnx_b']
    bonus = (jnp.sum(rh * kh * p['r_k'], axis=-1, keepdims=True) * vh).reshape(Bn, T, A_WIDTH)
    out = (y + bonus) * g
    return out.astype(pa.dtype), pa[:, -1], S_T


def mla_mix(pb, pos, past, p):
    Bn, T, _ = pb.shape
    qn_end = B_HEADS * B_NOPE_DIM
    qr_end = qn_end + B_HEADS * B_ROPE_DIM
    c_end = qr_end + KV_RANK
    q_nope = pb[..., :qn_end].reshape(Bn, T, B_HEADS, B_NOPE_DIM)
    q_rope = pb[..., qn_end:qr_end].reshape(Bn, T, B_HEADS, B_ROPE_DIM)
    ckv = rmsnorm(pb[..., qr_end:c_end], p['kv_norm_g'])
    cos, sin = rope_tables(pos)
    q_rope = apply_rope(q_rope, cos[:, None], sin[:, None])
    kr = apply_rope(pb[..., c_end:], cos, sin)
    q_lat = jnp.einsum('bthd,chd->bthc', q_nope, p['w_uk'])
    scale = (B_NOPE_DIM + B_ROPE_DIM) ** -0.5
    f32 = jnp.float32
    if past is None:
        nb = T // Q_BLOCK
        blocks = lambda t: jnp.moveaxis(t.reshape((Bn, nb, Q_BLOCK) + t.shape[2:]), 1, 0)

        def attend(blk):
            ql, qr, qp = blk
            sc = (jnp.einsum('bqhc,bkc->bhqk', ql, ckv, preferred_element_type=f32)
                  + jnp.einsum('bqhr,bkr->bhqk', qr, kr, preferred_element_type=f32)) * scale
            sc = jnp.where(pos[None, :] <= qp[:, None], sc, -jnp.inf)
            pr = jax.nn.softmax(sc, axis=-1)
            return jnp.einsum('bhqk,bkc->bqhc', pr, ckv.astype(f32))

        o = lax.map(attend, (blocks(q_lat), blocks(q_rope), pos.reshape(nb, Q_BLOCK)))
        o_lat = jnp.moveaxis(o, 0, 1).reshape(Bn, T, B_HEADS, KV_RANK)
    else:
        ckv_past, kr_past = past
        L = ckv_past.shape[1]
        s_past = (jnp.einsum('bqhc,blc->bhql', q_lat, ckv_past, preferred_element_type=f32)
                  + jnp.einsum('bqhr,blr->bhql', q_rope, kr_past, preferred_element_type=f32)) * scale
        s_new = (jnp.einsum('bqhc,bkc->bhqk', q_lat, ckv, preferred_element_type=f32)
                 + jnp.einsum('bqhr,bkr->bhqk', q_rope, kr, preferred_element_type=f32)) * scale
        s_new = jnp.where(pos[None, :] <= pos[:, None], s_new, -jnp.inf)
        pr = jax.nn.softmax(jnp.concatenate([s_past, s_new], axis=-1), axis=-1)
        o_lat = (jnp.einsum('bhql,blc->bqhc', pr[..., :L], ckv_past.astype(f32))
                 + jnp.einsum('bhqk,bkc->bqhc', pr[..., L:], ckv.astype(f32)))
    out = jnp.einsum('bthc,chd->bthd', o_lat, p['w_uv']).reshape(Bn, T, B_WIDTH)
    return out.astype(pb.dtype), ckv, kr


def pool_mix(pc, pool_prev, pos, p):
    Bn, T, _ = pc.shape
    u = jnp.concatenate([pool_prev.astype(pc.dtype), pc], axis=1)
    uf = u.astype(jnp.float32)
    cs = jnp.concatenate([jnp.zeros((Bn, 1, C_WIDTH), jnp.float32), jnp.cumsum(uf, axis=1)], axis=1)
    end = cs[:, POOL_PREFIX + 1:]
    groups = []
    for gi, w in enumerate(POOL_WINDOWS):
        sl = slice(gi * C_GROUP_DIM, (gi + 1) * C_GROUP_DIM)
        start = cs[:, POOL_PREFIX + 1 - w:POOL_PREFIX + 1 - w + T, sl]
        cnt = jnp.minimum(pos + 1, w).astype(jnp.float32)[None, :, None]
        groups.append((end[..., sl] - start) / cnt - uf[:, POOL_PREFIX:, sl])
    d = jnp.stack(groups, axis=2)
    y = jnp.einsum('btgc,gcd->btgd', d, p['pool_w']).reshape(Bn, T, C_WIDTH) * p['pool_scale']
    return y.astype(pc.dtype), u[:, -POOL_PREFIX:]


def conv_ffn(xn, conv_prev, p):
    Bn, T, _ = xn.shape
    h = xn @ p['w_up']
    up, gate = h[..., :D_FF], h[..., D_FF:]
    gx = jnp.concatenate([conv_prev.astype(h.dtype), gate], axis=1)
    gc = p['conv_b'] + p['conv_w'][0] * gx[:, 0:T]
    for j in range(1, CONV_WIDTH):
        gc = gc + p['conv_w'][j] * gx[:, j:j + T]
    hh = jax.nn.silu(gc) * up
    return hh @ p['w_down'], gx[:, T:]


def decoder_layer(x, pos, shift_prev, wkv_prev, pool_prev, conv_prev, mla_past, p):
    xn = rmsnorm(x, p['g_mix'])
    proj = xn @ p['w_in']
    pa = proj[..., :A_COLS]
    pb = proj[..., A_COLS:A_COLS + B_COLS]
    pc = proj[..., A_COLS + B_COLS:]
    ya, shift_new, wkv_new = rwkv7_mix(pa, shift_prev, wkv_prev, p)
    yb, ckv_new, kr_new = mla_mix(pb, pos, mla_past, p)
    yc, pool_new = pool_mix(pc, pool_prev, pos, p)
    x = x + jnp.concatenate([ya, yb, yc], axis=-1) @ p['w_out']
    yf, conv_new = conv_ffn(rmsnorm(x, p['g_ffn']), conv_prev, p)
    x = x + yf
    return x, (ckv_new, kr_new, wkv_new, shift_new, pool_new, conv_new)


def stack_layers(outs, i):
    return jnp.stack([o[i] for o in outs], axis=0)


def setup_inputs(seed: int = 0) -> dict:
    key = jax.random.key(seed)
    ks = jax.random.split(key, 40)
    f32 = jnp.float32
    nrm = lambda k, shape, sc: jax.random.normal(k, shape, f32) * sc
    n_pages = PAST_LEN // PAGE_SIZE
    n_pool = (DEC_BATCH * n_pages * 5) // 4
    page_table = jax.random.permutation(ks[4], n_pool)[:DEC_BATCH * n_pages].reshape(DEC_BATCH, n_pages).astype(jnp.int32)
    return {
        'x_prompt': nrm(ks[0], (BATCH, SEQ, D_MODEL), 1.0),
        'x_sample': nrm(ks[1], (DEC_BATCH, DEC_SEQ, D_MODEL), 1.0),
        'cache_kv_latent': nrm(ks[2], (DEPTH, n_pool, PAGE_SIZE, KV_RANK), 1.0),
        'cache_k_rope': nrm(ks[3], (DEPTH, n_pool, PAGE_SIZE, B_ROPE_DIM), 1.0),
        'page_table': page_table,
        'state_wkv': nrm(ks[5], (DEPTH, DEC_BATCH, A_HEADS, A_HEAD_DIM, A_HEAD_DIM), 0.5),
        'state_shift': nrm(ks[6], (DEPTH, DEC_BATCH, A_COLS), 1.0),
        'state_pool': nrm(ks[7], (DEPTH, DEC_BATCH, POOL_PREFIX, C_WIDTH), 1.0),
        'state_conv': nrm(ks[8], (DEPTH, DEC_BATCH, CONV_WIDTH - 1, D_FF), 1.0),
        'norm_mix_g': 1.0 + nrm(ks[9], (DEPTH, D_MODEL), 0.05),
        'w_in': nrm(ks[10], (DEPTH, D_MODEL, IN_COLS), D_MODEL ** -0.5),
        'mu_shift': jax.random.uniform(ks[11], (DEPTH, A_COLS), f32),
        'decay_w0': -2.0 + nrm(ks[12], (DEPTH, A_WIDTH), 0.5),
        'decay_w2': nrm(ks[13], (DEPTH, DECAY_LORA, A_WIDTH), DECAY_LORA ** -0.5),
        'iclr_a0': nrm(ks[14], (DEPTH, A_WIDTH), 0.1),
        'iclr_a2': nrm(ks[15], (DEPTH, ICLR_LORA, A_WIDTH), ICLR_LORA ** -0.5),
        'gate_g2': nrm(ks[16], (DEPTH, GATE_LORA, A_WIDTH), GATE_LORA ** -0.5),
        'k_k': 0.85 + nrm(ks[17], (DEPTH, A_WIDTH), 0.05),
        'k_a': 1.0 + nrm(ks[18], (DEPTH, A_WIDTH), 0.05),
        'r_k': nrm(ks[19], (DEPTH, A_HEADS, A_HEAD_DIM), 0.1),
        'lnx_w': 1.0 + nrm(ks[20], (DEPTH, A_WIDTH), 0.05),
        'lnx_b': nrm(ks[21], (DEPTH, A_WIDTH), 0.02),
        'kv_norm_g': 1.0 + nrm(ks[22], (DEPTH, KV_RANK), 0.05),
        'w_uk': nrm(ks[23], (DEPTH, KV_RANK, B_HEADS, B_NOPE_DIM), KV_RANK ** -0.5),
        'w_uv': nrm(ks[24], (DEPTH, KV_RANK, B_HEADS, B_V_DIM), KV_RANK ** -0.5),
        'pool_w': nrm(ks[25], (DEPTH, C_GROUPS, C_GROUP_DIM, C_GROUP_DIM), C_GROUP_DIM ** -0.5),
        'pool_scale': 1.0 + nrm(ks[26], (DEPTH, C_WIDTH), 0.1),
        'w_out': nrm(ks[27], (DEPTH, D_MODEL, D_MODEL), D_MODEL ** -0.5),
        'norm_ffn_g': 1.0 + nrm(ks[28], (DEPTH, D_MODEL), 0.05),
        'w_up': nrm(ks[29], (DEPTH, D_MODEL, 2 * D_FF), D_MODEL ** -0.5),
        'conv_w': nrm(ks[30], (DEPTH, CONV_WIDTH, D_FF), CONV_WIDTH ** -0.5),
        'conv_b': nrm(ks[31], (DEPTH, D_FF), 0.02),
        'w_down': nrm(ks[32], (DEPTH, D_FF, D_MODEL), D_FF ** -0.5),
        'norm_final_g': 1.0 + nrm(ks[33], (D_MODEL,), 0.05),
    }


def reference(x_prompt, x_sample, cache_kv_latent, cache_k_rope, page_table, state_wkv, state_shift,
              state_pool, state_conv, norm_mix_g, w_in, mu_shift, decay_w0, decay_w2, iclr_a0, iclr_a2,
              gate_g2, k_k, k_a, r_k, lnx_w, lnx_b, kv_norm_g, w_uk, w_uv, pool_w, pool_scale, w_out,
              norm_ffn_g, w_up, conv_w, conv_b, w_down, norm_final_g):
    Bp, Tp, _ = x_prompt.shape
    Bs, Ts, _ = x_sample.shape
    past_len = page_table.shape[1] * PAGE_SIZE
    pos_p = jnp.arange(Tp, dtype=jnp.int32)
    pos_s = past_len + jnp.arange(Ts, dtype=jnp.int32)
    dt = x_prompt.dtype
    zero_shift = jnp.zeros((Bp, A_COLS), dt)
    zero_wkv = jnp.zeros((Bp, A_HEADS, A_HEAD_DIM, A_HEAD_DIM), jnp.float32)
    zero_pool = jnp.zeros((Bp, POOL_PREFIX, C_WIDTH), dt)
    zero_conv = jnp.zeros((Bp, CONV_WIDTH - 1, D_FF), dt)
    xp, xs = x_prompt, x_sample
    outs_p, outs_s = [], []
    for l in range(DEPTH):
        p = dict(g_mix=norm_mix_g[l], w_in=w_in[l], mu_shift=mu_shift[l], decay_w0=decay_w0[l],
                 decay_w2=decay_w2[l], iclr_a0=iclr_a0[l], iclr_a2=iclr_a2[l], gate_g2=gate_g2[l],
                 k_k=k_k[l], k_a=k_a[l], r_k=r_k[l], lnx_w=lnx_w[l], lnx_b=lnx_b[l],
                 kv_norm_g=kv_norm_g[l], w_uk=w_uk[l], w_uv=w_uv[l], pool_w=pool_w[l],
                 pool_scale=pool_scale[l], w_out=w_out[l], g_ffn=norm_ffn_g[l], w_up=w_up[l],
                 conv_w=conv_w[l], conv_b=conv_b[l], w_down=w_down[l])
        xp, st_p = decoder_layer(xp, pos_p, zero_shift, zero_wkv, zero_pool, zero_conv, None, p)
        ckv_past = cache_kv_latent[l][page_table].reshape(Bs, past_len, KV_RANK)
        kr_past = cache_k_rope[l][page_table].reshape(Bs, past_len, B_ROPE_DIM)
        xs, st_s = decoder_layer(xs, pos_s, state_shift[l], state_wkv[l], state_pool[l], state_conv[l],
                                 (ckv_past, kr_past), p)
        outs_p.append(st_p)
        outs_s.append(st_s)
    y_prompt = rmsnorm(xp, norm_final_g)
    y_sample = rmsnorm(xs, norm_final_g)
    return (y_prompt, y_sample,
            stack_layers(outs_p, 0), stack_layers(outs_p, 1), stack_layers(outs_p, 2),
            stack_layers(outs_p, 3), stack_layers(outs_p, 4), stack_layers(outs_p, 5),
            stack_layers(outs_s, 0), stack_layers(outs_s, 1), stack_layers(outs_s, 2),
            stack_layers(outs_s, 3), stack_layers(outs_s, 4), stack_layers(outs_s, 5))
```

```python
import functools
import math

import jax
import jax.numpy as jnp
import numpy as np
from jax import lax
from jax.experimental import pallas as pl
from jax.experimental.pallas import tpu as pltpu

F32 = jnp.float32
BF16 = jnp.bfloat16

D_MODEL = 1024
DEPTH = 2
PAGE_SIZE = 128
A_HEADS = 6
A_HEAD_DIM = 64
A_WIDTH = A_HEADS * A_HEAD_DIM
DECAY_LORA = 64
ICLR_LORA = 64
GATE_LORA = 128
A_COLS = 3 * A_WIDTH + DECAY_LORA + ICLR_LORA + GATE_LORA
B_HEADS = 6
B_NOPE_DIM = 64
B_ROPE_DIM = 32
B_V_DIM = 64
B_WIDTH = B_HEADS * B_V_DIM
KV_RANK = 256
ROPE_BASE = 10000.0
B_COLS = B_HEADS * (B_NOPE_DIM + B_ROPE_DIM) + KV_RANK + B_ROPE_DIM
C_GROUPS = 4
C_GROUP_DIM = 64
C_WIDTH = C_GROUPS * C_GROUP_DIM
POOL_WINDOWS = (2, 4, 8, 16)
POOL_PREFIX = 15
D_FF = 2816
CONV_WIDTH = 3
NORM_EPS = 1e-6
GN_EPS = A_HEAD_DIM * 1e-5

LANES = 128
SUBLANES = 8
HALF_ROPE = B_ROPE_DIM // 2
IN_EXT = A_COLS + B_HEADS * B_NOPE_DIM + KV_RANK + C_WIDTH + 4 * LANES
OFF_QN = A_COLS
OFF_CKV = OFF_QN + B_HEADS * B_NOPE_DIM
OFF_PC = OFF_CKV + KV_RANK
OFF_ROPE = OFF_PC + C_WIDTH
KFULL = KV_RANK + 2 * LANES
FF_CHUNK = D_FF // 2
VMEM_LIMIT = 56 * 1024 * 1024


def _cparams(sem):
    return pltpu.CompilerParams(dimension_semantics=sem, vmem_limit_bytes=VMEM_LIMIT)


def _const_spec(shape):
    nd = len(shape)
    return pl.BlockSpec(shape, lambda *_: (0,) * nd, pipeline_mode=pl.Buffered(1))


class _Tok:
    def __init__(self, time_major, B, T, tile):
        self.time_major, self.B, self.T = time_major, B, T
        if time_major:
            self.bb = tile
            self.tm, self.ts = T * tile, tile
            self.grid = (B // tile, 1)
        else:
            self.tm, self.ts = tile, 1
            self.grid = (B, T // tile)
        self.n = B * T

    def shape(self, C):
        return (self.T, self.B, C) if self.time_major else (self.B, self.T, C)

    def spec(self, C):
        if self.time_major:
            return pl.BlockSpec((self.T, self.bb, C), lambda i, j: (0, i, 0))
        return pl.BlockSpec((None, self.tm, C), lambda i, j: (i, j, 0))

    def state_shape(self, steps, C):
        return (steps, self.B, C) if self.time_major else (self.B, steps, C)

    def state_spec(self, steps, C):
        if self.time_major:
            return pl.BlockSpec((steps, self.bb, C), lambda i, j: (0, i, 0))
        return pl.BlockSpec((None, steps, C), lambda i, j: (i, 0, 0))


def _rms(x, g):
    return x * lax.rsqrt(jnp.mean(x * x, axis=-1, keepdims=True) + NORM_EPS) * g


def _sigmoid(x):
    return 1.0 / (1.0 + jnp.exp(-x))


def _div_pow2(x, d):
    assert d & (d - 1) == 0
    return lax.shift_right_logical(x, int(math.log2(d)))


def _mod_pow2(x, d):
    assert d & (d - 1) == 0
    return lax.bitwise_and(x, d - 1)


def _inproj_kernel(x_ref, g_ref, w_ref, kvg_ref, cos_ref, sin_ref,
                   pa_ref, qn_ref, kvlat_ref, pc_ref, qr_ref, kfull_ref, krope_ref, *, tm):
    x = x_ref[...].reshape(tm, D_MODEL)
    xb = _rms(x, g_ref[...]).astype(BF16)

    def proj(lo, hi):
        return jnp.dot(xb, w_ref[:, lo:hi], preferred_element_type=F32)

    pa_ref[...] = proj(0, A_COLS).reshape(pa_ref.shape)
    qn_ref[...] = proj(OFF_QN, OFF_CKV).astype(BF16).reshape(qn_ref.shape)
    ckv = _rms(proj(OFF_CKV, OFF_PC), kvg_ref[...])
    kvlat_ref[...] = ckv.reshape(kvlat_ref.shape)
    pc_ref[...] = proj(OFF_PC, OFF_ROPE).reshape(pc_ref.shape)
    rr = proj(OFF_ROPE, IN_EXT)
    cos = cos_ref[...]
    sin = sin_ref[...]
    q1, q2 = rr[:, 0:LANES], rr[:, LANES:2 * LANES]
    k1, k2 = rr[:, 2 * LANES:3 * LANES], rr[:, 3 * LANES:4 * LANES]
    o1 = q1 * cos - q2 * sin
    o2 = q1 * sin + q2 * cos
    qr = jnp.concatenate([o1, o2], axis=-1).astype(BF16)
    qr_ref[...] = qr.reshape(qr_ref.shape)
    kf = jnp.concatenate([ckv, k1 * cos - k2 * sin, k1 * sin + k2 * cos], axis=-1).astype(BF16)
    kfull_ref[...] = kf.reshape(kfull_ref.shape)
    kr_lo = B_HEADS * HALF_ROPE
    krope = jnp.concatenate([o1[:, kr_lo:kr_lo + HALF_ROPE], o2[:, kr_lo:kr_lo + HALF_ROPE]], axis=-1)
    krope_ref[...] = krope.reshape(krope_ref.shape)


def _inproj(tok, x, g, w_ext, kvg, cos, sin):
    tm = tok.tm
    n_tab = cos.shape[0] // tm
    tab_spec = pl.BlockSpec((tm, LANES), lambda i, j: (j % n_tab, 0))
    widths = (A_COLS, A_WIDTH, KV_RANK, C_WIDTH, 2 * LANES, KFULL, B_ROPE_DIM)
    dtypes = (F32, BF16, F32, F32, BF16, BF16, F32)
    return pl.pallas_call(
        functools.partial(_inproj_kernel, tm=tm),
        grid=tok.grid,
        in_specs=[tok.spec(D_MODEL), _const_spec((1, D_MODEL)), _const_spec((D_MODEL, IN_EXT)),
                  _const_spec((1, KV_RANK)), tab_spec, tab_spec],
        out_specs=[tok.spec(c) for c in widths],
        out_shape=[jax.ShapeDtypeStruct(tok.shape(c), d) for c, d in zip(widths, dtypes)],
        compiler_params=_cparams(("arbitrary", "arbitrary")),
        name="inproj",
    )(x, g, w_ext, kvg, cos, sin)


def _rwkv_prep_kernel(pa_ref, st_ref, mu_ref, wcat_ref, w0_ref, a0_ref, g2_ref, kk_ref, ka_ref, ones_ref,
                      r_ref, w_ref, k_ref, v_ref, a_ref, b_ref, g_ref, newst_ref, buf_ref, *, tm, ts):
    pad = max(SUBLANES, ts)
    it = pl.program_id(1)

    @pl.when(it == 0)
    def _():
        buf_ref[pad - ts:pad, :] = st_ref[...].reshape(ts, A_COLS)

    pa = pa_ref[...].reshape(tm, A_COLS)
    buf_ref[pad:pad + tm, :] = pa
    prev = buf_ref[pad - ts:pad - ts + tm, :]
    last = pa[tm - ts:tm, :]
    buf_ref[pad - ts:pad, :] = last
    newst_ref[...] = last.reshape(newst_ref.shape)

    s = pa + (prev - pa) * mu_ref[...]
    W = A_WIDTH
    r, k, v = s[:, 0:W], s[:, W:2 * W], s[:, 2 * W:3 * W]
    lo_in = s[:, 3 * W:3 * W + LANES]
    gd = s[:, 3 * W + LANES:3 * W + 2 * LANES]
    lane = lax.broadcasted_iota(jnp.int32, lo_in.shape, 1)
    lo_in = jnp.where(lane < DECAY_LORA, jnp.tanh(lo_in), lo_in)
    lo = jnp.dot(lo_in.astype(BF16), wcat_ref[...], preferred_element_type=F32)
    z = -(w0_ref[...] + lo[:, 0:W])
    softplus = jnp.maximum(z, 0.0) + jnp.log(1.0 + jnp.exp(-jnp.abs(z)))
    decay = jnp.exp(-jnp.exp(-softplus - 0.5))
    a = _sigmoid(a0_ref[...] + lo[:, W:2 * W])
    g = jnp.dot(_sigmoid(gd).astype(BF16), g2_ref[...], preferred_element_type=F32)
    kk = k * kk_ref[...]
    kk2 = kk * kk
    hi = kk2.astype(BF16)
    lo2 = (kk2 - hi.astype(F32)).astype(BF16)
    ones = ones_ref[...]
    ss = (jnp.dot(hi, ones, preferred_element_type=F32) + jnp.dot(lo2, ones, preferred_element_type=F32))
    kk = kk * lax.rsqrt(jnp.maximum(ss, 1e-24))
    kmod = k * (1.0 + (a - 1.0) * ka_ref[...])
    for ref, val in ((r_ref, r), (w_ref, decay), (k_ref, kmod), (v_ref, v), (a_ref, -kk), (b_ref, kk * a),
                     (g_ref, g)):
        ref[...] = val.reshape(ref.shape)


def _rwkv_prep(tok, pa, shift_prev, p):
    tm, ts = tok.tm, tok.ts
    pad = max(SUBLANES, ts)
    vec = _const_spec((1, A_WIDTH))
    return pl.pallas_call(
        functools.partial(_rwkv_prep_kernel, tm=tm, ts=ts),
        grid=tok.grid,
        in_specs=[tok.spec(A_COLS), tok.state_spec(1, A_COLS), _const_spec((1, A_COLS)),
                  _const_spec((LANES, 2 * A_WIDTH)), vec, vec, _const_spec((GATE_LORA, A_WIDTH)), vec, vec,
                  _const_spec((A_WIDTH, A_WIDTH))],
        out_specs=[tok.spec(A_WIDTH)] * 7 + [tok.state_spec(1, A_COLS)],
        out_shape=[jax.ShapeDtypeStruct(tok.shape(A_WIDTH), F32)] * 7
        + [jax.ShapeDtypeStruct(tok.state_shape(1, A_COLS), F32)],
        scratch_shapes=[pltpu.VMEM((pad + tm, A_COLS), F32)],
        compiler_params=_cparams(("arbitrary", "arbitrary")),
        name="rwkv_prep",
    )(pa, shift_prev, p["mu"], p["lora_cat"], p["w0"], p["a0"], p["g2"], p["k_k"], p["k_a"], p["head_ones"])


def _rwkv_scan_kernel(w_ref, a_ref, b_ref, k_ref, v_ref, r_ref, s0_ref, lnw_ref, lnb_ref, rk_ref,
                      y_ref, sT_ref, S_ref, wr_ref, *, tc):
    N = A_HEAD_DIM
    it = pl.program_id(1)

    @pl.when(it == 0)
    def _():
        S_ref[...] = s0_ref[0]

    def step(t, carry):
        w = w_ref[0, t]
        r = r_ref[0, t]
        k = k_ref[0, t]
        b = b_ref[0, t]
        v = v_ref[0, t]
        wr_ref[...] = w * r
        sa = [jnp.zeros((N, LANES), F32) for _ in range(2)]
        yp = [jnp.zeros((N, LANES), F32) for _ in range(2)]
        for j in range(N):
            Sj = S_ref[j]
            sa[j % 2] = sa[j % 2] + Sj * a_ref[0, t, pl.ds(j, 1), :]
            yp[j % 2] = yp[j % 2] + Sj * wr_ref[pl.ds(j, 1), :]
        sa = sa[0] + sa[1]
        yp = yp[0] + yp[1]
        for j in range(N):
            S_ref[j] = (S_ref[j] * w_ref[0, t, pl.ds(j, 1), :] + sa * b_ref[0, t, pl.ds(j, 1), :]
                        + v * k_ref[0, t, pl.ds(j, 1), :])
        br = jnp.sum(b * r, axis=0, keepdims=True)
        kr = jnp.sum(k * r, axis=0, keepdims=True)
        y = yp + sa * br + v * kr
        mu = jnp.mean(y, axis=0, keepdims=True)
        d = y - mu
        var = jnp.mean(d * d, axis=0, keepdims=True)
        yn = d * lax.rsqrt(var + GN_EPS) * lnw_ref[0] + lnb_ref[0]
        bonus = jnp.sum(r * k * rk_ref[0], axis=0, keepdims=True)
        y_ref[0, t] = yn + bonus * v
        return carry

    lax.fori_loop(0, tc, step, 0)

    @pl.when(it == pl.num_programs(1) - 1)
    def _():
        sT_ref[0] = S_ref[...]


def _rwkv_scan(w, a, b, k, v, r, s0, lnw, lnb, rk, tc):
    G, T, N, _ = w.shape
    seq = pl.BlockSpec((1, tc, N, LANES), lambda g, i: (g, i, 0, 0))
    st = pl.BlockSpec((1, N, N, LANES), lambda g, i: (g, 0, 0, 0))
    cst = pl.BlockSpec((1, N, LANES), lambda g, i: (g, 0, 0))
    return pl.pallas_call(
        functools.partial(_rwkv_scan_kernel, tc=tc),
        grid=(G, T // tc),
        in_specs=[seq] * 6 + [st, cst, cst, cst],
        out_specs=[seq, st],
        out_shape=[jax.ShapeDtypeStruct((G, T, N, LANES), F32), jax.ShapeDtypeStruct((G, N, N, LANES), F32)],
        scratch_shapes=[pltpu.VMEM((N, N, LANES), F32), pltpu.VMEM((N, LANES), F32)],
        compiler_params=_cparams(("arbitrary", "arbitrary")),
        name="rwkv_scan",
    )(w, a, b, k, v, r, s0, lnw, lnb, rk)


_NT = (((1,), (1,)), ((), ()))


def _build_queries(qn, qr, wuk_ref, scale, rows):
    lane = lax.broadcasted_iota(jnp.int32, (rows, 2 * LANES), 1)
    slot = _div_pow2(_mod_pow2(lane, LANES), HALF_ROPE)
    qrf = qr.astype(F32) * scale
    parts = []
    for h in range(B_HEADS):
        ql = jnp.dot(qn[:, h * B_NOPE_DIM:(h + 1) * B_NOPE_DIM], wuk_ref[h], preferred_element_type=F32) * scale
        parts.append(jnp.concatenate([ql, jnp.where(slot == h, qrf, 0.0)], axis=-1).astype(BF16))
    return parts


def _mla_prompt_kernel(qn_ref, qr_ref, kf_ref, wuk_ref, wuv_ref, o_ref, q_s, m_s, l_s, acc_s, *, tq, scale):
    qi = pl.program_id(1)
    parts = _build_queries(qn_ref[...], qr_ref[...], wuk_ref, scale, tq)
    for h in range(B_HEADS):
        q_s[h * tq:(h + 1) * tq, :] = parts[h]
    rows = B_HEADS * tq
    m_s[...] = jnp.full((rows, 1), -jnp.inf, F32)
    l_s[...] = jnp.zeros((rows, 1), F32)
    acc_s[...] = jnp.zeros((rows, KV_RANK), F32)

    def chunk(kc, masked):
        kt = kf_ref[pl.ds(pl.multiple_of(kc * tq, tq), tq), :]
        s = lax.dot_general(q_s[...], kt, _NT, preferred_element_type=F32)
        if masked:
            qt = _mod_pow2(lax.broadcasted_iota(jnp.int32, (rows, tq), 0), tq)
            kt_pos = lax.broadcasted_iota(jnp.int32, (rows, tq), 1)
            s = jnp.where(kt_pos <= qt, s, -jnp.inf)
        m_prev = m_s[...]
        m_new = jnp.maximum(m_prev, jnp.max(s, axis=-1, keepdims=True))
        alpha = jnp.exp(m_prev - m_new)
        p = jnp.exp(s - m_new)
        l_s[...] = alpha * l_s[...] + jnp.sum(p, axis=-1, keepdims=True)
        acc_s[...] = alpha * acc_s[...] + jnp.dot(p.astype(BF16), kt[:, 0:KV_RANK], preferred_element_type=F32)
        m_s[...] = m_new

    def body(kc, carry):
        chunk(kc, False)
        return carry

    lax.fori_loop(0, qi, body, 0)
    chunk(qi, True)
    o = (acc_s[...] / l_s[...]).astype(BF16)
    outs = [jnp.dot(o[h * tq:(h + 1) * tq, :], wuv_ref[h], preferred_element_type=F32) for h in range(B_HEADS)]
    o_ref[...] = jnp.concatenate(outs, axis=-1).astype(o_ref.dtype)


def _mla_prompt(qn, qr, kfull, wuk, wuv, tq):
    B, T, _ = qn.shape
    scale = (B_NOPE_DIM + B_ROPE_DIM) ** -0.5
    rows = B_HEADS * tq
    tile = lambda c: pl.BlockSpec((None, tq, c), lambda b, i: (b, i, 0))
    return pl.pallas_call(
        functools.partial(_mla_prompt_kernel, tq=tq, scale=scale),
        grid=(B, T // tq),
        in_specs=[tile(A_WIDTH), tile(2 * LANES), pl.BlockSpec((None, T, KFULL), lambda b, i: (b, 0, 0)),
                  _const_spec((B_HEADS, B_NOPE_DIM, KV_RANK)), _const_spec((B_HEADS, KV_RANK, B_V_DIM))],
        out_specs=tile(B_WIDTH),
        out_shape=jax.ShapeDtypeStruct((B, T, B_WIDTH), BF16),
        scratch_shapes=[pltpu.VMEM((rows, KFULL), BF16), pltpu.VMEM((rows, 1), F32), pltpu.VMEM((rows, 1), F32),
                        pltpu.VMEM((rows, KV_RANK), F32)],
        compiler_params=_cparams(("arbitrary", "arbitrary")),
        name="mla_prompt",
    )(qn, qr, kfull, wuk, wuv)


def _mla_decode_kernel(pt_ref, qn_ref, qr_ref, kfn_ref, wuk_ref, wuv_ref, ckv_hbm, kr_hbm,
                       o_ref, ckv_buf, kr_buf, kv16, sem, *, layer, n_pages, T, scale):
    b = pl.program_id(0)
    nb = pl.num_programs(0)
    slot = b % 2
    L = n_pages * PAGE_SIZE

    def page_copies(bb, sl, p):
        page = pt_ref[bb, p]
        rows = pl.ds(pl.multiple_of(p * PAGE_SIZE, PAGE_SIZE), PAGE_SIZE)
        return (pltpu.make_async_copy(ckv_hbm.at[layer, page], ckv_buf.at[sl, rows], sem.at[0, sl]),
                pltpu.make_async_copy(kr_hbm.at[layer, page], kr_buf.at[sl, rows], sem.at[1, sl]))

    def issue(bb, sl):
        def body(p, carry):
            for cp in page_copies(bb, sl, p):
                cp.start()
            return carry
        lax.fori_loop(0, n_pages, body, 0)

    @pl.when(b == 0)
    def _():
        issue(0, 0)

    @pl.when(b + 1 < nb)
    def _():
        issue(b + 1, 1 - slot)

    def wait_body(p, carry):
        for cp in page_copies(b, slot, p):
            cp.wait()
        return carry

    lax.fori_loop(0, n_pages, wait_body, 0)

    cast_rows = math.gcd(L, 1024)

    def cast_body(c, carry):
        rows = pl.ds(pl.multiple_of(c * cast_rows, cast_rows), cast_rows)
        kv16[rows, :] = ckv_buf[slot, rows, :].astype(BF16)
        return carry
    lax.fori_loop(0, L // cast_rows, cast_body, 0)

    qn = qn_ref[0]
    qr = qr_ref[0]
    kfn = kfn_ref[0]
    parts = _build_queries(qn, qr, wuk_ref, scale, T)
    qfull = jnp.concatenate(parts, axis=0)
    qlat = qfull[:, 0:KV_RANK]
    qrs = (qr.astype(F32) * scale)
    qrope = jnp.concatenate(
        [jnp.concatenate([qrs[:, h * HALF_ROPE:(h + 1) * HALF_ROPE],
                          qrs[:, LANES + h * HALF_ROPE:LANES + (h + 1) * HALF_ROPE]], axis=-1)
         for h in range(B_HEADS)], axis=0).astype(BF16)
    kr16 = kr_buf[slot].astype(BF16)
    s_past = (lax.dot_general(qlat, kv16[...], _NT, preferred_element_type=F32)
              + lax.dot_general(qrope, kr16, _NT, preferred_element_type=F32))
    s_new = lax.dot_general(qfull, kfn, _NT, preferred_element_type=F32)
    rows = B_HEADS * T
    qt = _mod_pow2(lax.broadcasted_iota(jnp.int32, (rows, T), 0), T)
    kt = lax.broadcasted_iota(jnp.int32, (rows, T), 1)
    s_new = jnp.where(kt <= qt, s_new, -jnp.inf)
    m = jnp.maximum(jnp.max(s_past, axis=-1, keepdims=True), jnp.max(s_new, axis=-1, keepdims=True))
    p_past = jnp.exp(s_past - m)
    p_new = jnp.exp(s_new - m)
    l = jnp.sum(p_past, axis=-1, keepdims=True) + jnp.sum(p_new, axis=-1, keepdims=True)
    o = (jnp.dot(p_past.astype(BF16), kv16[...], preferred_element_type=F32)
         + jnp.dot(p_new.astype(BF16), kfn[:, 0:KV_RANK], preferred_element_type=F32))
    o = (o / l).astype(BF16)
    outs = [jnp.dot(o[h * T:(h + 1) * T, :], wuv_ref[h], preferred_element_type=F32) for h in range(B_HEADS)]
    o_ref[0] = jnp.concatenate(outs, axis=-1).astype(o_ref.dtype)


def _mla_decode(page_table, qn, qr, kfn, wuk, wuv, cache_kv, cache_kr, layer):
    B, T, _ = qn.shape
    n_pages = page_table.shape[1]
    L = n_pages * PAGE_SIZE
    scale = (B_NOPE_DIM + B_ROPE_DIM) ** -0.5
    tile = lambda c: pl.BlockSpec((1, T, c), lambda b, pt: (b, 0, 0))
    cst = lambda shape: pl.BlockSpec(shape, lambda b, pt: (0,) * len(shape))
    grid_spec = pltpu.PrefetchScalarGridSpec(
        num_scalar_prefetch=1,
        grid=(B,),
        in_specs=[tile(A_WIDTH), tile(2 * LANES), tile(KFULL),
                  cst((B_HEADS, B_NOPE_DIM, KV_RANK)), cst((B_HEADS, KV_RANK, B_V_DIM)),
                  pl.BlockSpec(memory_space=pl.ANY), pl.BlockSpec(memory_space=pl.ANY)],
        out_specs=tile(B_WIDTH),
        scratch_shapes=[pltpu.VMEM((2, L, KV_RANK), F32), pltpu.VMEM((2, L, B_ROPE_DIM), F32),
                        pltpu.VMEM((L, KV_RANK), BF16), pltpu.SemaphoreType.DMA((2, 2))],
    )
    return pl.pallas_call(
        functools.partial(_mla_decode_kernel, layer=layer, n_pages=n_pages, T=T, scale=scale),
        grid_spec=grid_spec,
        out_shape=jax.ShapeDtypeStruct((B, T, B_WIDTH), BF16),
        compiler_params=_cparams(("arbitrary",)),
        name="mla_decode",
    )(page_table, qn, qr, kfn, wuk, wuv, cache_kv, cache_kr)


def _pool_kernel(pc_ref, st_ref, wbd_ref, scale_ref, y_ref, newst_ref, e_s, s2_s, s4_s, s8_s, *, ts, T, pos0):
    L = T * ts
    P = (POOL_PREFIX + 1) * ts
    pc = pc_ref[...].reshape(L, C_WIDTH)
    e_s[P - POOL_PREFIX * ts:P, :] = st_ref[...].reshape(POOL_PREFIX * ts, C_WIDTH)
    e_s[P:P + L, :] = pc
    newst_ref[...] = e_s[P + L - POOL_PREFIX * ts:P + L, :].reshape(newst_ref.shape)
    s2_s[P - 14 * ts:P + L, :] = e_s[P - 14 * ts:P + L, :] + e_s[P - 15 * ts:P + L - ts, :]
    s4_s[P - 12 * ts:P + L, :] = s2_s[P - 12 * ts:P + L, :] + s2_s[P - 14 * ts:P + L - 2 * ts, :]
    s8_s[P - 8 * ts:P + L, :] = s4_s[P - 8 * ts:P + L, :] + s4_s[P - 12 * ts:P + L - 4 * ts, :]
    s16 = s8_s[P:P + L, :] + s8_s[P - 8 * ts:P + L - 8 * ts, :]
    lane = lax.broadcasted_iota(jnp.int32, (L, C_WIDTH), 1)
    grp = _div_pow2(lane, C_GROUP_DIM)
    win = jnp.where(grp == 0, s2_s[P:P + L, :],
                    jnp.where(grp == 1, s4_s[P:P + L, :], jnp.where(grp == 2, s8_s[P:P + L, :], s16)))
    wsize = jnp.where(grp == 0, 2, jnp.where(grp == 1, 4, jnp.where(grp == 2, 8, 16)))
    pos = pos0 + _div_pow2(lax.broadcasted_iota(jnp.int32, (L, C_WIDTH), 0), ts)
    cnt = jnp.minimum(pos + 1, wsize).astype(F32)
    d = win / cnt - pc
    y = jnp.dot(d.astype(BF16), wbd_ref[...], preferred_element_type=F32) * scale_ref[...]
    y_ref[...] = y.astype(y_ref.dtype).reshape(y_ref.shape)


def _pool(tok, pc, pool_prev, wbd, pscale, pos0):
    ts, T = tok.ts, tok.T
    rows = (POOL_PREFIX + 1 + T) * ts
    if tok.time_major:
        grid, blk = (tok.B // tok.bb, 1), tok.spec(C_WIDTH)
    else:
        grid, blk = (tok.B, 1), pl.BlockSpec((None, T, C_WIDTH), lambda i, j: (i, 0, 0))
    return pl.pallas_call(
        functools.partial(_pool_kernel, ts=ts, T=T, pos0=pos0),
        grid=grid,
        in_specs=[blk, tok.state_spec(POOL_PREFIX, C_WIDTH), _const_spec((C_WIDTH, C_WIDTH)),
                  _const_spec((1, C_WIDTH))],
        out_specs=[blk, tok.state_spec(POOL_PREFIX, C_WIDTH)],
        out_shape=[jax.ShapeDtypeStruct(tok.shape(C_WIDTH), BF16),
                   jax.ShapeDtypeStruct(tok.state_shape(POOL_PREFIX, C_WIDTH), F32)],
        scratch_shapes=[pltpu.VMEM((rows, C_WIDTH), F32)] * 4,
        compiler_params=_cparams(("arbitrary", "arbitrary")),
        name="pool",
    )(pc, pool_prev, wbd, pscale)


def _ffn_kernel(x_ref, ya_ref, g_ref, yb_ref, yc_ref, cst_ref, wout_ref, gffn_ref, wup_ref, cw_ref, cb_ref,
                wdn_ref, gfin_ref, xo_ref, cnew_ref, gbuf_s, *, tm, ts, final):
    taps = CONV_WIDTH - 1
    pad = max(SUBLANES, taps * ts)
    it = pl.program_id(1)

    @pl.when(it == 0)
    def _():
        gbuf_s[pad - taps * ts:pad, :] = cst_ref[...].reshape(taps * ts, D_FF)

    x = x_ref[...].reshape(tm, D_MODEL)
    ya = (ya_ref[...].reshape(tm, A_WIDTH) * g_ref[...].reshape(tm, A_WIDTH)).astype(BF16)
    ycat = jnp.concatenate([ya, yb_ref[...].reshape(tm, B_WIDTH), yc_ref[...].reshape(tm, C_WIDTH)], axis=-1)
    x1 = x + jnp.dot(ycat, wout_ref[...], preferred_element_type=F32)
    xn = _rms(x1, gffn_ref[...]).astype(BF16)
    acc = jnp.zeros((tm, D_MODEL), F32)
    for c in range(D_FF // FF_CHUNK):
        lo = c * FF_CHUNK
        up = jnp.dot(xn, wup_ref[:, lo:lo + FF_CHUNK], preferred_element_type=F32)
        gate = jnp.dot(xn, wup_ref[:, D_FF + lo:D_FF + lo + FF_CHUNK], preferred_element_type=F32)
        gbuf_s[pad:pad + tm, lo:lo + FF_CHUNK] = gate
        gc = cb_ref[:, lo:lo + FF_CHUNK] + cw_ref[2:3, lo:lo + FF_CHUNK] * gate
        for j in range(taps):
            off = pad - (taps - j) * ts
            gc = gc + cw_ref[j:j + 1, lo:lo + FF_CHUNK] * gbuf_s[off:off + tm, lo:lo + FF_CHUNK]
        hh = (gc * _sigmoid(gc) * up).astype(BF16)
        acc = acc + jnp.dot(hh, wdn_ref[lo:lo + FF_CHUNK, :], preferred_element_type=F32)
    last = gbuf_s[pad + tm - taps * ts:pad + tm, :]
    cnew_ref[...] = last.reshape(cnew_ref.shape)
    gbuf_s[pad - taps * ts:pad, :] = last
    x2 = x1 + acc
    if final:
        x2 = _rms(x2, gfin_ref[...])
    xo_ref[...] = x2.reshape(xo_ref.shape)


def _ffn(tok, x, ya, g, yb, yc, conv_prev, p, gfin, final):
    tm, ts = tok.tm, tok.ts
    taps = CONV_WIDTH - 1
    pad = max(SUBLANES, taps * ts)
    return pl.pallas_call(
        functools.partial(_ffn_kernel, tm=tm, ts=ts, final=final),
        grid=tok.grid,
        in_specs=[tok.spec(D_MODEL), tok.spec(A_WIDTH), tok.spec(A_WIDTH), tok.spec(B_WIDTH), tok.spec(C_WIDTH),
                  tok.state_spec(taps, D_FF), _const_spec((D_MODEL, D_MODEL)), _const_spec((1, D_MODEL)),
                  _const_spec((D_MODEL, 2 * D_FF)), _const_spec((CONV_WIDTH, D_FF)), _const_spec((1, D_FF)),
                  _const_spec((D_FF, D_MODEL)), _const_spec((1, D_MODEL))],
        out_specs=[tok.spec(D_MODEL), tok.state_spec(taps, D_FF)],
        out_shape=[jax.ShapeDtypeStruct(tok.shape(D_MODEL), F32),
                   jax.ShapeDtypeStruct(tok.state_shape(taps, D_FF), F32)],
        scratch_shapes=[pltpu.VMEM((pad + tm, D_FF), F32)],
        compiler_params=_cparams(("arbitrary", "arbitrary")),
        name="outproj_ffn",
    )(x, ya, g, yb, yc, conv_prev, p["w_out"], p["g_ffn"], p["w_up"], p["conv_w"], p["conv_b"], p["w_down"], gfin)


def _ext_columns():
    qn0 = A_COLS
    qr0 = qn0 + B_HEADS * B_NOPE_DIM
    ckv0 = qr0 + B_HEADS * B_ROPE_DIM
    kr0 = ckv0 + KV_RANK
    pc0 = A_COLS + B_COLS
    cols = list(range(A_COLS)) + list(range(qn0, qr0)) + list(range(ckv0, kr0)) + list(range(pc0, pc0 + C_WIDTH))
    for half in range(2):
        blk = []
        for h in range(B_HEADS):
            blk += [qr0 + h * B_ROPE_DIM + half * HALF_ROPE + i for i in range(HALF_ROPE)]
        blk += [kr0 + half * HALF_ROPE + i for i in range(HALF_ROPE)]
        cols += blk + [-1] * (LANES - len(blk))
    for half in range(2):
        blk = [kr0 + half * HALF_ROPE + i for _ in range(B_HEADS) for i in range(HALF_ROPE)]
        cols += blk + [-1] * (LANES - len(blk))
    assert len(cols) == IN_EXT
    return np.asarray(cols, np.int32)


def _layer_params(l, norm_mix_g, w_in, mu_shift, decay_w0, decay_w2, iclr_a0, iclr_a2, gate_g2, k_k, k_a, r_k,
                  lnx_w, lnx_b, kv_norm_g, w_uk, w_uv, pool_w, pool_scale, w_out, norm_ffn_g, w_up, conv_w,
                  conv_b, w_down):
    cols = _ext_columns()
    w_ext = jnp.where(cols[None, :] >= 0, w_in[l][:, np.maximum(cols, 0)], 0.0).astype(BF16)
    lora_cat = jnp.zeros((LANES, 2 * A_WIDTH), F32)
    lora_cat = lora_cat.at[0:DECAY_LORA, 0:A_WIDTH].set(decay_w2[l])
    lora_cat = lora_cat.at[DECAY_LORA:DECAY_LORA + ICLR_LORA, A_WIDTH:].set(iclr_a2[l])
    head = np.arange(A_WIDTH) // A_HEAD_DIM
    head_ones = jnp.asarray((head[:, None] == head[None, :]).astype(np.float32), BF16)
    wbd = jnp.zeros((C_WIDTH, C_WIDTH), F32)
    for gi in range(C_GROUPS):
        sl = slice(gi * C_GROUP_DIM, (gi + 1) * C_GROUP_DIM)
        wbd = wbd.at[sl, sl].set(pool_w[l, gi])
    row = lambda v: v.reshape(1, -1)
    return dict(
        g_mix=row(norm_mix_g[l]), w_ext=w_ext, kv_g=row(kv_norm_g[l]),
        mu=row(mu_shift[l]), lora_cat=lora_cat.astype(BF16), w0=row(decay_w0[l]), a0=row(iclr_a0[l]),
        g2=gate_g2[l].astype(BF16), k_k=row(k_k[l]), k_a=row(k_a[l]), head_ones=head_ones,
        r_k=r_k[l], lnx_w=lnx_w[l].reshape(A_HEADS, A_HEAD_DIM), lnx_b=lnx_b[l].reshape(A_HEADS, A_HEAD_DIM),
        wuk=jnp.transpose(w_uk[l], (1, 2, 0)).astype(BF16), wuv=jnp.transpose(w_uv[l], (1, 0, 2)).astype(BF16),
        pool_wbd=wbd.astype(BF16), pool_scale=row(pool_scale[l]),
        w_out=w_out[l].astype(BF16), g_ffn=row(norm_ffn_g[l]), w_up=w_up[l].astype(BF16),
        conv_w=conv_w[l], conv_b=row(conv_b[l]), w_down=w_down[l].astype(BF16),
    )


def _rope_tables(pos):
    inv = ROPE_BASE ** (-jnp.arange(0, B_ROPE_DIM, 2, dtype=F32) / B_ROPE_DIM)
    ang = pos.astype(F32)[:, None] * inv[None, :]
    reps = LANES // HALF_ROPE
    return jnp.tile(jnp.cos(ang), (1, reps)), jnp.tile(jnp.sin(ang), (1, reps))


def _to_scan(tok, x):
    H, N = A_HEADS, A_HEAD_DIM
    if tok.time_major:
        x = x.reshape(tok.T, tok.B, H, N)
        return jnp.transpose(x, (2, 0, 3, 1))
    x = x.reshape(tok.B, tok.T, H, N)
    x = jnp.transpose(x, (1, 3, 0, 2)).reshape(tok.T, N, tok.B * H)
    return jnp.pad(x, ((0, 0), (0, 0), (0, LANES - tok.B * H)))[None]


def _from_scan(tok, y):
    H, N = A_HEADS, A_HEAD_DIM
    if tok.time_major:
        return jnp.transpose(y, (1, 3, 0, 2)).reshape(tok.T, tok.B, H * N)
    y = y[0, :, :, :tok.B * H].reshape(tok.T, N, tok.B, H)
    return jnp.transpose(y, (2, 0, 3, 1)).reshape(tok.B, tok.T, H * N)


def _state_to_scan(tok, s):
    if tok.time_major:
        return jnp.transpose(s, (1, 3, 2, 0))
    B, H, N, _ = s.shape
    s = jnp.transpose(s, (3, 2, 0, 1)).reshape(N, N, B * H)
    return jnp.pad(s, ((0, 0), (0, 0), (0, LANES - B * H)))[None]


def _state_from_scan(tok, s):
    if tok.time_major:
        return jnp.transpose(s, (3, 0, 2, 1))
    N = A_HEAD_DIM
    s = s[0, :, :, :tok.B * A_HEADS].reshape(N, N, tok.B, A_HEADS)
    return jnp.transpose(s, (2, 3, 1, 0))


def _head_consts(tok, p):
    def lay(v):
        if tok.time_major:
            return jnp.broadcast_to(v[:, :, None], (A_HEADS, A_HEAD_DIM, LANES))
        t = jnp.tile(v.T, (1, tok.B))
        return jnp.pad(t, ((0, 0), (0, LANES - tok.B * A_HEADS)))[None]
    return lay(p["lnx_w"]), lay(p["lnx_b"]), lay(p["r_k"])


def _layer(tok, x, cos, sin, pos0, shift_prev, wkv_prev, pool_prev, conv_prev, p, gfin, final, scan_tc, attend):
    pa, qn, kvlat, pc, qr, kfull, krope = _inproj(tok, x, p["g_mix"], p["w_ext"], p["kv_g"], cos, sin)
    r, w, k, v, a, b, g, shift_new = _rwkv_prep(tok, pa, shift_prev, p)
    lnw, lnb, rk = _head_consts(tok, p)
    y, s_new = _rwkv_scan(*(_to_scan(tok, t) for t in (w, a, b, k, v, r)), _state_to_scan(tok, wkv_prev),
                          lnw, lnb, rk, scan_tc)
    ya = _from_scan(tok, y)
    yb = attend(qn, qr, kfull)
    yc, pool_new = _pool(tok, pc, pool_prev, p["pool_wbd"], p["pool_scale"], pos0)
    x, conv_new = _ffn(tok, x, ya, g, yb, yc, conv_prev, p, gfin, final)
    return x, (kvlat, krope, _state_from_scan(tok, s_new), shift_new, pool_new, conv_new)


def kernel(x_prompt, x_sample, cache_kv_latent, cache_k_rope, page_table, state_wkv, state_shift, state_pool,
           state_conv, norm_mix_g, w_in, mu_shift, decay_w0, decay_w2, iclr_a0, iclr_a2, gate_g2, k_k, k_a, r_k,
           lnx_w, lnx_b, kv_norm_g, w_uk, w_uv, pool_w, pool_scale, w_out, norm_ffn_g, w_up, conv_w, conv_b,
           w_down, norm_final_g):
    Bp, Tp, _ = x_prompt.shape
    Bs, Ts, _ = x_sample.shape
    past_len = page_table.shape[1] * PAGE_SIZE
    tokp = _Tok(False, Bp, Tp, min(512, Tp))
    toks = _Tok(True, Bs, Ts, min(32, Bs))
    cos_p, sin_p = _rope_tables(jnp.arange(Tp, dtype=jnp.int32))
    pos_s = past_len + jnp.repeat(jnp.arange(Ts, dtype=jnp.int32), toks.bb)
    cos_s, sin_s = _rope_tables(pos_s)
    gfin = norm_final_g.reshape(1, -1)
    tm_first = lambda t: jnp.swapaxes(t, 0, 1)

    xp = x_prompt
    xs = tm_first(x_sample)
    zeros = lambda *shape: jnp.zeros(shape, F32)
    outs_p, outs_s = [], []
    for l in range(DEPTH):
        p = _layer_params(l, norm_mix_g, w_in, mu_shift, decay_w0, decay_w2, iclr_a0, iclr_a2, gate_g2, k_k, k_a,
                          r_k, lnx_w, lnx_b, kv_norm_g, w_uk, w_uv, pool_w, pool_scale, w_out, norm_ffn_g, w_up,
                          conv_w, conv_b, w_down)
        final = l == DEPTH - 1

        attend_p = lambda qn, qr, kfull: _mla_prompt(qn, qr, kfull, p["wuk"], p["wuv"], min(256, Tp))
        xp, st_p = _layer(tokp, xp, cos_p, sin_p, 0, zeros(Bp, 1, A_COLS),
                          zeros(Bp, A_HEADS, A_HEAD_DIM, A_HEAD_DIM), zeros(Bp, POOL_PREFIX, C_WIDTH),
                          zeros(Bp, CONV_WIDTH - 1, D_FF), p, gfin, final, min(16, Tp), attend_p)

        def attend_s(qn, qr, kfull):
            yb = _mla_decode(page_table, tm_first(qn), tm_first(qr), tm_first(kfull), p["wuk"], p["wuv"],
                             cache_kv_latent, cache_k_rope, l)
            return tm_first(yb)
        xs, st_s = _layer(toks, xs, cos_s, sin_s, past_len, state_shift[l][None], state_wkv[l],
                          tm_first(state_pool[l]), tm_first(state_conv[l]), p, gfin, final, Ts, attend_s)
        outs_p.append(st_p)
        outs_s.append(st_s)

    stack = lambda outs, i, f: jnp.stack([f(o[i]) for o in outs], axis=0)
    ident = lambda t: t
    return (xp, tm_first(xs),
            stack(outs_p, 0, ident), stack(outs_p, 1, ident), stack(outs_p, 2, ident),
            stack(outs_p, 3, lambda t: t[:, 0]), stack(outs_p, 4, ident), stack(outs_p, 5, ident),
            stack(outs_s, 0, tm_first), stack(outs_s, 1, tm_first), stack(outs_s, 2, ident),
            stack(outs_s, 3, lambda t: t[0]), stack(outs_s, 4, tm_first), stack(outs_s, 5, tm_first))
```

```python
import functools
import math

import jax
import jax.numpy as jnp
import numpy as np
from jax import lax
from jax.experimental import pallas as pl
from jax.experimental.pallas import tpu as pltpu

F32 = jnp.float32
BF16 = jnp.bfloat16

D_MODEL = 1024
DEPTH = 2
PAGE_SIZE = 128
A_HEADS = 6
A_HEAD_DIM = 64
A_WIDTH = A_HEADS * A_HEAD_DIM
DECAY_LORA = 64
ICLR_LORA = 64
GATE_LORA = 128
A_COLS = 3 * A_WIDTH + DECAY_LORA + ICLR_LORA + GATE_LORA
B_HEADS = 6
B_NOPE_DIM = 64
B_ROPE_DIM = 32
B_V_DIM = 64
B_WIDTH = B_HEADS * B_V_DIM
KV_RANK = 256
ROPE_BASE = 10000.0
B_COLS = B_HEADS * (B_NOPE_DIM + B_ROPE_DIM) + KV_RANK + B_ROPE_DIM
C_GROUPS = 4
C_GROUP_DIM = 64
C_WIDTH = C_GROUPS * C_GROUP_DIM
POOL_WINDOWS = (2, 4, 8, 16)
POOL_PREFIX = 15
D_FF = 2816
CONV_WIDTH = 3
NORM_EPS = 1e-6
GN_EPS = A_HEAD_DIM * 1e-5

LANES = 128
SUBLANES = 8
HALF_ROPE = B_ROPE_DIM // 2
IN_EXT = A_COLS + B_HEADS * B_NOPE_DIM + KV_RANK + C_WIDTH + 4 * LANES
OFF_QN = A_COLS
OFF_CKV = OFF_QN + B_HEADS * B_NOPE_DIM
OFF_PC = OFF_CKV + KV_RANK
OFF_ROPE = OFF_PC + C_WIDTH
KFULL = KV_RANK + 2 * LANES
FF_CHUNK = D_FF // 2
VMEM_LIMIT = 56 * 1024 * 1024


def _cparams(sem):
    return pltpu.CompilerParams(dimension_semantics=sem, vmem_limit_bytes=VMEM_LIMIT)


def _const_spec(shape):
    nd = len(shape)
    return pl.BlockSpec(shape, lambda *_: (0,) * nd, pipeline_mode=pl.Buffered(1))


class _Tok:
    def __init__(self, time_major, B, T, tile):
        self.time_major, self.B, self.T = time_major, B, T
        if time_major:
            self.bb = tile
            self.tm, self.ts = T * tile, tile
            self.grid = (B // tile, 1)
        else:
            self.tm, self.ts = tile, 1
            self.grid = (B, T // tile)
        self.n = B * T

    def shape(self, C):
        return (self.T, self.B, C) if self.time_major else (self.B, self.T, C)

    def spec(self, C):
        if self.time_major:
            return pl.BlockSpec((self.T, self.bb, C), lambda i, j: (0, i, 0))
        return pl.BlockSpec((None, self.tm, C), lambda i, j: (i, j, 0))

    def state_shape(self, steps, C):
        return (steps, self.B, C) if self.time_major else (self.B, steps, C)

    def state_spec(self, steps, C):
        if self.time_major:
            return pl.BlockSpec((steps, self.bb, C), lambda i, j: (0, i, 0))
        return pl.BlockSpec((None, steps, C), lambda i, j: (i, 0, 0))


def _rms(x, g):
    return x * lax.rsqrt(jnp.mean(x * x, axis=-1, keepdims=True) + NORM_EPS) * g


def _sigmoid(x):
    return 1.0 / (1.0 + jnp.exp(-x))


def _div_pow2(x, d):
    assert d & (d - 1) == 0
    return lax.shift_right_logical(x, int(math.log2(d)))


def _mod_pow2(x, d):
    assert d & (d - 1) == 0
    return lax.bitwise_and(x, d - 1)


def _inproj_kernel(x_ref, g_ref, w_ref, kvg_ref, cos_ref, sin_ref,
                   pa_ref, qn_ref, kvlat_ref, pc_ref, qr_ref, kfull_ref, krope_ref, *, tm):
    x = x_ref[...].reshape(tm, D_MODEL)
    xb = _rms(x, g_ref[...]).astype(BF16)

    def proj(lo, hi):
        return jnp.dot(xb, w_ref[:, lo:hi], preferred_element_type=F32)

    pa_ref[...] = proj(0, A_COLS).reshape(pa_ref.shape)
    qn_ref[...] = proj(OFF_QN, OFF_CKV).astype(BF16).reshape(qn_ref.shape)
    ckv = _rms(proj(OFF_CKV, OFF_PC), kvg_ref[...])
    kvlat_ref[...] = ckv.reshape(kvlat_ref.shape)
    pc_ref[...] = proj(OFF_PC, OFF_ROPE).reshape(pc_ref.shape)
    rr = proj(OFF_ROPE, IN_EXT)
    cos = cos_ref[...]
    sin = sin_ref[...]
    q1, q2 = rr[:, 0:LANES], rr[:, LANES:2 * LANES]
    k1, k2 = rr[:, 2 * LANES:3 * LANES], rr[:, 3 * LANES:4 * LANES]
    o1 = q1 * cos - q2 * sin
    o2 = q1 * sin + q2 * cos
    qr = jnp.concatenate([o1, o2], axis=-1).astype(BF16)
    qr_ref[...] = qr.reshape(qr_ref.shape)
    kf = jnp.concatenate([ckv, k1 * cos - k2 * sin, k1 * sin + k2 * cos], axis=-1).astype(BF16)
    kfull_ref[...] = kf.reshape(kfull_ref.shape)
    kr_lo = B_HEADS * HALF_ROPE
    krope = jnp.concatenate([o1[:, kr_lo:kr_lo + HALF_ROPE], o2[:, kr_lo:kr_lo + HALF_ROPE]], axis=-1)
    krope_ref[...] = krope.reshape(krope_ref.shape)


def _inproj(tok, x, g, w_ext, kvg, cos, sin):
    tm = tok.tm
    n_tab = cos.shape[0] // tm
    tab_spec = pl.BlockSpec((tm, LANES), lambda i, j: (j % n_tab, 0))
    widths = (A_COLS, A_WIDTH, KV_RANK, C_WIDTH, 2 * LANES, KFULL, B_ROPE_DIM)
    dtypes = (F32, BF16, F32, F32, BF16, BF16, F32)
    return pl.pallas_call(
        functools.partial(_inproj_kernel, tm=tm),
        grid=tok.grid,
        in_specs=[tok.spec(D_MODEL), _const_spec((1, D_MODEL)), _const_spec((D_MODEL, IN_EXT)),
                  _const_spec((1, KV_RANK)), tab_spec, tab_spec],
        out_specs=[tok.spec(c) for c in widths],
        out_shape=[jax.ShapeDtypeStruct(tok.shape(c), d) for c, d in zip(widths, dtypes)],
        compiler_params=_cparams(("arbitrary", "arbitrary")),
        name="inproj",
    )(x, g, w_ext, kvg, cos, sin)


def _rwkv_prep_kernel(pa_ref, st_ref, mu_ref, wcat_ref, w0_ref, a0_ref, g2_ref, kk_ref, ka_ref, ones_ref,
                      r_ref, w_ref, k_ref, v_ref, a_ref, b_ref, g_ref, newst_ref, buf_ref, *, tm, ts):
    pad = max(SUBLANES, ts)
    it = pl.program_id(1)

    @pl.when(it == 0)
    def _():
        buf_ref[pad - ts:pad, :] = st_ref[...].reshape(ts, A_COLS)

    pa = pa_ref[...].reshape(tm, A_COLS)
    buf_ref[pad:pad + tm, :] = pa
    prev = buf_ref[pad - ts:pad - ts + tm, :]
    last = pa[tm - ts:tm, :]
    buf_ref[pad - ts:pad, :] = last
    newst_ref[...] = last.reshape(newst_ref.shape)

    s = pa + (prev - pa) * mu_ref[...]
    W = A_WIDTH
    r, k, v = s[:, 0:W], s[:, W:2 * W], s[:, 2 * W:3 * W]
    lo_in = s[:, 3 * W:3 * W + LANES]
    gd = s[:, 3 * W + LANES:3 * W + 2 * LANES]
    lane = lax.broadcasted_iota(jnp.int32, lo_in.shape, 1)
    lo_in = jnp.where(lane < DECAY_LORA, jnp.tanh(lo_in), lo_in)
    lo = jnp.dot(lo_in.astype(BF16), wcat_ref[...], preferred_element_type=F32)
    z = -(w0_ref[...] + lo[:, 0:W])
    softplus = jnp.maximum(z, 0.0) + jnp.log(1.0 + jnp.exp(-jnp.abs(z)))
    decay = jnp.exp(-jnp.exp(-softplus - 0.5))
    a = _sigmoid(a0_ref[...] + lo[:, W:2 * W])
    g = jnp.dot(_sigmoid(gd).astype(BF16), g2_ref[...], preferred_element_type=F32)
    kk = k * kk_ref[...]
    kk2 = kk * kk
    hi = kk2.astype(BF16)
    lo2 = (kk2 - hi.astype(F32)).astype(BF16)
    ones = ones_ref[...]
    ss = (jnp.dot(hi, ones, preferred_element_type=F32) + jnp.dot(lo2, ones, preferred_element_type=F32))
    kk = kk * lax.rsqrt(jnp.maximum(ss, 1e-24))
    kmod = k * (1.0 + (a - 1.0) * ka_ref[...])
    for ref, val in ((r_ref, r), (w_ref, decay), (k_ref, kmod), (v_ref, v), (a_ref, -kk), (b_ref, kk * a),
                     (g_ref, g)):
        ref[...] = val.reshape(ref.shape)


def _rwkv_prep(tok, pa, shift_prev, p):
    tm, ts = tok.tm, tok.ts
    pad = max(SUBLANES, ts)
    vec = _const_spec((1, A_WIDTH))
    return pl.pallas_call(
        functools.partial(_rwkv_prep_kernel, tm=tm, ts=ts),
        grid=tok.grid,
        in_specs=[tok.spec(A_COLS), tok.state_spec(1, A_COLS), _const_spec((1, A_COLS)),
                  _const_spec((LANES, 2 * A_WIDTH)), vec, vec, _const_spec((GATE_LORA, A_WIDTH)), vec, vec,
                  _const_spec((A_WIDTH, A_WIDTH))],
        out_specs=[tok.spec(A_WIDTH)] * 7 + [tok.state_spec(1, A_COLS)],
        out_shape=[jax.ShapeDtypeStruct(tok.shape(A_WIDTH), F32)] * 7
        + [jax.ShapeDtypeStruct(tok.state_shape(1, A_COLS), F32)],
        scratch_shapes=[pltpu.VMEM((pad + tm, A_COLS), F32)],
        compiler_params=_cparams(("arbitrary", "arbitrary")),
        name="rwkv_prep",
    )(pa, shift_prev, p["mu"], p["lora_cat"], p["w0"], p["a0"], p["g2"], p["k_k"], p["k_a"], p["head_ones"])


def _rwkv_scan_kernel(w_ref, a_ref, b_ref, k_ref, v_ref, r_ref, s0_ref, lnw_ref, lnb_ref, rk_ref,
                      y_ref, sT_ref, S_ref, wr_ref, *, tc):
    N = A_HEAD_DIM
    it = pl.program_id(1)

    @pl.when(it == 0)
    def _():
        S_ref[...] = s0_ref[0]

    def step(t, carry):
        w = w_ref[0, t]
        r = r_ref[0, t]
        k = k_ref[0, t]
        b = b_ref[0, t]
        v = v_ref[0, t]
        wr_ref[...] = w * r
        sa = [jnp.zeros((N, LANES), F32) for _ in range(2)]
        yp = [jnp.zeros((N, LANES), F32) for _ in range(2)]
        for j in range(N):
            Sj = S_ref[j]
            sa[j % 2] = sa[j % 2] + Sj * a_ref[0, t, pl.ds(j, 1), :]
            yp[j % 2] = yp[j % 2] + Sj * wr_ref[pl.ds(j, 1), :]
        sa = sa[0] + sa[1]
        yp = yp[0] + yp[1]
        for j in range(N):
            S_ref[j] = (S_ref[j] * w_ref[0, t, pl.ds(j, 1), :] + sa * b_ref[0, t, pl.ds(j, 1), :]
                        + v * k_ref[0, t, pl.ds(j, 1), :])
        br = jnp.sum(b * r, axis=0, keepdims=True)
        kr = jnp.sum(k * r, axis=0, keepdims=True)
        y = yp + sa * br + v * kr
        mu = jnp.mean(y, axis=0, keepdims=True)
        d = y - mu
        var = jnp.mean(d * d, axis=0, keepdims=True)
        yn = d * lax.rsqrt(var + GN_EPS) * lnw_ref[0] + lnb_ref[0]
        bonus = jnp.sum(r * k * rk_ref[0], axis=0, keepdims=True)
        y_ref[0, t] = yn + bonus * v
        return carry

    lax.fori_loop(0, tc, step, 0)

    @pl.when(it == pl.num_programs(1) - 1)
    def _():
        sT_ref[0] = S_ref[...]


def _rwkv_scan(w, a, b, k, v, r, s0, lnw, lnb, rk, tc):
    G, T, N, _ = w.shape
    seq = pl.BlockSpec((1, tc, N, LANES), lambda g, i: (g, i, 0, 0))
    st = pl.BlockSpec((1, N, N, LANES), lambda g, i: (g, 0, 0, 0))
    cst = pl.BlockSpec((1, N, LANES), lambda g, i: (g, 0, 0))
    return pl.pallas_call(
        functools.partial(_rwkv_scan_kernel, tc=tc),
        grid=(G, T // tc),
        in_specs=[seq] * 6 + [st, cst, cst, cst],
        out_specs=[seq, st],
        out_shape=[jax.ShapeDtypeStruct((G, T, N, LANES), F32), jax.ShapeDtypeStruct((G, N, N, LANES), F32)],
        scratch_shapes=[pltpu.VMEM((N, N, LANES), F32), pltpu.VMEM((N, LANES), F32)],
        compiler_params=_cparams(("arbitrary", "arbitrary")),
        name="rwkv_scan",
    )(w, a, b, k, v, r, s0, lnw, lnb, rk)


_NT = (((1,), (1,)), ((), ()))


def _build_queries(qn, qr, wuk_ref, scale, rows):
    lane = lax.broadcasted_iota(jnp.int32, (rows, 2 * LANES), 1)
    slot = _div_pow2(_mod_pow2(lane, LANES), HALF_ROPE)
    qrf = qr.astype(F32) * scale
    parts = []
    for h in range(B_HEADS):
        ql = jnp.dot(qn[:, h * B_NOPE_DIM:(h + 1) * B_NOPE_DIM], wuk_ref[h], preferred_element_type=F32) * scale
        parts.append(jnp.concatenate([ql, jnp.where(slot == h, qrf, 0.0)], axis=-1).astype(BF16))
    return parts


def _mla_prompt_kernel(qn_ref, qr_ref, kf_ref, wuk_ref, wuv_ref, o_ref, q_s, s_s, p_s, m_s, l_s, a_s, acc_s,
                       *, tq, scale, row_block):
    qi = pl.program_id(1)
    parts = _build_queries(qn_ref[...], qr_ref[...], wuk_ref, scale, tq)
    for h in range(B_HEADS):
        q_s[h * tq:(h + 1) * tq, :] = parts[h]
    rows = B_HEADS * tq
    m_s[...] = jnp.full((rows, LANES), -jnp.inf, F32)
    l_s[...] = jnp.zeros((rows, LANES), F32)
    acc_s[...] = jnp.zeros((rows, KV_RANK), F32)

    def keys(kc):
        return pl.ds(pl.multiple_of(kc * tq, tq), tq)

    def scores(kc, slot):
        s_s[slot] = lax.dot_general(q_s[...], kf_ref[keys(kc), :], _NT, preferred_element_type=F32)

    def lane_tiles(s):
        return [s[:, c * LANES:(c + 1) * LANES] for c in range(tq // LANES)]

    def softmax_pv(kc, slot, masked):
        blocks = [slice(rb * row_block, (rb + 1) * row_block) for rb in range(rows // row_block)]
        for rb, blk in enumerate(blocks):
            s = s_s[slot, blk, :]
            if masked:
                qt = _mod_pow2(rb * row_block + lax.broadcasted_iota(jnp.int32, (row_block, tq), 0), tq)
                kpos = lax.broadcasted_iota(jnp.int32, (row_block, tq), 1)
                s = jnp.where(kpos <= qt, s, -jnp.inf)
                s_s[slot, blk, :] = s
            mx = functools.reduce(jnp.maximum, lane_tiles(s))
            m_prev = m_s[blk, :]
            m_new = jnp.maximum(m_prev, jnp.max(mx, axis=-1, keepdims=True))
            a_s[blk, :] = jnp.exp2(m_prev - m_new)
            m_s[blk, :] = m_new
        for blk in blocks:
            m_new = m_s[blk, :]
            ps = [jnp.exp2(t - m_new) for t in lane_tiles(s_s[slot, blk, :])]
            l_s[blk, :] = a_s[blk, :] * l_s[blk, :] + functools.reduce(jnp.add, ps)
            p_s[blk, :] = jnp.concatenate(ps, axis=-1).astype(BF16)
        pv = jnp.dot(p_s[...], kf_ref[keys(kc), 0:KV_RANK], preferred_element_type=F32)
        alpha = a_s[...]
        acc_s[...] = acc_s[...] * jnp.concatenate([alpha] * (KV_RANK // LANES), axis=-1) + pv

    scores(0, 0)

    def pair(i, carry):
        scores(2 * i + 1, 1)
        softmax_pv(2 * i, 0, False)
        scores(2 * i + 2, 0)
        softmax_pv(2 * i + 1, 1, False)
        return carry

    lax.fori_loop(0, qi // 2, pair, 0)

    @pl.when(qi % 2 == 1)
    def _():
        scores(qi, 1)
        softmax_pv(qi - 1, 0, False)
        softmax_pv(qi, 1, True)

    @pl.when(qi % 2 == 0)
    def _():
        softmax_pv(qi, 0, True)

    o = (acc_s[...] * (1.0 / jnp.sum(l_s[...], axis=-1, keepdims=True))).astype(BF16)
    outs = [jnp.dot(o[h * tq:(h + 1) * tq, :], wuv_ref[h], preferred_element_type=F32) for h in range(B_HEADS)]
    o_ref[...] = jnp.concatenate(outs, axis=-1).astype(o_ref.dtype)


def _mla_prompt(qn, qr, kfull, wuk, wuv, tq):
    B, T, _ = qn.shape
    scale = (B_NOPE_DIM + B_ROPE_DIM) ** -0.5 * math.log2(math.e)
    rows = B_HEADS * tq
    tile = lambda c: pl.BlockSpec((None, tq, c), lambda b, i: (b, i, 0))
    stat = pltpu.VMEM((rows, LANES), F32)
    return pl.pallas_call(
        functools.partial(_mla_prompt_kernel, tq=tq, scale=scale, row_block=min(128, tq)),
        grid=(B, T // tq),
        in_specs=[tile(A_WIDTH), tile(2 * LANES), pl.BlockSpec((None, T, KFULL), lambda b, i: (b, 0, 0)),
                  _const_spec((B_HEADS, B_NOPE_DIM, KV_RANK)), _const_spec((B_HEADS, KV_RANK, B_V_DIM))],
        out_specs=tile(B_WIDTH),
        out_shape=jax.ShapeDtypeStruct((B, T, B_WIDTH), BF16),
        scratch_shapes=[pltpu.VMEM((rows, KFULL), BF16), pltpu.VMEM((2, rows, tq), F32),
                        pltpu.VMEM((rows, tq), BF16), stat, stat, stat, pltpu.VMEM((rows, KV_RANK), F32)],
        compiler_params=_cparams(("arbitrary", "arbitrary")),
        name="mla_prompt",
    )(qn, qr, kfull, wuk, wuv)


def _mla_decode_kernel(pt_ref, qn_ref, qr_ref, kfn_ref, wuk_ref, wuv_ref, ckv_hbm, krt_hbm,
                       o_ref, ckv_buf, krt_buf, kv16, sem, *, layer, n_pages, T, scale, n_chunks):
    i = pl.program_id(0)
    n = pl.num_programs(0)
    L = n_pages * PAGE_SIZE

    def page_copies(sl, p, page):
        rows = pl.ds(pl.multiple_of(p * PAGE_SIZE, PAGE_SIZE), PAGE_SIZE)
        return (pltpu.make_async_copy(ckv_hbm.at[layer, page], ckv_buf.at[sl, rows], sem.at[0, sl]),
                pltpu.make_async_copy(krt_hbm.at[layer, page], krt_buf.at[sl, :, rows], sem.at[1, sl]))

    def issue(bb, sl):
        def body(p, carry):
            for cp in page_copies(sl, p, pt_ref[bb, p]):
                cp.start()
            return carry
        lax.fori_loop(0, n_pages, body, 0, unroll=math.gcd(n_pages, 8))

    def wait(sl):
        for p in range(n_pages):
            for cp in page_copies(sl, p, 0):
                cp.wait()

    def softmax_parts(s, values):
        m = jnp.max(s, axis=-1, keepdims=True)
        p = jnp.exp(s - m)
        return m, jnp.sum(p, axis=-1, keepdims=True), jnp.dot(p.astype(BF16), values, preferred_element_type=F32)

    def attend(sl):
        qr = qr_ref[sl]
        kfn = kfn_ref[sl]
        parts = _build_queries(qn_ref[sl], qr, wuk_ref, scale, T)
        qfull = jnp.concatenate(parts, axis=0)
        qlat = qfull[:, 0:KV_RANK]
        qrs = (qr.astype(F32) * scale)
        qrope = jnp.concatenate(
            [jnp.concatenate([qrs[:, h * HALF_ROPE:(h + 1) * HALF_ROPE],
                              qrs[:, LANES + h * HALF_ROPE:LANES + (h + 1) * HALF_ROPE]], axis=-1)
             for h in range(B_HEADS)], axis=0).astype(BF16)
        lc = L // n_chunks
        stats = []
        for c in range(n_chunks):
            rows = slice(c * lc, (c + 1) * lc)
            kv16[rows, :] = ckv_buf[sl, rows, :].astype(BF16)
            s = (lax.dot_general(qlat, kv16[rows, :], _NT, preferred_element_type=F32)
                 + jnp.dot(qrope, krt_buf[sl, :, rows].astype(BF16), preferred_element_type=F32))
            stats.append(softmax_parts(s, kv16[rows, :]))
        s_new = lax.dot_general(qfull, kfn, _NT, preferred_element_type=F32)
        nrows = B_HEADS * T
        qt = _mod_pow2(lax.broadcasted_iota(jnp.int32, (nrows, T), 0), T)
        kt = lax.broadcasted_iota(jnp.int32, (nrows, T), 1)
        stats.append(softmax_parts(jnp.where(kt <= qt, s_new, -jnp.inf), kfn[:, 0:KV_RANK]))
        m = functools.reduce(jnp.maximum, [st[0] for st in stats])
        weights = [jnp.exp(st[0] - m) for st in stats]
        l = functools.reduce(jnp.add, [w * st[1] for w, st in zip(weights, stats)])
        o = functools.reduce(jnp.add, [w * st[2] for w, st in zip(weights, stats)])
        o = (o / l).astype(BF16)
        outs = [jnp.dot(o[h * T:(h + 1) * T, :], wuv_ref[h], preferred_element_type=F32) for h in range(B_HEADS)]
        o_ref[sl] = jnp.concatenate(outs, axis=-1).astype(o_ref.dtype)

    @pl.when(i == 0)
    def _():
        issue(0, 0)

    issue(2 * i + 1, 1)
    wait(0)
    attend(0)

    @pl.when(i + 1 < n)
    def _():
        issue(2 * i + 2, 0)

    wait(1)
    attend(1)


def _mla_decode(page_table, qn, qr, kfn, wuk, wuv, cache_kv, cache_krt, layer):
    B, T, _ = qn.shape
    assert B % 2 == 0
    n_pages = page_table.shape[1]
    L = n_pages * PAGE_SIZE
    scale = (B_NOPE_DIM + B_ROPE_DIM) ** -0.5
    tile = lambda c: pl.BlockSpec((2, T, c), lambda b, pt: (b, 0, 0))
    cst = lambda shape: pl.BlockSpec(shape, lambda b, pt: (0,) * len(shape))
    grid_spec = pltpu.PrefetchScalarGridSpec(
        num_scalar_prefetch=1,
        grid=(B // 2,),
        in_specs=[tile(A_WIDTH), tile(2 * LANES), tile(KFULL),
                  cst((B_HEADS, B_NOPE_DIM, KV_RANK)), cst((B_HEADS, KV_RANK, B_V_DIM)),
                  pl.BlockSpec(memory_space=pl.ANY), pl.BlockSpec(memory_space=pl.ANY)],
        out_specs=tile(B_WIDTH),
        scratch_shapes=[pltpu.VMEM((2, L, KV_RANK), F32), pltpu.VMEM((2, B_ROPE_DIM, L), F32),
                        pltpu.VMEM((L, KV_RANK), BF16), pltpu.SemaphoreType.DMA((2, 2))],
    )
    return pl.pallas_call(
        functools.partial(_mla_decode_kernel, layer=layer, n_pages=n_pages, T=T, scale=scale,
                          n_chunks=math.gcd(n_pages, 4)),
        grid_spec=grid_spec,
        out_shape=jax.ShapeDtypeStruct((B, T, B_WIDTH), BF16),
        compiler_params=_cparams(("arbitrary",)),
        name="mla_decode",
    )(page_table, qn, qr, kfn, wuk, wuv, cache_kv, cache_krt)


def _pool_kernel(pc_ref, st_ref, wbd_ref, scale_ref, y_ref, newst_ref, e_s, s2_s, s4_s, s8_s, *, ts, T, pos0):
    L = T * ts
    P = (POOL_PREFIX + 1) * ts
    pc = pc_ref[...].reshape(L, C_WIDTH)
    e_s[P - POOL_PREFIX * ts:P, :] = st_ref[...].reshape(POOL_PREFIX * ts, C_WIDTH)
    e_s[P:P + L, :] = pc
    newst_ref[...] = e_s[P + L - POOL_PREFIX * ts:P + L, :].reshape(newst_ref.shape)
    s2_s[P - 14 * ts:P + L, :] = e_s[P - 14 * ts:P + L, :] + e_s[P - 15 * ts:P + L - ts, :]
    s4_s[P - 12 * ts:P + L, :] = s2_s[P - 12 * ts:P + L, :] + s2_s[P - 14 * ts:P + L - 2 * ts, :]
    s8_s[P - 8 * ts:P + L, :] = s4_s[P - 8 * ts:P + L, :] + s4_s[P - 12 * ts:P + L - 4 * ts, :]
    s16 = s8_s[P:P + L, :] + s8_s[P - 8 * ts:P + L - 8 * ts, :]
    lane = lax.broadcasted_iota(jnp.int32, (L, C_WIDTH), 1)
    grp = _div_pow2(lane, C_GROUP_DIM)
    win = jnp.where(grp == 0, s2_s[P:P + L, :],
                    jnp.where(grp == 1, s4_s[P:P + L, :], jnp.where(grp == 2, s8_s[P:P + L, :], s16)))
    wsize = jnp.where(grp == 0, 2, jnp.where(grp == 1, 4, jnp.where(grp == 2, 8, 16)))
    pos = pos0 + _div_pow2(lax.broadcasted_iota(jnp.int32, (L, C_WIDTH), 0), ts)
    cnt = jnp.minimum(pos + 1, wsize).astype(F32)
    d = win / cnt - pc
    y = jnp.dot(d.astype(BF16), wbd_ref[...], preferred_element_type=F32) * scale_ref[...]
    y_ref[...] = y.astype(y_ref.dtype).reshape(y_ref.shape)


def _pool(tok, pc, pool_prev, wbd, pscale, pos0):
    ts, T = tok.ts, tok.T
    rows = (POOL_PREFIX + 1 + T) * ts
    if tok.time_major:
        grid, blk = (tok.B // tok.bb, 1), tok.spec(C_WIDTH)
    else:
        grid, blk = (tok.B, 1), pl.BlockSpec((None, T, C_WIDTH), lambda i, j: (i, 0, 0))
    return pl.pallas_call(
        functools.partial(_pool_kernel, ts=ts, T=T, pos0=pos0),
        grid=grid,
        in_specs=[blk, tok.state_spec(POOL_PREFIX, C_WIDTH), _const_spec((C_WIDTH, C_WIDTH)),
                  _const_spec((1, C_WIDTH))],
        out_specs=[blk, tok.state_spec(POOL_PREFIX, C_WIDTH)],
        out_shape=[jax.ShapeDtypeStruct(tok.shape(C_WIDTH), BF16),
                   jax.ShapeDtypeStruct(tok.state_shape(POOL_PREFIX, C_WIDTH), F32)],
        scratch_shapes=[pltpu.VMEM((rows, C_WIDTH), F32)] * 4,
        compiler_params=_cparams(("arbitrary", "arbitrary")),
        name="pool",
    )(pc, pool_prev, wbd, pscale)


def _ffn_kernel(x_ref, ya_ref, g_ref, yb_ref, yc_ref, cst_ref, wout_ref, gffn_ref, wup_ref, cw_ref, cb_ref,
                wdn_ref, gfin_ref, xo_ref, cnew_ref, gbuf_s, *, tm, ts, final):
    taps = CONV_WIDTH - 1
    pad = max(SUBLANES, taps * ts)
    it = pl.program_id(1)

    @pl.when(it == 0)
    def _():
        gbuf_s[pad - taps * ts:pad, :] = cst_ref[...].reshape(taps * ts, D_FF)

    x = x_ref[...].reshape(tm, D_MODEL)
    ya = (ya_ref[...].reshape(tm, A_WIDTH) * g_ref[...].reshape(tm, A_WIDTH)).astype(BF16)
    ycat = jnp.concatenate([ya, yb_ref[...].reshape(tm, B_WIDTH), yc_ref[...].reshape(tm, C_WIDTH)], axis=-1)
    x1 = x + jnp.dot(ycat, wout_ref[...], preferred_element_type=F32)
    xn = _rms(x1, gffn_ref[...]).astype(BF16)
    acc = jnp.zeros((tm, D_MODEL), F32)
    for c in range(D_FF // FF_CHUNK):
        lo = c * FF_CHUNK
        up = jnp.dot(xn, wup_ref[:, lo:lo + FF_CHUNK], preferred_element_type=F32)
        gate = jnp.dot(xn, wup_ref[:, D_FF + lo:D_FF + lo + FF_CHUNK], preferred_element_type=F32)
        gbuf_s[pad:pad + tm, lo:lo + FF_CHUNK] = gate
        gc = cb_ref[:, lo:lo + FF_CHUNK] + cw_ref[2:3, lo:lo + FF_CHUNK] * gate
        for j in range(taps):
            off = pad - (taps - j) * ts
            gc = gc + cw_ref[j:j + 1, lo:lo + FF_CHUNK] * gbuf_s[off:off + tm, lo:lo + FF_CHUNK]
        hh = (gc * _sigmoid(gc) * up).astype(BF16)
        acc = acc + jnp.dot(hh, wdn_ref[lo:lo + FF_CHUNK, :], preferred_element_type=F32)
    last = gbuf_s[pad + tm - taps * ts:pad + tm, :]
    cnew_ref[...] = last.reshape(cnew_ref.shape)
    gbuf_s[pad - taps * ts:pad, :] = last
    x2 = x1 + acc
    if final:
        x2 = _rms(x2, gfin_ref[...])
    xo_ref[...] = x2.reshape(xo_ref.shape)


def _ffn(tok, x, ya, g, yb, yc, conv_prev, p, gfin, final):
    tm, ts = tok.tm, tok.ts
    taps = CONV_WIDTH - 1
    pad = max(SUBLANES, taps * ts)
    return pl.pallas_call(
        functools.partial(_ffn_kernel, tm=tm, ts=ts, final=final),
        grid=tok.grid,
        in_specs=[tok.spec(D_MODEL), tok.spec(A_WIDTH), tok.spec(A_WIDTH), tok.spec(B_WIDTH), tok.spec(C_WIDTH),
                  tok.state_spec(taps, D_FF), _const_spec((D_MODEL, D_MODEL)), _const_spec((1, D_MODEL)),
                  _const_spec((D_MODEL, 2 * D_FF)), _const_spec((CONV_WIDTH, D_FF)), _const_spec((1, D_FF)),
                  _const_spec((D_FF, D_MODEL)), _const_spec((1, D_MODEL))],
        out_specs=[tok.spec(D_MODEL), tok.state_spec(taps, D_FF)],
        out_shape=[jax.ShapeDtypeStruct(tok.shape(D_MODEL), F32),
                   jax.ShapeDtypeStruct(tok.state_shape(taps, D_FF), F32)],
        scratch_shapes=[pltpu.VMEM((pad + tm, D_FF), F32)],
        compiler_params=_cparams(("arbitrary", "arbitrary")),
        name="outproj_ffn",
    )(x, ya, g, yb, yc, conv_prev, p["w_out"], p["g_ffn"], p["w_up"], p["conv_w"], p["conv_b"], p["w_down"], gfin)


def _ext_columns():
    qn0 = A_COLS
    qr0 = qn0 + B_HEADS * B_NOPE_DIM
    ckv0 = qr0 + B_HEADS * B_ROPE_DIM
    kr0 = ckv0 + KV_RANK
    pc0 = A_COLS + B_COLS
    cols = list(range(A_COLS)) + list(range(qn0, qr0)) + list(range(ckv0, kr0)) + list(range(pc0, pc0 + C_WIDTH))
    for half in range(2):
        blk = []
        for h in range(B_HEADS):
            blk += [qr0 + h * B_ROPE_DIM + half * HALF_ROPE + i for i in range(HALF_ROPE)]
        blk += [kr0 + half * HALF_ROPE + i for i in range(HALF_ROPE)]
        cols += blk + [-1] * (LANES - len(blk))
    for half in range(2):
        blk = [kr0 + half * HALF_ROPE + i for _ in range(B_HEADS) for i in range(HALF_ROPE)]
        cols += blk + [-1] * (LANES - len(blk))
    assert len(cols) == IN_EXT
    return np.asarray(cols, np.int32)


def _layer_params(l, norm_mix_g, w_in, mu_shift, decay_w0, decay_w2, iclr_a0, iclr_a2, gate_g2, k_k, k_a, r_k,
                  lnx_w, lnx_b, kv_norm_g, w_uk, w_uv, pool_w, pool_scale, w_out, norm_ffn_g, w_up, conv_w,
                  conv_b, w_down):
    cols = _ext_columns()
    w_ext = jnp.where(cols[None, :] >= 0, w_in[l][:, np.maximum(cols, 0)], 0.0).astype(BF16)
    lora_cat = jnp.zeros((LANES, 2 * A_WIDTH), F32)
    lora_cat = lora_cat.at[0:DECAY_LORA, 0:A_WIDTH].set(decay_w2[l])
    lora_cat = lora_cat.at[DECAY_LORA:DECAY_LORA + ICLR_LORA, A_WIDTH:].set(iclr_a2[l])
    head = np.arange(A_WIDTH) // A_HEAD_DIM
    head_ones = jnp.asarray((head[:, None] == head[None, :]).astype(np.float32), BF16)
    wbd = jnp.zeros((C_WIDTH, C_WIDTH), F32)
    for gi in range(C_GROUPS):
        sl = slice(gi * C_GROUP_DIM, (gi + 1) * C_GROUP_DIM)
        wbd = wbd.at[sl, sl].set(pool_w[l, gi])
    row = lambda v: v.reshape(1, -1)
    return dict(
        g_mix=row(norm_mix_g[l]), w_ext=w_ext, kv_g=row(kv_norm_g[l]),
        mu=row(mu_shift[l]), lora_cat=lora_cat.astype(BF16), w0=row(decay_w0[l]), a0=row(iclr_a0[l]),
        g2=gate_g2[l].astype(BF16), k_k=row(k_k[l]), k_a=row(k_a[l]), head_ones=head_ones,
        r_k=r_k[l], lnx_w=lnx_w[l].reshape(A_HEADS, A_HEAD_DIM), lnx_b=lnx_b[l].reshape(A_HEADS, A_HEAD_DIM),
        wuk=jnp.transpose(w_uk[l], (1, 2, 0)).astype(BF16), wuv=jnp.transpose(w_uv[l], (1, 0, 2)).astype(BF16),
        pool_wbd=wbd.astype(BF16), pool_scale=row(pool_scale[l]),
        w_out=w_out[l].astype(BF16), g_ffn=row(norm_ffn_g[l]), w_up=w_up[l].astype(BF16),
        conv_w=conv_w[l], conv_b=row(conv_b[l]), w_down=w_down[l].astype(BF16),
    )


def _rope_tables(pos):
    inv = ROPE_BASE ** (-jnp.arange(0, B_ROPE_DIM, 2, dtype=F32) / B_ROPE_DIM)
    ang = pos.astype(F32)[:, None] * inv[None, :]
    reps = LANES // HALF_ROPE
    return jnp.tile(jnp.cos(ang), (1, reps)), jnp.tile(jnp.sin(ang), (1, reps))


def _to_scan(tok, x):
    H, N = A_HEADS, A_HEAD_DIM
    if tok.time_major:
        x = x.reshape(tok.T, tok.B, H, N)
        return jnp.transpose(x, (2, 0, 3, 1))
    x = x.reshape(tok.B, tok.T, H, N)
    x = jnp.transpose(x, (1, 3, 0, 2)).reshape(tok.T, N, tok.B * H)
    return jnp.pad(x, ((0, 0), (0, 0), (0, LANES - tok.B * H)))[None]


def _from_scan(tok, y):
    H, N = A_HEADS, A_HEAD_DIM
    if tok.time_major:
        return jnp.transpose(y, (1, 3, 0, 2)).reshape(tok.T, tok.B, H * N)
    y = y[0, :, :, :tok.B * H].reshape(tok.T, N, tok.B, H)
    return jnp.transpose(y, (2, 0, 3, 1)).reshape(tok.B, tok.T, H * N)


def _state_to_scan(tok, s):
    if tok.time_major:
        return jnp.transpose(s, (1, 3, 2, 0))
    B, H, N, _ = s.shape
    s = jnp.transpose(s, (3, 2, 0, 1)).reshape(N, N, B * H)
    return jnp.pad(s, ((0, 0), (0, 0), (0, LANES - B * H)))[None]


def _state_from_scan(tok, s):
    if tok.time_major:
        return jnp.transpose(s, (3, 0, 2, 1))
    N = A_HEAD_DIM
    s = s[0, :, :, :tok.B * A_HEADS].reshape(N, N, tok.B, A_HEADS)
    return jnp.transpose(s, (2, 3, 1, 0))


def _head_consts(tok, p):
    def lay(v):
        if tok.time_major:
            return jnp.broadcast_to(v[:, :, None], (A_HEADS, A_HEAD_DIM, LANES))
        t = jnp.tile(v.T, (1, tok.B))
        return jnp.pad(t, ((0, 0), (0, LANES - tok.B * A_HEADS)))[None]
    return lay(p["lnx_w"]), lay(p["lnx_b"]), lay(p["r_k"])


def _layer(tok, x, cos, sin, pos0, shift_prev, wkv_prev, pool_prev, conv_prev, p, gfin, final, scan_tc, attend):
    pa, qn, kvlat, pc, qr, kfull, krope = _inproj(tok, x, p["g_mix"], p["w_ext"], p["kv_g"], cos, sin)
    r, w, k, v, a, b, g, shift_new = _rwkv_prep(tok, pa, shift_prev, p)
    lnw, lnb, rk = _head_consts(tok, p)
    y, s_new = _rwkv_scan(*(_to_scan(tok, t) for t in (w, a, b, k, v, r)), _state_to_scan(tok, wkv_prev),
                          lnw, lnb, rk, scan_tc)
    ya = _from_scan(tok, y)
    yb = attend(qn, qr, kfull)
    yc, pool_new = _pool(tok, pc, pool_prev, p["pool_wbd"], p["pool_scale"], pos0)
    x, conv_new = _ffn(tok, x, ya, g, yb, yc, conv_prev, p, gfin, final)
    return x, (kvlat, krope, _state_from_scan(tok, s_new), shift_new, pool_new, conv_new)


def kernel(x_prompt, x_sample, cache_kv_latent, cache_k_rope, page_table, state_wkv, state_shift, state_pool,
           state_conv, norm_mix_g, w_in, mu_shift, decay_w0, decay_w2, iclr_a0, iclr_a2, gate_g2, k_k, k_a, r_k,
           lnx_w, lnx_b, kv_norm_g, w_uk, w_uv, pool_w, pool_scale, w_out, norm_ffn_g, w_up, conv_w, conv_b,
           w_down, norm_final_g):
    Bp, Tp, _ = x_prompt.shape
    Bs, Ts, _ = x_sample.shape
    past_len = page_table.shape[1] * PAGE_SIZE
    tokp = _Tok(False, Bp, Tp, min(512, Tp))
    toks = _Tok(True, Bs, Ts, min(32, Bs))
    cos_p, sin_p = _rope_tables(jnp.arange(Tp, dtype=jnp.int32))
    pos_s = past_len + jnp.repeat(jnp.arange(Ts, dtype=jnp.int32), toks.bb)
    cos_s, sin_s = _rope_tables(pos_s)
    gfin = norm_final_g.reshape(1, -1)
    tm_first = lambda t: jnp.swapaxes(t, 0, 1)
    cache_krt = jnp.swapaxes(cache_k_rope, 2, 3)

    xp = x_prompt
    xs = tm_first(x_sample)
    zeros = lambda *shape: jnp.zeros(shape, F32)
    outs_p, outs_s = [], []
    for l in range(DEPTH):
        p = _layer_params(l, norm_mix_g, w_in, mu_shift, decay_w0, decay_w2, iclr_a0, iclr_a2, gate_g2, k_k, k_a,
                          r_k, lnx_w, lnx_b, kv_norm_g, w_uk, w_uv, pool_w, pool_scale, w_out, norm_ffn_g, w_up,
                          conv_w, conv_b, w_down)
        final = l == DEPTH - 1

        attend_p = lambda qn, qr, kfull: _mla_prompt(qn, qr, kfull, p["wuk"], p["wuv"], min(256, Tp))
        xp, st_p = _layer(tokp, xp, cos_p, sin_p, 0, zeros(Bp, 1, A_COLS),
                          zeros(Bp, A_HEADS, A_HEAD_DIM, A_HEAD_DIM), zeros(Bp, POOL_PREFIX, C_WIDTH),
                          zeros(Bp, CONV_WIDTH - 1, D_FF), p, gfin, final, min(16, Tp), attend_p)

        def attend_s(qn, qr, kfull):
            yb = _mla_decode(page_table, tm_first(qn), tm_first(qr), tm_first(kfull), p["wuk"], p["wuv"],
                             cache_kv_latent, cache_krt, l)
            return tm_first(yb)
        xs, st_s = _layer(toks, xs, cos_s, sin_s, past_len, state_shift[l][None], state_wkv[l],
                          tm_first(state_pool[l]), tm_first(state_conv[l]), p, gfin, final, Ts, attend_s)
        outs_p.append(st_p)
        outs_s.append(st_s)

    stack = lambda outs, i, f: jnp.stack([f(o[i]) for o in outs], axis=0)
    ident = lambda t: t
    return (xp, tm_first(xs),
            stack(outs_p, 0, ident), stack(outs_p, 1, ident), stack(outs_p, 2, ident),
            stack(outs_p, 3, lambda t: t[:, 0]), stack(outs_p, 4, ident), stack(outs_p, 5, ident),
            stack(outs_s, 0, tm_first), stack(outs_s, 1, tm_first), stack(outs_s, 2, ident),
            stack(outs_s, 3, lambda t: t[0]), stack(outs_s, 4, tm_first), stack(outs_s, 5, tm_first))
```

```python
import functools
import math

import jax
import jax.numpy as jnp
import numpy as np
from jax import lax
from jax.experimental import pallas as pl
from jax.experimental.pallas import tpu as pltpu

F32 = jnp.float32
BF16 = jnp.bfloat16

D_MODEL = 1024
DEPTH = 2
PAGE_SIZE = 128
A_HEADS = 6
A_HEAD_DIM = 64
A_WIDTH = A_HEADS * A_HEAD_DIM
DECAY_LORA = 64
ICLR_LORA = 64
GATE_LORA = 128
A_COLS = 3 * A_WIDTH + DECAY_LORA + ICLR_LORA + GATE_LORA
B_HEADS = 6
B_NOPE_DIM = 64
B_ROPE_DIM = 32
B_V_DIM = 64
B_WIDTH = B_HEADS * B_V_DIM
KV_RANK = 256
ROPE_BASE = 10000.0
B_COLS = B_HEADS * (B_NOPE_DIM + B_ROPE_DIM) + KV_RANK + B_ROPE_DIM
C_GROUPS = 4
C_GROUP_DIM = 64
C_WIDTH = C_GROUPS * C_GROUP_DIM
POOL_WINDOWS = (2, 4, 8, 16)
POOL_PREFIX = 15
D_FF = 2816
CONV_WIDTH = 3
NORM_EPS = 1e-6
GN_EPS = A_HEAD_DIM * 1e-5

LANES = 128
SUBLANES = 8
HALF_ROPE = B_ROPE_DIM // 2
IN_EXT = A_COLS + B_HEADS * B_NOPE_DIM + KV_RANK + C_WIDTH + 4 * LANES
OFF_QN = A_COLS
OFF_CKV = OFF_QN + B_HEADS * B_NOPE_DIM
OFF_PC = OFF_CKV + KV_RANK
OFF_ROPE = OFF_PC + C_WIDTH
KFULL = KV_RANK + 2 * LANES
FF_CHUNK = D_FF // 2
VMEM_LIMIT = 56 * 1024 * 1024


def _cparams(sem):
    return pltpu.CompilerParams(dimension_semantics=sem, vmem_limit_bytes=VMEM_LIMIT)


def _const_spec(shape):
    nd = len(shape)
    return pl.BlockSpec(shape, lambda *_: (0,) * nd, pipeline_mode=pl.Buffered(1))


class _Tok:
    def __init__(self, time_major, B, T, tile):
        self.time_major, self.B, self.T = time_major, B, T
        if time_major:
            self.bb = tile
            self.tm, self.ts = T * tile, tile
            self.grid = (B // tile, 1)
        else:
            self.tm, self.ts = tile, 1
            self.grid = (B, T // tile)
        self.n = B * T

    def shape(self, C):
        return (self.T, self.B, C) if self.time_major else (self.B, self.T, C)

    def spec(self, C):
        if self.time_major:
            return pl.BlockSpec((self.T, self.bb, C), lambda i, j: (0, i, 0))
        return pl.BlockSpec((None, self.tm, C), lambda i, j: (i, j, 0))

    def state_shape(self, steps, C):
        return (steps, self.B, C) if self.time_major else (self.B, steps, C)

    def state_spec(self, steps, C):
        if self.time_major:
            return pl.BlockSpec((steps, self.bb, C), lambda i, j: (0, i, 0))
        return pl.BlockSpec((None, steps, C), lambda i, j: (i, 0, 0))


def _rms(x, g):
    return x * lax.rsqrt(jnp.mean(x * x, axis=-1, keepdims=True) + NORM_EPS) * g


def _sigmoid(x):
    return 1.0 / (1.0 + jnp.exp(-x))


def _div_pow2(x, d):
    assert d & (d - 1) == 0
    return lax.shift_right_logical(x, int(math.log2(d)))


def _mod_pow2(x, d):
    assert d & (d - 1) == 0
    return lax.bitwise_and(x, d - 1)


def _inproj_kernel(x_ref, g_ref, w_ref, kvg_ref, cos_ref, sin_ref,
                   pa_ref, qn_ref, kvlat_ref, pc_ref, qr_ref, kfull_ref, krope_ref, *, tm):
    x = x_ref[...].reshape(tm, D_MODEL)
    xb = _rms(x, g_ref[...]).astype(BF16)

    def proj(lo, hi):
        return jnp.dot(xb, w_ref[:, lo:hi], preferred_element_type=F32)

    pa_ref[...] = proj(0, A_COLS).reshape(pa_ref.shape)
    qn_ref[...] = proj(OFF_QN, OFF_CKV).astype(BF16).reshape(qn_ref.shape)
    ckv = _rms(proj(OFF_CKV, OFF_PC), kvg_ref[...])
    kvlat_ref[...] = ckv.reshape(kvlat_ref.shape)
    pc_ref[...] = proj(OFF_PC, OFF_ROPE).reshape(pc_ref.shape)
    rr = proj(OFF_ROPE, IN_EXT)
    cos = cos_ref[...]
    sin = sin_ref[...]
    q1, q2 = rr[:, 0:LANES], rr[:, LANES:2 * LANES]
    k1, k2 = rr[:, 2 * LANES:3 * LANES], rr[:, 3 * LANES:4 * LANES]
    o1 = q1 * cos - q2 * sin
    o2 = q1 * sin + q2 * cos
    qr = jnp.concatenate([o1, o2], axis=-1).astype(BF16)
    qr_ref[...] = qr.reshape(qr_ref.shape)
    kf = jnp.concatenate([ckv, k1 * cos - k2 * sin, k1 * sin + k2 * cos], axis=-1).astype(BF16)
    kfull_ref[...] = kf.reshape(kfull_ref.shape)
    kr_lo = B_HEADS * HALF_ROPE
    krope = jnp.concatenate([o1[:, kr_lo:kr_lo + HALF_ROPE], o2[:, kr_lo:kr_lo + HALF_ROPE]], axis=-1)
    krope_ref[...] = krope.reshape(krope_ref.shape)


def _inproj(tok, x, g, w_ext, kvg, cos, sin):
    tm = tok.tm
    n_tab = cos.shape[0] // tm
    tab_spec = pl.BlockSpec((tm, LANES), lambda i, j: (j % n_tab, 0))
    widths = (A_COLS, A_WIDTH, KV_RANK, C_WIDTH, 2 * LANES, KFULL, B_ROPE_DIM)
    dtypes = (F32, BF16, F32, F32, BF16, BF16, F32)
    return pl.pallas_call(
        functools.partial(_inproj_kernel, tm=tm),
        grid=tok.grid,
        in_specs=[tok.spec(D_MODEL), _const_spec((1, D_MODEL)), _const_spec((D_MODEL, IN_EXT)),
                  _const_spec((1, KV_RANK)), tab_spec, tab_spec],
        out_specs=[tok.spec(c) for c in widths],
        out_shape=[jax.ShapeDtypeStruct(tok.shape(c), d) for c, d in zip(widths, dtypes)],
        compiler_params=_cparams(("arbitrary", "arbitrary")),
        name="inproj",
    )(x, g, w_ext, kvg, cos, sin)


def _rwkv_token_math(pa, prev, mu_ref, wcat_ref, w0_ref, a0_ref, g2_ref, kk_ref, ka_ref, ones_ref):
    s = pa + (prev - pa) * mu_ref[...]
    W = A_WIDTH
    r, k, v = s[:, 0:W], s[:, W:2 * W], s[:, 2 * W:3 * W]
    lo_in = s[:, 3 * W:3 * W + LANES]
    gd = s[:, 3 * W + LANES:3 * W + 2 * LANES]
    lane = lax.broadcasted_iota(jnp.int32, lo_in.shape, 1)
    lo_in = jnp.where(lane < DECAY_LORA, jnp.tanh(lo_in), lo_in)
    lo = jnp.dot(lo_in.astype(BF16), wcat_ref[...], preferred_element_type=F32)
    z = -(w0_ref[...] + lo[:, 0:W])
    softplus = jnp.maximum(z, 0.0) + jnp.log(1.0 + jnp.exp(-jnp.abs(z)))
    decay = jnp.exp(-jnp.exp(-softplus - 0.5))
    a = _sigmoid(a0_ref[...] + lo[:, W:2 * W])
    g = jnp.dot(_sigmoid(gd).astype(BF16), g2_ref[...], preferred_element_type=F32)
    kk = k * kk_ref[...]
    kk2 = kk * kk
    hi = kk2.astype(BF16)
    lo2 = (kk2 - hi.astype(F32)).astype(BF16)
    ones = ones_ref[...]
    ss = (jnp.dot(hi, ones, preferred_element_type=F32) + jnp.dot(lo2, ones, preferred_element_type=F32))
    kk = kk * lax.rsqrt(jnp.maximum(ss, 1e-24))
    kmod = k * (1.0 + (a - 1.0) * ka_ref[...])
    return r, decay, kmod, v, -kk, kk * a, g


def _rwkv_prep_kernel(pa_ref, st_ref, mu_ref, wcat_ref, w0_ref, a0_ref, g2_ref, kk_ref, ka_ref, ones_ref,
                      r_ref, w_ref, k_ref, v_ref, a_ref, b_ref, g_ref, newst_ref, buf_ref, *, tm, ts):
    pad = max(SUBLANES, ts)
    it = pl.program_id(1)

    @pl.when(it == 0)
    def _():
        buf_ref[pad - ts:pad, :] = st_ref[...].reshape(ts, A_COLS)

    pa = pa_ref[...].reshape(tm, A_COLS)
    buf_ref[pad:pad + tm, :] = pa
    prev = buf_ref[pad - ts:pad - ts + tm, :]
    last = pa[tm - ts:tm, :]
    buf_ref[pad - ts:pad, :] = last
    newst_ref[...] = last.reshape(newst_ref.shape)
    vals = _rwkv_token_math(pa, prev, mu_ref, wcat_ref, w0_ref, a0_ref, g2_ref, kk_ref, ka_ref, ones_ref)
    for ref, val in zip((r_ref, w_ref, k_ref, v_ref, a_ref, b_ref, g_ref), vals):
        ref[...] = val.reshape(ref.shape)


def _prompt_lane(b, h, nb):
    return (h % 2) * (A_HEADS // 2) * nb + (h // 2) * nb + b


def _rwkv_prep_scan_kernel(pa_ref, st_ref, mu_ref, wcat_ref, w0_ref, a0_ref, g2_ref, kk_ref, ka_ref, ones_ref,
                           r_ref, w_ref, k_ref, v_ref, a_ref, b_ref, g_ref, newst_ref, buf_ref, q_ref, *, nb, tt):
    it = pl.program_id(0)

    @pl.when(it == 0)
    def _():
        buf_ref[0:nb, :] = st_ref[...]

    for t in range(tt):
        buf_ref[(t + 1) * nb:(t + 2) * nb, :] = pa_ref[:, t, :]
    pa = buf_ref[nb:(tt + 1) * nb, :]
    prev = buf_ref[0:tt * nb, :]
    vals = _rwkv_token_math(pa, prev, mu_ref, wcat_ref, w0_ref, a0_ref, g2_ref, kk_ref, ka_ref, ones_ref)
    last = buf_ref[tt * nb:(tt + 1) * nb, :]
    buf_ref[0:nb, :] = last
    newst_ref[...] = last
    g = vals[6]
    for t in range(tt):
        g_ref[:, t, :] = g[t * nb:(t + 1) * nb, :]
    for qi in range(6):
        q_ref[qi] = vals[qi]
    outs = (r_ref, w_ref, k_ref, v_ref, a_ref, b_ref)
    half = A_HEADS // 2
    win = half * nb
    N = A_HEAD_DIM

    def pair(pi, carry):
        for qi, o_ref in enumerate(outs):
            rows = q_ref[qi, pl.ds(pl.multiple_of(pi * 2 * nb, 2 * nb), 2 * nb), :]
            pieces = [rows[s * nb:(s + 1) * nb, hp * LANES:(hp + 1) * LANES] for s in range(2) for hp in range(half)]
            m = jnp.concatenate(pieces + [jnp.zeros((LANES - 2 * win, LANES), F32)], axis=0)
            mt = m.T
            unused = jnp.full((N, LANES - 2 * win), 1.0 if o_ref is w_ref else 0.0, F32)
            for s in range(2):
                tile = jnp.concatenate([mt[0:N, s * win:(s + 1) * win], mt[N:2 * N, s * win:(s + 1) * win], unused],
                                       axis=1)
                o_ref[0, pi * 2 + s] = tile
        return carry

    lax.fori_loop(0, tt // 2, pair, 0)


def _rwkv_prep_scan(pa, shift_prev, p, tt):
    nb, T, _ = pa.shape
    assert 2 * (A_HEADS // 2) * nb <= LANES and tt % 2 == 0
    vec = _const_spec((1, A_WIDTH))
    seq = pl.BlockSpec((1, tt, A_HEAD_DIM, LANES), lambda i: (0, i, 0, 0))
    st = pl.BlockSpec((nb, A_COLS), lambda i: (0, 0))
    return pl.pallas_call(
        functools.partial(_rwkv_prep_scan_kernel, nb=nb, tt=tt),
        grid=(T // tt,),
        in_specs=[pl.BlockSpec((nb, tt, A_COLS), lambda i: (0, i, 0)), st, _const_spec((1, A_COLS)),
                  _const_spec((LANES, 2 * A_WIDTH)), vec, vec, _const_spec((GATE_LORA, A_WIDTH)), vec, vec,
                  _const_spec((A_WIDTH, A_WIDTH))],
        out_specs=[seq] * 6 + [pl.BlockSpec((nb, tt, A_WIDTH), lambda i: (0, i, 0)), st],
        out_shape=[jax.ShapeDtypeStruct((1, T, A_HEAD_DIM, LANES), F32)] * 6
        + [jax.ShapeDtypeStruct((nb, T, A_WIDTH), F32), jax.ShapeDtypeStruct((nb, A_COLS), F32)],
        scratch_shapes=[pltpu.VMEM(((tt + 1) * nb, A_COLS), F32), pltpu.VMEM((6, tt * nb, A_WIDTH), F32)],
        compiler_params=_cparams(("arbitrary",)),
        name="rwkv_prep_scan",
    )(pa, shift_prev, p["mu"], p["lora_cat"], p["w0"], p["a0"], p["g2"], p["k_k"], p["k_a"], p["head_ones"])


def _rwkv_prep(tok, pa, shift_prev, p):
    tm, ts = tok.tm, tok.ts
    pad = max(SUBLANES, ts)
    vec = _const_spec((1, A_WIDTH))
    return pl.pallas_call(
        functools.partial(_rwkv_prep_kernel, tm=tm, ts=ts),
        grid=tok.grid,
        in_specs=[tok.spec(A_COLS), tok.state_spec(1, A_COLS), _const_spec((1, A_COLS)),
                  _const_spec((LANES, 2 * A_WIDTH)), vec, vec, _const_spec((GATE_LORA, A_WIDTH)), vec, vec,
                  _const_spec((A_WIDTH, A_WIDTH))],
        out_specs=[tok.spec(A_WIDTH)] * 7 + [tok.state_spec(1, A_COLS)],
        out_shape=[jax.ShapeDtypeStruct(tok.shape(A_WIDTH), F32)] * 7
        + [jax.ShapeDtypeStruct(tok.state_shape(1, A_COLS), F32)],
        scratch_shapes=[pltpu.VMEM((pad + tm, A_COLS), F32)],
        compiler_params=_cparams(("arbitrary", "arbitrary")),
        name="rwkv_prep",
    )(pa, shift_prev, p["mu"], p["lora_cat"], p["w0"], p["a0"], p["g2"], p["k_k"], p["k_a"], p["head_ones"])


def _rwkv_scan_kernel(w_ref, a_ref, b_ref, k_ref, v_ref, r_ref, s0_ref, lnw_ref, lnb_ref, rk_ref,
                      y_ref, sT_ref, S_ref, op_ref, *, tc):
    N = A_HEAD_DIM
    it = pl.program_id(1)

    @pl.when(it == 0)
    def _():
        S_ref[...] = s0_ref[0]

    def step(t, p_prev):
        w = w_ref[0, t]
        r = r_ref[0, t]
        k = k_ref[0, t]
        v = v_ref[0, t]
        p_new = p_prev * w
        inv = 1.0 / p_new
        rt = r * p_new
        bt = b_ref[0, t] * inv
        kt = k * inv
        op_ref[0] = a_ref[0, t] * p_prev
        op_ref[1] = rt
        op_ref[2] = bt
        op_ref[3] = kt
        sa = [jnp.zeros((N, LANES), F32) for _ in range(2)]
        yp = [jnp.zeros((N, LANES), F32) for _ in range(2)]
        for j in range(N):
            Zj = S_ref[j]
            sa[j % 2] = sa[j % 2] + Zj * op_ref[0, pl.ds(j, 1), :]
            yp[j % 2] = yp[j % 2] + Zj * op_ref[1, pl.ds(j, 1), :]
        sa = sa[0] + sa[1]
        yp = yp[0] + yp[1]
        for j in range(N):
            S_ref[j] = S_ref[j] + sa * op_ref[2, pl.ds(j, 1), :] + v * op_ref[3, pl.ds(j, 1), :]
        br = jnp.sum(bt * rt, axis=0, keepdims=True)
        kr = jnp.sum(kt * rt, axis=0, keepdims=True)
        y = yp + sa * br + v * kr
        mu = jnp.mean(y, axis=0, keepdims=True)
        d = y - mu
        var = jnp.mean(d * d, axis=0, keepdims=True)
        yn = d * lax.rsqrt(var + GN_EPS) * lnw_ref[0] + lnb_ref[0]
        bonus = jnp.sum(r * k * rk_ref[0], axis=0, keepdims=True)
        y_ref[0, t] = yn + bonus * v
        return p_new

    op_ref[0] = lax.fori_loop(0, tc, step, jnp.ones((N, LANES), F32))
    for j in range(N):
        S_ref[j] = S_ref[j] * op_ref[0, pl.ds(j, 1), :]

    @pl.when(it == pl.num_programs(1) - 1)
    def _():
        sT_ref[0] = S_ref[...]


def _rwkv_scan(w, a, b, k, v, r, s0, lnw, lnb, rk, tc):
    G, T, N, _ = w.shape
    seq = pl.BlockSpec((1, tc, N, LANES), lambda g, i: (g, i, 0, 0))
    st = pl.BlockSpec((1, N, N, LANES), lambda g, i: (g, 0, 0, 0))
    cst = pl.BlockSpec((1, N, LANES), lambda g, i: (g, 0, 0))
    return pl.pallas_call(
        functools.partial(_rwkv_scan_kernel, tc=tc),
        grid=(G, T // tc),
        in_specs=[seq] * 6 + [st, cst, cst, cst],
        out_specs=[seq, st],
        out_shape=[jax.ShapeDtypeStruct((G, T, N, LANES), F32), jax.ShapeDtypeStruct((G, N, N, LANES), F32)],
        scratch_shapes=[pltpu.VMEM((N, N, LANES), F32), pltpu.VMEM((4, N, LANES), F32)],
        compiler_params=_cparams(("arbitrary", "arbitrary")),
        name="rwkv_scan",
    )(w, a, b, k, v, r, s0, lnw, lnb, rk)


_NT = (((1,), (1,)), ((), ()))


def _build_queries(qn, qr, wuk_ref, scale, rows):
    lane = lax.broadcasted_iota(jnp.int32, (rows, 2 * LANES), 1)
    slot = _div_pow2(_mod_pow2(lane, LANES), HALF_ROPE)
    qrf = qr.astype(F32) * scale
    parts = []
    for h in range(B_HEADS):
        ql = jnp.dot(qn[:, h * B_NOPE_DIM:(h + 1) * B_NOPE_DIM], wuk_ref[h], preferred_element_type=F32) * scale
        parts.append(jnp.concatenate([ql, jnp.where(slot == h, qrf, 0.0)], axis=-1).astype(BF16))
    return parts


def _mla_prompt_kernel(qn_ref, qr_ref, kf_ref, wuk_ref, wuv_ref, o_ref, q_s, s_s, p_s, m_s, l_s, a_s, acc_s,
                       *, tq, scale, row_block):
    qi = pl.program_id(1)
    parts = _build_queries(qn_ref[...], qr_ref[...], wuk_ref, scale, tq)
    for h in range(B_HEADS):
        q_s[h * tq:(h + 1) * tq, :] = parts[h]
    rows = B_HEADS * tq
    m_s[...] = jnp.full((rows, LANES), -jnp.inf, F32)
    l_s[...] = jnp.zeros((rows, LANES), F32)
    acc_s[...] = jnp.zeros((rows, KV_RANK), F32)

    def keys(kc):
        return pl.ds(pl.multiple_of(kc * tq, tq), tq)

    def scores(kc, slot):
        s_s[slot] = lax.dot_general(q_s[...], kf_ref[keys(kc), :], _NT, preferred_element_type=F32)

    def lane_tiles(s):
        return [s[:, c * LANES:(c + 1) * LANES] for c in range(tq // LANES)]

    def softmax_pv(kc, slot, masked):
        blocks = [slice(rb * row_block, (rb + 1) * row_block) for rb in range(rows // row_block)]
        for rb, blk in enumerate(blocks):
            s = s_s[slot, blk, :]
            if masked:
                qt = _mod_pow2(rb * row_block + lax.broadcasted_iota(jnp.int32, (row_block, tq), 0), tq)
                kpos = lax.broadcasted_iota(jnp.int32, (row_block, tq), 1)
                s = jnp.where(kpos <= qt, s, -jnp.inf)
                s_s[slot, blk, :] = s
            mx = functools.reduce(jnp.maximum, lane_tiles(s))
            m_prev = m_s[blk, :]
            m_new = jnp.maximum(m_prev, jnp.max(mx, axis=-1, keepdims=True))
            a_s[blk, :] = jnp.exp2(m_prev - m_new)
            m_s[blk, :] = m_new
        for blk in blocks:
            m_new = m_s[blk, :]
            ps = [jnp.exp2(t - m_new) for t in lane_tiles(s_s[slot, blk, :])]
            l_s[blk, :] = a_s[blk, :] * l_s[blk, :] + functools.reduce(jnp.add, ps)
            p_s[blk, :] = jnp.concatenate(ps, axis=-1).astype(BF16)
        pv = jnp.dot(p_s[...], kf_ref[keys(kc), 0:KV_RANK], preferred_element_type=F32)
        alpha = a_s[...]
        acc_s[...] = acc_s[...] * jnp.concatenate([alpha] * (KV_RANK // LANES), axis=-1) + pv

    scores(0, 0)

    def pair(i, carry):
        scores(2 * i + 1, 1)
        softmax_pv(2 * i, 0, False)
        scores(2 * i + 2, 0)
        softmax_pv(2 * i + 1, 1, False)
        return carry

    lax.fori_loop(0, qi // 2, pair, 0)

    @pl.when(qi % 2 == 1)
    def _():
        scores(qi, 1)
        softmax_pv(qi - 1, 0, False)
        softmax_pv(qi, 1, True)

    @pl.when(qi % 2 == 0)
    def _():
        softmax_pv(qi, 0, True)

    o = (acc_s[...] * (1.0 / jnp.sum(l_s[...], axis=-1, keepdims=True))).astype(BF16)
    outs = [jnp.dot(o[h * tq:(h + 1) * tq, :], wuv_ref[h], preferred_element_type=F32) for h in range(B_HEADS)]
    o_ref[...] = jnp.concatenate(outs, axis=-1).astype(o_ref.dtype)


def _mla_prompt(qn, qr, kfull, wuk, wuv, tq):
    B, T, _ = qn.shape
    scale = (B_NOPE_DIM + B_ROPE_DIM) ** -0.5 * math.log2(math.e)
    rows = B_HEADS * tq
    tile = lambda c: pl.BlockSpec((None, tq, c), lambda b, i: (b, i, 0))
    stat = pltpu.VMEM((rows, LANES), F32)
    return pl.pallas_call(
        functools.partial(_mla_prompt_kernel, tq=tq, scale=scale, row_block=min(128, tq)),
        grid=(B, T // tq),
        in_specs=[tile(A_WIDTH), tile(2 * LANES), pl.BlockSpec((None, T, KFULL), lambda b, i: (b, 0, 0)),
                  _const_spec((B_HEADS, B_NOPE_DIM, KV_RANK)), _const_spec((B_HEADS, KV_RANK, B_V_DIM))],
        out_specs=tile(B_WIDTH),
        out_shape=jax.ShapeDtypeStruct((B, T, B_WIDTH), BF16),
        scratch_shapes=[pltpu.VMEM((rows, KFULL), BF16), pltpu.VMEM((2, rows, tq), F32),
                        pltpu.VMEM((rows, tq), BF16), stat, stat, stat, pltpu.VMEM((rows, KV_RANK), F32)],
        compiler_params=_cparams(("arbitrary", "arbitrary")),
        name="mla_prompt",
    )(qn, qr, kfull, wuk, wuv)


def _mla_decode_kernel(pt_ref, qn_ref, qr_ref, kfn_ref, wuk_ref, wuv_ref, ckv_hbm, krt_hbm,
                       o_ref, ckv_buf, krt_buf, kv16, sem, *, layer, n_pages, T, scale, n_chunks):
    i = pl.program_id(0)
    n = pl.num_programs(0)
    L = n_pages * PAGE_SIZE

    def page_copies(sl, p, page):
        rows = pl.ds(pl.multiple_of(p * PAGE_SIZE, PAGE_SIZE), PAGE_SIZE)
        return (pltpu.make_async_copy(ckv_hbm.at[layer, page], ckv_buf.at[sl, rows], sem.at[0, sl]),
                pltpu.make_async_copy(krt_hbm.at[layer, page], krt_buf.at[sl, :, rows], sem.at[1, sl]))

    def issue(bb, sl):
        def body(p, carry):
            for cp in page_copies(sl, p, pt_ref[bb, p]):
                cp.start()
            return carry
        lax.fori_loop(0, n_pages, body, 0, unroll=math.gcd(n_pages, 8))

    def wait(sl):
        for p in range(n_pages):
            for cp in page_copies(sl, p, 0):
                cp.wait()

    def softmax_parts(s, values):
        m = jnp.max(s, axis=-1, keepdims=True)
        p = jnp.exp(s - m)
        return m, jnp.sum(p, axis=-1, keepdims=True), jnp.dot(p.astype(BF16), values, preferred_element_type=F32)

    def attend(sl):
        qr = qr_ref[sl]
        kfn = kfn_ref[sl]
        parts = _build_queries(qn_ref[sl], qr, wuk_ref, scale, T)
        qfull = jnp.concatenate(parts, axis=0)
        qlat = qfull[:, 0:KV_RANK]
        qrs = (qr.astype(F32) * scale)
        qrope = jnp.concatenate(
            [jnp.concatenate([qrs[:, h * HALF_ROPE:(h + 1) * HALF_ROPE],
                              qrs[:, LANES + h * HALF_ROPE:LANES + (h + 1) * HALF_ROPE]], axis=-1)
             for h in range(B_HEADS)], axis=0).astype(BF16)
        lc = L // n_chunks
        stats = []
        for c in range(n_chunks):
            rows = slice(c * lc, (c + 1) * lc)
            kv16[rows, :] = ckv_buf[sl, rows, :].astype(BF16)
            s = (lax.dot_general(qlat, kv16[rows, :], _NT, preferred_element_type=F32)
                 + jnp.dot(qrope, krt_buf[sl, :, rows].astype(BF16), preferred_element_type=F32))
            stats.append(softmax_parts(s, kv16[rows, :]))
        s_new = lax.dot_general(qfull, kfn, _NT, preferred_element_type=F32)
        nrows = B_HEADS * T
        qt = _mod_pow2(lax.broadcasted_iota(jnp.int32, (nrows, T), 0), T)
        kt = lax.broadcasted_iota(jnp.int32, (nrows, T), 1)
        stats.append(softmax_parts(jnp.where(kt <= qt, s_new, -jnp.inf), kfn[:, 0:KV_RANK]))
        m = functools.reduce(jnp.maximum, [st[0] for st in stats])
        weights = [jnp.exp(st[0] - m) for st in stats]
        l = functools.reduce(jnp.add, [w * st[1] for w, st in zip(weights, stats)])
        o = functools.reduce(jnp.add, [w * st[2] for w, st in zip(weights, stats)])
        o = (o / l).astype(BF16)
        outs = [jnp.dot(o[h * T:(h + 1) * T, :], wuv_ref[h], preferred_element_type=F32) for h in range(B_HEADS)]
        o_ref[sl] = jnp.concatenate(outs, axis=-1).astype(o_ref.dtype)

    @pl.when(i == 0)
    def _():
        issue(0, 0)

    issue(2 * i + 1, 1)
    wait(0)
    attend(0)

    @pl.when(i + 1 < n)
    def _():
        issue(2 * i + 2, 0)

    wait(1)
    attend(1)


def _mla_decode(page_table, qn, qr, kfn, wuk, wuv, cache_kv, cache_krt, layer):
    B, T, _ = qn.shape
    assert B % 2 == 0
    n_pages = page_table.shape[1]
    L = n_pages * PAGE_SIZE
    scale = (B_NOPE_DIM + B_ROPE_DIM) ** -0.5
    tile = lambda c: pl.BlockSpec((2, T, c), lambda b, pt: (b, 0, 0))
    cst = lambda shape: pl.BlockSpec(shape, lambda b, pt: (0,) * len(shape))
    grid_spec = pltpu.PrefetchScalarGridSpec(
        num_scalar_prefetch=1,
        grid=(B // 2,),
        in_specs=[tile(A_WIDTH), tile(2 * LANES), tile(KFULL),
                  cst((B_HEADS, B_NOPE_DIM, KV_RANK)), cst((B_HEADS, KV_RANK, B_V_DIM)),
                  pl.BlockSpec(memory_space=pl.ANY), pl.BlockSpec(memory_space=pl.ANY)],
        out_specs=tile(B_WIDTH),
        scratch_shapes=[pltpu.VMEM((2, L, KV_RANK), F32), pltpu.VMEM((2, B_ROPE_DIM, L), F32),
                        pltpu.VMEM((L, KV_RANK), BF16), pltpu.SemaphoreType.DMA((2, 2))],
    )
    return pl.pallas_call(
        functools.partial(_mla_decode_kernel, layer=layer, n_pages=n_pages, T=T, scale=scale,
                          n_chunks=math.gcd(n_pages, 4)),
        grid_spec=grid_spec,
        out_shape=jax.ShapeDtypeStruct((B, T, B_WIDTH), BF16),
        compiler_params=_cparams(("arbitrary",)),
        name="mla_decode",
    )(page_table, qn, qr, kfn, wuk, wuv, cache_kv, cache_krt)


def _pool_kernel(pc_ref, st_ref, wbd_ref, scale_ref, y_ref, newst_ref, e_s, s2_s, s4_s, s8_s, *, ts, T, pos0):
    L = T * ts
    P = (POOL_PREFIX + 1) * ts
    pc = pc_ref[...].reshape(L, C_WIDTH)
    e_s[P - POOL_PREFIX * ts:P, :] = st_ref[...].reshape(POOL_PREFIX * ts, C_WIDTH)
    e_s[P:P + L, :] = pc
    newst_ref[...] = e_s[P + L - POOL_PREFIX * ts:P + L, :].reshape(newst_ref.shape)
    s2_s[P - 14 * ts:P + L, :] = e_s[P - 14 * ts:P + L, :] + e_s[P - 15 * ts:P + L - ts, :]
    s4_s[P - 12 * ts:P + L, :] = s2_s[P - 12 * ts:P + L, :] + s2_s[P - 14 * ts:P + L - 2 * ts, :]
    s8_s[P - 8 * ts:P + L, :] = s4_s[P - 8 * ts:P + L, :] + s4_s[P - 12 * ts:P + L - 4 * ts, :]
    s16 = s8_s[P:P + L, :] + s8_s[P - 8 * ts:P + L - 8 * ts, :]
    lane = lax.broadcasted_iota(jnp.int32, (L, C_WIDTH), 1)
    grp = _div_pow2(lane, C_GROUP_DIM)
    win = jnp.where(grp == 0, s2_s[P:P + L, :],
                    jnp.where(grp == 1, s4_s[P:P + L, :], jnp.where(grp == 2, s8_s[P:P + L, :], s16)))
    wsize = jnp.where(grp == 0, 2, jnp.where(grp == 1, 4, jnp.where(grp == 2, 8, 16)))
    pos = pos0 + _div_pow2(lax.broadcasted_iota(jnp.int32, (L, C_WIDTH), 0), ts)
    cnt = jnp.minimum(pos + 1, wsize).astype(F32)
    d = win / cnt - pc
    y = jnp.dot(d.astype(BF16), wbd_ref[...], preferred_element_type=F32) * scale_ref[...]
    y_ref[...] = y.astype(y_ref.dtype).reshape(y_ref.shape)


def _pool(tok, pc, pool_prev, wbd, pscale, pos0):
    ts, T = tok.ts, tok.T
    rows = (POOL_PREFIX + 1 + T) * ts
    if tok.time_major:
        grid, blk = (tok.B // tok.bb, 1), tok.spec(C_WIDTH)
    else:
        grid, blk = (tok.B, 1), pl.BlockSpec((None, T, C_WIDTH), lambda i, j: (i, 0, 0))
    return pl.pallas_call(
        functools.partial(_pool_kernel, ts=ts, T=T, pos0=pos0),
        grid=grid,
        in_specs=[blk, tok.state_spec(POOL_PREFIX, C_WIDTH), _const_spec((C_WIDTH, C_WIDTH)),
                  _const_spec((1, C_WIDTH))],
        out_specs=[blk, tok.state_spec(POOL_PREFIX, C_WIDTH)],
        out_shape=[jax.ShapeDtypeStruct(tok.shape(C_WIDTH), BF16),
                   jax.ShapeDtypeStruct(tok.state_shape(POOL_PREFIX, C_WIDTH), F32)],
        scratch_shapes=[pltpu.VMEM((rows, C_WIDTH), F32)] * 4,
        compiler_params=_cparams(("arbitrary", "arbitrary")),
        name="pool",
    )(pc, pool_prev, wbd, pscale)


def _ffn_kernel(x_ref, ya_ref, g_ref, yb_ref, yc_ref, cst_ref, wout_ref, gffn_ref, wup_ref, cw_ref, cb_ref,
                wdn_ref, gfin_ref, xo_ref, cnew_ref, gbuf_s, *, tm, ts, final):
    taps = CONV_WIDTH - 1
    pad = max(SUBLANES, taps * ts)
    it = pl.program_id(1)

    @pl.when(it == 0)
    def _():
        gbuf_s[pad - taps * ts:pad, :] = cst_ref[...].reshape(taps * ts, D_FF)

    x = x_ref[...].reshape(tm, D_MODEL)
    ya = (ya_ref[...].reshape(tm, A_WIDTH) * g_ref[...].reshape(tm, A_WIDTH)).astype(BF16)
    ycat = jnp.concatenate([ya, yb_ref[...].reshape(tm, B_WIDTH), yc_ref[...].reshape(tm, C_WIDTH)], axis=-1)
    x1 = x + jnp.dot(ycat, wout_ref[...], preferred_element_type=F32)
    xn = _rms(x1, gffn_ref[...]).astype(BF16)
    acc = jnp.zeros((tm, D_MODEL), F32)
    for c in range(D_FF // FF_CHUNK):
        lo = c * FF_CHUNK
        up = jnp.dot(xn, wup_ref[:, lo:lo + FF_CHUNK], preferred_element_type=F32)
        gate = jnp.dot(xn, wup_ref[:, D_FF + lo:D_FF + lo + FF_CHUNK], preferred_element_type=F32)
        gbuf_s[pad:pad + tm, lo:lo + FF_CHUNK] = gate
        gc = cb_ref[:, lo:lo + FF_CHUNK] + cw_ref[2:3, lo:lo + FF_CHUNK] * gate
        for j in range(taps):
            off = pad - (taps - j) * ts
            gc = gc + cw_ref[j:j + 1, lo:lo + FF_CHUNK] * gbuf_s[off:off + tm, lo:lo + FF_CHUNK]
        hh = (gc * _sigmoid(gc) * up).astype(BF16)
        acc = acc + jnp.dot(hh, wdn_ref[lo:lo + FF_CHUNK, :], preferred_element_type=F32)
    last = gbuf_s[pad + tm - taps * ts:pad + tm, :]
    cnew_ref[...] = last.reshape(cnew_ref.shape)
    gbuf_s[pad - taps * ts:pad, :] = last
    x2 = x1 + acc
    if final:
        x2 = _rms(x2, gfin_ref[...])
    xo_ref[...] = x2.reshape(xo_ref.shape)


def _ffn(tok, x, ya, g, yb, yc, conv_prev, p, gfin, final):
    tm, ts = tok.tm, tok.ts
    taps = CONV_WIDTH - 1
    pad = max(SUBLANES, taps * ts)
    return pl.pallas_call(
        functools.partial(_ffn_kernel, tm=tm, ts=ts, final=final),
        grid=tok.grid,
        in_specs=[tok.spec(D_MODEL), tok.spec(A_WIDTH), tok.spec(A_WIDTH), tok.spec(B_WIDTH), tok.spec(C_WIDTH),
                  tok.state_spec(taps, D_FF), _const_spec((D_MODEL, D_MODEL)), _const_spec((1, D_MODEL)),
                  _const_spec((D_MODEL, 2 * D_FF)), _const_spec((CONV_WIDTH, D_FF)), _const_spec((1, D_FF)),
                  _const_spec((D_FF, D_MODEL)), _const_spec((1, D_MODEL))],
        out_specs=[tok.spec(D_MODEL), tok.state_spec(taps, D_FF)],
        out_shape=[jax.ShapeDtypeStruct(tok.shape(D_MODEL), F32),
                   jax.ShapeDtypeStruct(tok.state_shape(taps, D_FF), F32)],
        scratch_shapes=[pltpu.VMEM((pad + tm, D_FF), F32)],
        compiler_params=_cparams(("arbitrary", "arbitrary")),
        name="outproj_ffn",
    )(x, ya, g, yb, yc, conv_prev, p["w_out"], p["g_ffn"], p["w_up"], p["conv_w"], p["conv_b"], p["w_down"], gfin)


def _ext_columns():
    qn0 = A_COLS
    qr0 = qn0 + B_HEADS * B_NOPE_DIM
    ckv0 = qr0 + B_HEADS * B_ROPE_DIM
    kr0 = ckv0 + KV_RANK
    pc0 = A_COLS + B_COLS
    cols = list(range(A_COLS)) + list(range(qn0, qr0)) + list(range(ckv0, kr0)) + list(range(pc0, pc0 + C_WIDTH))
    for half in range(2):
        blk = []
        for h in range(B_HEADS):
            blk += [qr0 + h * B_ROPE_DIM + half * HALF_ROPE + i for i in range(HALF_ROPE)]
        blk += [kr0 + half * HALF_ROPE + i for i in range(HALF_ROPE)]
        cols += blk + [-1] * (LANES - len(blk))
    for half in range(2):
        blk = [kr0 + half * HALF_ROPE + i for _ in range(B_HEADS) for i in range(HALF_ROPE)]
        cols += blk + [-1] * (LANES - len(blk))
    assert len(cols) == IN_EXT
    return np.asarray(cols, np.int32)


def _layer_params(l, norm_mix_g, w_in, mu_shift, decay_w0, decay_w2, iclr_a0, iclr_a2, gate_g2, k_k, k_a, r_k,
                  lnx_w, lnx_b, kv_norm_g, w_uk, w_uv, pool_w, pool_scale, w_out, norm_ffn_g, w_up, conv_w,
                  conv_b, w_down):
    cols = _ext_columns()
    w_ext = jnp.where(cols[None, :] >= 0, w_in[l][:, np.maximum(cols, 0)], 0.0).astype(BF16)
    lora_cat = jnp.zeros((LANES, 2 * A_WIDTH), F32)
    lora_cat = lora_cat.at[0:DECAY_LORA, 0:A_WIDTH].set(decay_w2[l])
    lora_cat = lora_cat.at[DECAY_LORA:DECAY_LORA + ICLR_LORA, A_WIDTH:].set(iclr_a2[l])
    head = np.arange(A_WIDTH) // A_HEAD_DIM
    head_ones = jnp.asarray((head[:, None] == head[None, :]).astype(np.float32), BF16)
    wbd = jnp.zeros((C_WIDTH, C_WIDTH), F32)
    for gi in range(C_GROUPS):
        sl = slice(gi * C_GROUP_DIM, (gi + 1) * C_GROUP_DIM)
        wbd = wbd.at[sl, sl].set(pool_w[l, gi])
    row = lambda v: v.reshape(1, -1)
    return dict(
        g_mix=row(norm_mix_g[l]), w_ext=w_ext, kv_g=row(kv_norm_g[l]),
        mu=row(mu_shift[l]), lora_cat=lora_cat.astype(BF16), w0=row(decay_w0[l]), a0=row(iclr_a0[l]),
        g2=gate_g2[l].astype(BF16), k_k=row(k_k[l]), k_a=row(k_a[l]), head_ones=head_ones,
        r_k=r_k[l], lnx_w=lnx_w[l].reshape(A_HEADS, A_HEAD_DIM), lnx_b=lnx_b[l].reshape(A_HEADS, A_HEAD_DIM),
        wuk=jnp.transpose(w_uk[l], (1, 2, 0)).astype(BF16), wuv=jnp.transpose(w_uv[l], (1, 0, 2)).astype(BF16),
        pool_wbd=wbd.astype(BF16), pool_scale=row(pool_scale[l]),
        w_out=w_out[l].astype(BF16), g_ffn=row(norm_ffn_g[l]), w_up=w_up[l].astype(BF16),
        conv_w=conv_w[l], conv_b=row(conv_b[l]), w_down=w_down[l].astype(BF16),
    )


def _rope_tables(pos):
    inv = ROPE_BASE ** (-jnp.arange(0, B_ROPE_DIM, 2, dtype=F32) / B_ROPE_DIM)
    ang = pos.astype(F32)[:, None] * inv[None, :]
    reps = LANES // HALF_ROPE
    return jnp.tile(jnp.cos(ang), (1, reps)), jnp.tile(jnp.sin(ang), (1, reps))


def _to_scan(tok, x):
    assert tok.time_major
    x = x.reshape(tok.T, tok.B, A_HEADS, A_HEAD_DIM)
    return jnp.transpose(x, (2, 0, 3, 1))


def _lane_maps(nb):
    lane_of = np.asarray([[_prompt_lane(b, h, nb) for h in range(A_HEADS)] for b in range(nb)], np.int32)
    lanes_b = np.zeros(nb * A_HEADS, np.int32)
    lanes_h = np.zeros(nb * A_HEADS, np.int32)
    for b in range(nb):
        for h in range(A_HEADS):
            lanes_b[lane_of[b, h]], lanes_h[lane_of[b, h]] = b, h
    return lane_of, lanes_b, lanes_h


def _from_scan(tok, y):
    H, N = A_HEADS, A_HEAD_DIM
    if tok.time_major:
        return jnp.transpose(y, (1, 3, 0, 2)).reshape(tok.T, tok.B, H * N)
    lane_of, _, _ = _lane_maps(tok.B)
    y = y[0][:, :, lane_of]
    return jnp.transpose(y, (2, 0, 3, 1)).reshape(tok.B, tok.T, H * N)


def _pad_lanes(x):
    return jnp.pad(x, [(0, 0)] * (x.ndim - 1) + [(0, LANES - x.shape[-1])])


def _state_to_scan(tok, s):
    if tok.time_major:
        return jnp.transpose(s, (1, 3, 2, 0))
    _, lanes_b, lanes_h = _lane_maps(tok.B)
    return _pad_lanes(jnp.transpose(s, (3, 2, 0, 1))[:, :, lanes_b, lanes_h])[None]


def _state_from_scan(tok, s):
    if tok.time_major:
        return jnp.transpose(s, (3, 0, 2, 1))
    lane_of, _, _ = _lane_maps(tok.B)
    return jnp.transpose(s[0][:, :, lane_of], (2, 3, 1, 0))


def _head_consts(tok, p):
    def lay(v):
        if tok.time_major:
            return jnp.broadcast_to(v[:, :, None], (A_HEADS, A_HEAD_DIM, LANES))
        _, _, lanes_h = _lane_maps(tok.B)
        return _pad_lanes(v.T[:, lanes_h])[None]
    return lay(p["lnx_w"]), lay(p["lnx_b"]), lay(p["r_k"])


def _layer(tok, x, cos, sin, pos0, shift_prev, wkv_prev, pool_prev, conv_prev, p, gfin, final, scan_tc, attend):
    pa, qn, kvlat, pc, qr, kfull, krope = _inproj(tok, x, p["g_mix"], p["w_ext"], p["kv_g"], cos, sin)
    if tok.time_major:
        r, w, k, v, a, b, g, shift_new = _rwkv_prep(tok, pa, shift_prev, p)
        r, w, k, v, a, b = (_to_scan(tok, t) for t in (r, w, k, v, a, b))
    else:
        r, w, k, v, a, b, g, shift_new = _rwkv_prep_scan(pa, shift_prev, p, scan_tc)
    lnw, lnb, rk = _head_consts(tok, p)
    y, s_new = _rwkv_scan(w, a, b, k, v, r, _state_to_scan(tok, wkv_prev), lnw, lnb, rk, scan_tc)
    ya = _from_scan(tok, y)
    yb = attend(qn, qr, kfull)
    yc, pool_new = _pool(tok, pc, pool_prev, p["pool_wbd"], p["pool_scale"], pos0)
    x, conv_new = _ffn(tok, x, ya, g, yb, yc, conv_prev, p, gfin, final)
    return x, (kvlat, krope, _state_from_scan(tok, s_new), shift_new, pool_new, conv_new)


def kernel(x_prompt, x_sample, cache_kv_latent, cache_k_rope, page_table, state_wkv, state_shift, state_pool,
           state_conv, norm_mix_g, w_in, mu_shift, decay_w0, decay_w2, iclr_a0, iclr_a2, gate_g2, k_k, k_a, r_k,
           lnx_w, lnx_b, kv_norm_g, w_uk, w_uv, pool_w, pool_scale, w_out, norm_ffn_g, w_up, conv_w, conv_b,
           w_down, norm_final_g):
    Bp, Tp, _ = x_prompt.shape
    Bs, Ts, _ = x_sample.shape
    past_len = page_table.shape[1] * PAGE_SIZE
    tokp = _Tok(False, Bp, Tp, min(512, Tp))
    toks = _Tok(True, Bs, Ts, min(32, Bs))
    cos_p, sin_p = _rope_tables(jnp.arange(Tp, dtype=jnp.int32))
    pos_s = past_len + jnp.repeat(jnp.arange(Ts, dtype=jnp.int32), toks.bb)
    cos_s, sin_s = _rope_tables(pos_s)
    gfin = norm_final_g.reshape(1, -1)
    tm_first = lambda t: jnp.swapaxes(t, 0, 1)
    cache_krt = jnp.swapaxes(cache_k_rope, 2, 3)

    xp = x_prompt
    xs = tm_first(x_sample)
    zeros = lambda *shape: jnp.zeros(shape, F32)
    outs_p, outs_s = [], []
    for l in range(DEPTH):
        p = _layer_params(l, norm_mix_g, w_in, mu_shift, decay_w0, decay_w2, iclr_a0, iclr_a2, gate_g2, k_k, k_a,
                          r_k, lnx_w, lnx_b, kv_norm_g, w_uk, w_uv, pool_w, pool_scale, w_out, norm_ffn_g, w_up,
                          conv_w, conv_b, w_down)
        final = l == DEPTH - 1

        attend_p = lambda qn, qr, kfull: _mla_prompt(qn, qr, kfull, p["wuk"], p["wuv"], min(256, Tp))
        xp, st_p = _layer(tokp, xp, cos_p, sin_p, 0, zeros(Bp, A_COLS),
                          zeros(Bp, A_HEADS, A_HEAD_DIM, A_HEAD_DIM), zeros(Bp, POOL_PREFIX, C_WIDTH),
                          zeros(Bp, CONV_WIDTH - 1, D_FF), p, gfin, final, min(32, Tp), attend_p)

        def attend_s(qn, qr, kfull):
            yb = _mla_decode(page_table, tm_first(qn), tm_first(qr), tm_first(kfull), p["wuk"], p["wuv"],
                             cache_kv_latent, cache_krt, l)
            return tm_first(yb)
        xs, st_s = _layer(toks, xs, cos_s, sin_s, past_len, state_shift[l][None], state_wkv[l],
                          tm_first(state_pool[l]), tm_first(state_conv[l]), p, gfin, final, Ts, attend_s)
        outs_p.append(st_p)
        outs_s.append(st_s)

    stack = lambda outs, i, f: jnp.stack([f(o[i]) for o in outs], axis=0)
    ident = lambda t: t
    return (xp, tm_first(xs),
            stack(outs_p, 0, ident), stack(outs_p, 1, ident), stack(outs_p, 2, ident),
            stack(outs_p, 3, ident), stack(outs_p, 4, ident), stack(outs_p, 5, ident),
            stack(outs_s, 0, tm_first), stack(outs_s, 1, tm_first), stack(outs_s, 2, ident),
            stack(outs_s, 3, lambda t: t[0]), stack(outs_s, 4, tm_first), stack(outs_s, 5, tm_first))
```

```python
import functools
import math

import jax
import jax.numpy as jnp
import numpy as np
from jax import lax
from jax.experimental import pallas as pl
from jax.experimental.pallas import tpu as pltpu

F32 = jnp.float32
BF16 = jnp.bfloat16

D_MODEL = 1024
DEPTH = 2
PAGE_SIZE = 128
A_HEADS = 6
A_HEAD_DIM = 64
A_WIDTH = A_HEADS * A_HEAD_DIM
DECAY_LORA = 64
ICLR_LORA = 64
GATE_LORA = 128
A_COLS = 3 * A_WIDTH + DECAY_LORA + ICLR_LORA + GATE_LORA
B_HEADS = 6
B_NOPE_DIM = 64
B_ROPE_DIM = 32
B_V_DIM = 64
B_WIDTH = B_HEADS * B_V_DIM
KV_RANK = 256
ROPE_BASE = 10000.0
B_COLS = B_HEADS * (B_NOPE_DIM + B_ROPE_DIM) + KV_RANK + B_ROPE_DIM
C_GROUPS = 4
C_GROUP_DIM = 64
C_WIDTH = C_GROUPS * C_GROUP_DIM
POOL_WINDOWS = (2, 4, 8, 16)
POOL_PREFIX = 15
D_FF = 2816
CONV_WIDTH = 3
NORM_EPS = 1e-6
GN_EPS = A_HEAD_DIM * 1e-5

LANES = 128
SUBLANES = 8
HALF_ROPE = B_ROPE_DIM // 2
IN_EXT = A_COLS + B_HEADS * B_NOPE_DIM + KV_RANK + C_WIDTH + 4 * LANES
OFF_QN = A_COLS
OFF_CKV = OFF_QN + B_HEADS * B_NOPE_DIM
OFF_PC = OFF_CKV + KV_RANK
OFF_ROPE = OFF_PC + C_WIDTH
KFULL = KV_RANK + 2 * LANES
FF_CHUNK = D_FF // 2
VMEM_LIMIT = 56 * 1024 * 1024


def _cparams(sem):
    return pltpu.CompilerParams(dimension_semantics=sem, vmem_limit_bytes=VMEM_LIMIT)


def _const_spec(shape):
    nd = len(shape)
    return pl.BlockSpec(shape, lambda *_: (0,) * nd, pipeline_mode=pl.Buffered(1))


class _Tok:
    def __init__(self, time_major, B, T, tile):
        self.time_major, self.B, self.T = time_major, B, T
        if time_major:
            self.bb = tile
            self.tm, self.ts = T * tile, tile
            self.grid = (B // tile, 1)
        else:
            self.tm, self.ts = tile, 1
            self.grid = (B, T // tile)
        self.n = B * T

    def shape(self, C):
        return (self.T, self.B, C) if self.time_major else (self.B, self.T, C)

    def spec(self, C):
        if self.time_major:
            return pl.BlockSpec((self.T, self.bb, C), lambda i, j: (0, i, 0))
        return pl.BlockSpec((None, self.tm, C), lambda i, j: (i, j, 0))

    def state_shape(self, steps, C):
        return (steps, self.B, C) if self.time_major else (self.B, steps, C)

    def state_spec(self, steps, C):
        if self.time_major:
            return pl.BlockSpec((steps, self.bb, C), lambda i, j: (0, i, 0))
        return pl.BlockSpec((None, steps, C), lambda i, j: (i, 0, 0))


def _rms(x, g):
    return x * lax.rsqrt(jnp.mean(x * x, axis=-1, keepdims=True) + NORM_EPS) * g


def _sigmoid(x):
    return 1.0 / (1.0 + jnp.exp(-x))


def _div_pow2(x, d):
    assert d & (d - 1) == 0
    return lax.shift_right_logical(x, int(math.log2(d)))


def _mod_pow2(x, d):
    assert d & (d - 1) == 0
    return lax.bitwise_and(x, d - 1)


def _inproj_kernel(x_ref, g_ref, w_ref, kvg_ref, cos_ref, sin_ref,
                   pa_ref, qn_ref, kvlat_ref, pc_ref, qr_ref, kfull_ref, krope_ref, *, tm):
    x = x_ref[...].reshape(tm, D_MODEL)
    xb = _rms(x, g_ref[...]).astype(BF16)

    def proj(lo, hi):
        return jnp.dot(xb, w_ref[:, lo:hi], preferred_element_type=F32)

    pa_ref[...] = proj(0, A_COLS).reshape(pa_ref.shape)
    qn_ref[...] = proj(OFF_QN, OFF_CKV).astype(BF16).reshape(qn_ref.shape)
    ckv = _rms(proj(OFF_CKV, OFF_PC), kvg_ref[...])
    kvlat_ref[...] = ckv.reshape(kvlat_ref.shape)
    pc_ref[...] = proj(OFF_PC, OFF_ROPE).reshape(pc_ref.shape)
    rr = proj(OFF_ROPE, IN_EXT)
    cos = cos_ref[...]
    sin = sin_ref[...]
    q1, q2 = rr[:, 0:LANES], rr[:, LANES:2 * LANES]
    k1, k2 = rr[:, 2 * LANES:3 * LANES], rr[:, 3 * LANES:4 * LANES]
    o1 = q1 * cos - q2 * sin
    o2 = q1 * sin + q2 * cos
    qr = jnp.concatenate([o1, o2], axis=-1).astype(BF16)
    qr_ref[...] = qr.reshape(qr_ref.shape)
    kf = jnp.concatenate([ckv, k1 * cos - k2 * sin, k1 * sin + k2 * cos], axis=-1).astype(BF16)
    kfull_ref[...] = kf.reshape(kfull_ref.shape)
    kr_lo = B_HEADS * HALF_ROPE
    krope = jnp.concatenate([o1[:, kr_lo:kr_lo + HALF_ROPE], o2[:, kr_lo:kr_lo + HALF_ROPE]], axis=-1)
    krope_ref[...] = krope.reshape(krope_ref.shape)


def _inproj(tok, x, g, w_ext, kvg, cos, sin):
    tm = tok.tm
    n_tab = cos.shape[0] // tm
    tab_spec = pl.BlockSpec((tm, LANES), lambda i, j: (j % n_tab, 0))
    widths = (A_COLS, A_WIDTH, KV_RANK, C_WIDTH, 2 * LANES, KFULL, B_ROPE_DIM)
    dtypes = (F32, BF16, F32, F32, BF16, BF16, F32)
    return pl.pallas_call(
        functools.partial(_inproj_kernel, tm=tm),
        grid=tok.grid,
        in_specs=[tok.spec(D_MODEL), _const_spec((1, D_MODEL)), _const_spec((D_MODEL, IN_EXT)),
                  _const_spec((1, KV_RANK)), tab_spec, tab_spec],
        out_specs=[tok.spec(c) for c in widths],
        out_shape=[jax.ShapeDtypeStruct(tok.shape(c), d) for c, d in zip(widths, dtypes)],
        compiler_params=_cparams(("arbitrary", "arbitrary")),
        name="inproj",
    )(x, g, w_ext, kvg, cos, sin)


def _rwkv_token_math(pa, prev, mu_ref, wcat_ref, w0_ref, a0_ref, g2_ref, kk_ref, ka_ref, ones_ref):
    s = pa + (prev - pa) * mu_ref[...]
    W = A_WIDTH
    r, k, v = s[:, 0:W], s[:, W:2 * W], s[:, 2 * W:3 * W]
    lo_in = s[:, 3 * W:3 * W + LANES]
    gd = s[:, 3 * W + LANES:3 * W + 2 * LANES]
    lane = lax.broadcasted_iota(jnp.int32, lo_in.shape, 1)
    lo_in = jnp.where(lane < DECAY_LORA, jnp.tanh(lo_in), lo_in)
    lo = jnp.dot(lo_in.astype(BF16), wcat_ref[...], preferred_element_type=F32)
    z = -(w0_ref[...] + lo[:, 0:W])
    softplus = jnp.maximum(z, 0.0) + jnp.log(1.0 + jnp.exp(-jnp.abs(z)))
    decay = jnp.exp(-jnp.exp(-softplus - 0.5))
    a = _sigmoid(a0_ref[...] + lo[:, W:2 * W])
    g = jnp.dot(_sigmoid(gd).astype(BF16), g2_ref[...], preferred_element_type=F32)
    kk = k * kk_ref[...]
    kk2 = kk * kk
    hi = kk2.astype(BF16)
    lo2 = (kk2 - hi.astype(F32)).astype(BF16)
    ones = ones_ref[...]
    ss = (jnp.dot(hi, ones, preferred_element_type=F32) + jnp.dot(lo2, ones, preferred_element_type=F32))
    kk = kk * lax.rsqrt(jnp.maximum(ss, 1e-24))
    kmod = k * (1.0 + (a - 1.0) * ka_ref[...])
    return r, decay, kmod, v, -kk, kk * a, g


def _rwkv_prep_kernel(pa_ref, st_ref, mu_ref, wcat_ref, w0_ref, a0_ref, g2_ref, kk_ref, ka_ref, ones_ref,
                      r_ref, w_ref, k_ref, v_ref, a_ref, b_ref, g_ref, newst_ref, buf_ref, *, tm, ts):
    pad = max(SUBLANES, ts)
    it = pl.program_id(1)

    @pl.when(it == 0)
    def _():
        buf_ref[pad - ts:pad, :] = st_ref[...].reshape(ts, A_COLS)

    pa = pa_ref[...].reshape(tm, A_COLS)
    buf_ref[pad:pad + tm, :] = pa
    prev = buf_ref[pad - ts:pad - ts + tm, :]
    last = pa[tm - ts:tm, :]
    buf_ref[pad - ts:pad, :] = last
    newst_ref[...] = last.reshape(newst_ref.shape)
    vals = _rwkv_token_math(pa, prev, mu_ref, wcat_ref, w0_ref, a0_ref, g2_ref, kk_ref, ka_ref, ones_ref)
    for ref, val in zip((r_ref, w_ref, k_ref, v_ref, a_ref, b_ref, g_ref), vals):
        ref[...] = val.reshape(ref.shape)


def _prompt_lane(b, h, nb):
    return (h % 2) * (A_HEADS // 2) * nb + (h // 2) * nb + b


def _rwkv_prep_scan_kernel(pa_ref, st_ref, mu_ref, wcat_ref, w0_ref, a0_ref, g2_ref, kk_ref, ka_ref, ones_ref,
                           r_ref, w_ref, k_ref, v_ref, a_ref, b_ref, g_ref, newst_ref, buf_ref, q_ref, *, nb, tt):
    it = pl.program_id(0)

    @pl.when(it == 0)
    def _():
        buf_ref[0:nb, :] = st_ref[...]

    for t in range(tt):
        buf_ref[(t + 1) * nb:(t + 2) * nb, :] = pa_ref[:, t, :]
    pa = buf_ref[nb:(tt + 1) * nb, :]
    prev = buf_ref[0:tt * nb, :]
    vals = _rwkv_token_math(pa, prev, mu_ref, wcat_ref, w0_ref, a0_ref, g2_ref, kk_ref, ka_ref, ones_ref)
    last = buf_ref[tt * nb:(tt + 1) * nb, :]
    buf_ref[0:nb, :] = last
    newst_ref[...] = last
    g = vals[6]
    for t in range(tt):
        g_ref[:, t, :] = g[t * nb:(t + 1) * nb, :]
    for qi in range(6):
        q_ref[qi] = vals[qi]
    outs = (r_ref, w_ref, k_ref, v_ref, a_ref, b_ref)
    half = A_HEADS // 2
    win = half * nb
    N = A_HEAD_DIM

    def pair(pi, carry):
        for qi, o_ref in enumerate(outs):
            rows = q_ref[qi, pl.ds(pl.multiple_of(pi * 2 * nb, 2 * nb), 2 * nb), :]
            pieces = [rows[s * nb:(s + 1) * nb, hp * LANES:(hp + 1) * LANES] for s in range(2) for hp in range(half)]
            m = jnp.concatenate(pieces + [jnp.zeros((LANES - 2 * win, LANES), F32)], axis=0)
            mt = m.T
            unused = jnp.full((N, LANES - 2 * win), 1.0 if o_ref is w_ref else 0.0, F32)
            for s in range(2):
                tile = jnp.concatenate([mt[0:N, s * win:(s + 1) * win], mt[N:2 * N, s * win:(s + 1) * win], unused],
                                       axis=1)
                o_ref[0, pi * 2 + s] = tile
        return carry

    lax.fori_loop(0, tt // 2, pair, 0)


def _rwkv_prep_scan(pa, shift_prev, p, tt):
    nb, T, _ = pa.shape
    assert 2 * (A_HEADS // 2) * nb <= LANES and tt % 2 == 0
    vec = _const_spec((1, A_WIDTH))
    seq = pl.BlockSpec((1, tt, A_HEAD_DIM, LANES), lambda i: (0, i, 0, 0))
    st = pl.BlockSpec((nb, A_COLS), lambda i: (0, 0))
    return pl.pallas_call(
        functools.partial(_rwkv_prep_scan_kernel, nb=nb, tt=tt),
        grid=(T // tt,),
        in_specs=[pl.BlockSpec((nb, tt, A_COLS), lambda i: (0, i, 0)), st, _const_spec((1, A_COLS)),
                  _const_spec((LANES, 2 * A_WIDTH)), vec, vec, _const_spec((GATE_LORA, A_WIDTH)), vec, vec,
                  _const_spec((A_WIDTH, A_WIDTH))],
        out_specs=[seq] * 6 + [pl.BlockSpec((nb, tt, A_WIDTH), lambda i: (0, i, 0)), st],
        out_shape=[jax.ShapeDtypeStruct((1, T, A_HEAD_DIM, LANES), F32)] * 6
        + [jax.ShapeDtypeStruct((nb, T, A_WIDTH), F32), jax.ShapeDtypeStruct((nb, A_COLS), F32)],
        scratch_shapes=[pltpu.VMEM(((tt + 1) * nb, A_COLS), F32), pltpu.VMEM((6, tt * nb, A_WIDTH), F32)],
        compiler_params=_cparams(("arbitrary",)),
        name="rwkv_prep_scan",
    )(pa, shift_prev, p["mu"], p["lora_cat"], p["w0"], p["a0"], p["g2"], p["k_k"], p["k_a"], p["head_ones"])


def _rwkv_prep(tok, pa, shift_prev, p):
    tm, ts = tok.tm, tok.ts
    pad = max(SUBLANES, ts)
    vec = _const_spec((1, A_WIDTH))
    return pl.pallas_call(
        functools.partial(_rwkv_prep_kernel, tm=tm, ts=ts),
        grid=tok.grid,
        in_specs=[tok.spec(A_COLS), tok.state_spec(1, A_COLS), _const_spec((1, A_COLS)),
                  _const_spec((LANES, 2 * A_WIDTH)), vec, vec, _const_spec((GATE_LORA, A_WIDTH)), vec, vec,
                  _const_spec((A_WIDTH, A_WIDTH))],
        out_specs=[tok.spec(A_WIDTH)] * 7 + [tok.state_spec(1, A_COLS)],
        out_shape=[jax.ShapeDtypeStruct(tok.shape(A_WIDTH), F32)] * 7
        + [jax.ShapeDtypeStruct(tok.state_shape(1, A_COLS), F32)],
        scratch_shapes=[pltpu.VMEM((pad + tm, A_COLS), F32)],
        compiler_params=_cparams(("arbitrary", "arbitrary")),
        name="rwkv_prep",
    )(pa, shift_prev, p["mu"], p["lora_cat"], p["w0"], p["a0"], p["g2"], p["k_k"], p["k_a"], p["head_ones"])


def _rwkv_scan_kernel(w_ref, a_ref, b_ref, k_ref, v_ref, r_ref, s0_ref, lnw_ref, lnb_ref, rk_ref,
                      y_ref, sT_ref, S_ref, op_ref, *, tc):
    N = A_HEAD_DIM
    it = pl.program_id(1)

    @pl.when(it == 0)
    def _():
        S_ref[...] = s0_ref[0]

    def step(t, p_prev):
        w = w_ref[0, t]
        r = r_ref[0, t]
        k = k_ref[0, t]
        v = v_ref[0, t]
        p_new = p_prev * w
        inv = 1.0 / p_new
        rt = r * p_new
        bt = b_ref[0, t] * inv
        kt = k * inv
        op_ref[0] = a_ref[0, t] * p_prev
        op_ref[1] = rt
        op_ref[2] = bt
        op_ref[3] = kt
        sa = [jnp.zeros((N, LANES), F32) for _ in range(2)]
        yp = [jnp.zeros((N, LANES), F32) for _ in range(2)]
        for j in range(N):
            Zj = S_ref[j]
            sa[j % 2] = sa[j % 2] + Zj * op_ref[0, pl.ds(j, 1), :]
            yp[j % 2] = yp[j % 2] + Zj * op_ref[1, pl.ds(j, 1), :]
        sa = sa[0] + sa[1]
        yp = yp[0] + yp[1]
        for j in range(N):
            S_ref[j] = S_ref[j] + sa * op_ref[2, pl.ds(j, 1), :] + v * op_ref[3, pl.ds(j, 1), :]
        br = jnp.sum(bt * rt, axis=0, keepdims=True)
        kr = jnp.sum(kt * rt, axis=0, keepdims=True)
        y = yp + sa * br + v * kr
        mu = jnp.mean(y, axis=0, keepdims=True)
        d = y - mu
        var = jnp.mean(d * d, axis=0, keepdims=True)
        yn = d * lax.rsqrt(var + GN_EPS) * lnw_ref[0] + lnb_ref[0]
        bonus = jnp.sum(r * k * rk_ref[0], axis=0, keepdims=True)
        y_ref[0, t] = yn + bonus * v
        return p_new

    op_ref[0] = lax.fori_loop(0, tc, step, jnp.ones((N, LANES), F32))
    for j in range(N):
        S_ref[j] = S_ref[j] * op_ref[0, pl.ds(j, 1), :]

    @pl.when(it == pl.num_programs(1) - 1)
    def _():
        sT_ref[0] = S_ref[...]


def _rwkv_scan(w, a, b, k, v, r, s0, lnw, lnb, rk, tc):
    G, T, N, _ = w.shape
    seq = pl.BlockSpec((1, tc, N, LANES), lambda g, i: (g, i, 0, 0))
    st = pl.BlockSpec((1, N, N, LANES), lambda g, i: (g, 0, 0, 0))
    cst = pl.BlockSpec((1, N, LANES), lambda g, i: (g, 0, 0))
    return pl.pallas_call(
        functools.partial(_rwkv_scan_kernel, tc=tc),
        grid=(G, T // tc),
        in_specs=[seq] * 6 + [st, cst, cst, cst],
        out_specs=[seq, st],
        out_shape=[jax.ShapeDtypeStruct((G, T, N, LANES), F32), jax.ShapeDtypeStruct((G, N, N, LANES), F32)],
        scratch_shapes=[pltpu.VMEM((N, N, LANES), F32), pltpu.VMEM((4, N, LANES), F32)],
        compiler_params=_cparams(("arbitrary", "arbitrary")),
        name="rwkv_scan",
    )(w, a, b, k, v, r, s0, lnw, lnb, rk)


_NT = (((1,), (1,)), ((), ()))


def _build_queries(qn, qr, wuk_ref, scale, rows):
    lane = lax.broadcasted_iota(jnp.int32, (rows, 2 * LANES), 1)
    slot = _div_pow2(_mod_pow2(lane, LANES), HALF_ROPE)
    qrf = qr.astype(F32) * scale
    parts = []
    for h in range(B_HEADS):
        ql = jnp.dot(qn[:, h * B_NOPE_DIM:(h + 1) * B_NOPE_DIM], wuk_ref[h], preferred_element_type=F32) * scale
        parts.append(jnp.concatenate([ql, jnp.where(slot == h, qrf, 0.0)], axis=-1).astype(BF16))
    return parts


def _mla_prompt_kernel(qn_ref, qr_ref, kf_ref, wuk_ref, wuv_ref, o_ref, q_s, s_s, p_s, m_s, l_s, a_s, acc_s,
                       *, tq, scale, row_block):
    qi = pl.program_id(1)
    parts = _build_queries(qn_ref[...], qr_ref[...], wuk_ref, scale, tq)
    for h in range(B_HEADS):
        q_s[h * tq:(h + 1) * tq, :] = parts[h]
    rows = B_HEADS * tq
    m_s[...] = jnp.full((rows, LANES), -jnp.inf, F32)
    l_s[...] = jnp.zeros((rows, LANES), F32)
    acc_s[...] = jnp.zeros((rows, KV_RANK), F32)

    def keys(kc):
        return pl.ds(pl.multiple_of(kc * tq, tq), tq)

    def scores(kc, slot):
        s_s[slot] = lax.dot_general(q_s[...], kf_ref[keys(kc), :], _NT, preferred_element_type=F32)

    def lane_tiles(s):
        return [s[:, c * LANES:(c + 1) * LANES] for c in range(tq // LANES)]

    def softmax_pv(kc, slot, masked):
        blocks = [slice(rb * row_block, (rb + 1) * row_block) for rb in range(rows // row_block)]
        for rb, blk in enumerate(blocks):
            s = s_s[slot, blk, :]
            if masked:
                qt = _mod_pow2(rb * row_block + lax.broadcasted_iota(jnp.int32, (row_block, tq), 0), tq)
                kpos = lax.broadcasted_iota(jnp.int32, (row_block, tq), 1)
                s = jnp.where(kpos <= qt, s, -jnp.inf)
                s_s[slot, blk, :] = s
            mx = functools.reduce(jnp.maximum, lane_tiles(s))
            m_prev = m_s[blk, :]
            m_new = jnp.maximum(m_prev, jnp.max(mx, axis=-1, keepdims=True))
            a_s[blk, :] = jnp.exp2(m_prev - m_new)
            m_s[blk, :] = m_new
        for blk in blocks:
            m_new = m_s[blk, :]
            ps = [jnp.exp2(t - m_new) for t in lane_tiles(s_s[slot, blk, :])]
            l_s[blk, :] = a_s[blk, :] * l_s[blk, :] + functools.reduce(jnp.add, ps)
            p_s[blk, :] = jnp.concatenate(ps, axis=-1).astype(BF16)
        pv = jnp.dot(p_s[...], kf_ref[keys(kc), 0:KV_RANK], preferred_element_type=F32)
        alpha = a_s[...]
        acc_s[...] = acc_s[...] * jnp.concatenate([alpha] * (KV_RANK // LANES), axis=-1) + pv

    scores(0, 0)

    def pair(i, carry):
        scores(2 * i + 1, 1)
        softmax_pv(2 * i, 0, False)
        scores(2 * i + 2, 0)
        softmax_pv(2 * i + 1, 1, False)
        return carry

    lax.fori_loop(0, qi // 2, pair, 0)

    @pl.when(qi % 2 == 1)
    def _():
        scores(qi, 1)
        softmax_pv(qi - 1, 0, False)
        softmax_pv(qi, 1, True)

    @pl.when(qi % 2 == 0)
    def _():
        softmax_pv(qi, 0, True)

    o = (acc_s[...] * (1.0 / jnp.sum(l_s[...], axis=-1, keepdims=True))).astype(BF16)
    outs = [jnp.dot(o[h * tq:(h + 1) * tq, :], wuv_ref[h], preferred_element_type=F32) for h in range(B_HEADS)]
    o_ref[...] = jnp.concatenate(outs, axis=-1).astype(o_ref.dtype)


def _mla_prompt(qn, qr, kfull, wuk, wuv, tq):
    B, T, _ = qn.shape
    scale = (B_NOPE_DIM + B_ROPE_DIM) ** -0.5 * math.log2(math.e)
    rows = B_HEADS * tq
    tile = lambda c: pl.BlockSpec((None, tq, c), lambda b, i: (b, i, 0))
    stat = pltpu.VMEM((rows, LANES), F32)
    return pl.pallas_call(
        functools.partial(_mla_prompt_kernel, tq=tq, scale=scale, row_block=min(128, tq)),
        grid=(B, T // tq),
        in_specs=[tile(A_WIDTH), tile(2 * LANES), pl.BlockSpec((None, T, KFULL), lambda b, i: (b, 0, 0)),
                  _const_spec((B_HEADS, B_NOPE_DIM, KV_RANK)), _const_spec((B_HEADS, KV_RANK, B_V_DIM))],
        out_specs=tile(B_WIDTH),
        out_shape=jax.ShapeDtypeStruct((B, T, B_WIDTH), BF16),
        scratch_shapes=[pltpu.VMEM((rows, KFULL), BF16), pltpu.VMEM((2, rows, tq), F32),
                        pltpu.VMEM((rows, tq), BF16), stat, stat, stat, pltpu.VMEM((rows, KV_RANK), F32)],
        compiler_params=_cparams(("arbitrary", "arbitrary")),
        name="mla_prompt",
    )(qn, qr, kfull, wuk, wuv)


def _mla_decode_kernel(pt_ref, qn_ref, qr_ref, kfn_ref, wuk_ref, wuv_ref, ckv_hbm, krt_hbm,
                       o_ref, ckv_buf, krt_buf, kv16, s_s, m_s, l_s, acc_s, sem,
                       *, layer, n_pages, T, scale, ck, n_streams):
    i = pl.program_id(0)
    n = pl.num_programs(0)
    L = n_pages * PAGE_SIZE

    def page_copies(sl, p, page):
        rows = pl.ds(pl.multiple_of(p * PAGE_SIZE, PAGE_SIZE), PAGE_SIZE)
        return (pltpu.make_async_copy(ckv_hbm.at[layer, page], ckv_buf.at[sl, rows], sem.at[0, sl]),
                pltpu.make_async_copy(krt_hbm.at[layer, page], krt_buf.at[sl, :, rows], sem.at[1, sl]))

    def issue(bb, sl):
        def body(p, carry):
            for cp in page_copies(sl, p, pt_ref[bb, p]):
                cp.start()
            return carry
        lax.fori_loop(0, n_pages, body, 0, unroll=math.gcd(n_pages, 8))

    def wait(sl):
        for p in range(n_pages):
            for cp in page_copies(sl, p, 0):
                cp.wait()

    def attend(sl):
        qr = qr_ref[sl]
        kfn = kfn_ref[sl]
        parts = _build_queries(qn_ref[sl], qr, wuk_ref, scale, T)
        qfull = jnp.concatenate(parts, axis=0)
        qlat = qfull[:, 0:KV_RANK]
        qrs = (qr.astype(F32) * scale)
        qrope = jnp.concatenate(
            [jnp.concatenate([qrs[:, h * HALF_ROPE:(h + 1) * HALF_ROPE],
                              qrs[:, LANES + h * HALF_ROPE:LANES + (h + 1) * HALF_ROPE]], axis=-1)
             for h in range(B_HEADS)], axis=0).astype(BF16)
        nrows = B_HEADS * T
        per_stream = L // n_streams
        n_ck = per_stream // ck
        m_s[...] = jnp.full(m_s.shape, -jnp.inf, F32)
        l_s[...] = jnp.zeros(l_s.shape, F32)
        acc_s[...] = jnp.zeros(acc_s.shape, F32)

        def scores(st, kc, slot):
            keys = slice(st * per_stream + kc * ck, st * per_stream + (kc + 1) * ck)
            kv16[st, slot] = ckv_buf[sl, keys, :].astype(BF16)
            s_s[st, slot] = (lax.dot_general(qlat, kv16[st, slot], _NT, preferred_element_type=F32)
                             + jnp.dot(qrope, krt_buf[sl, :, keys].astype(BF16), preferred_element_type=F32))

        def softmax_pv(st, slot):
            s = s_s[st, slot]
            tiles = [s[:, c * LANES:(c + 1) * LANES] for c in range(ck // LANES)]
            m_prev = m_s[st]
            m_new = jnp.maximum(m_prev, jnp.max(functools.reduce(jnp.maximum, tiles), axis=-1, keepdims=True))
            alpha = jnp.exp(m_prev - m_new)
            ps = [jnp.exp(t - m_new) for t in tiles]
            l_s[st] = alpha * l_s[st] + functools.reduce(jnp.add, ps)
            m_s[st] = m_new
            pv = jnp.dot(jnp.concatenate(ps, axis=-1).astype(BF16), kv16[st, slot], preferred_element_type=F32)
            acc_s[st] = acc_s[st] * jnp.concatenate([alpha] * (KV_RANK // LANES), axis=-1) + pv

        streams = range(n_streams)
        for st in streams:
            scores(st, 0, 0)
        for kc in range(n_ck):
            if kc + 1 < n_ck:
                for st in streams:
                    scores(st, kc + 1, kc + 1)
            for st in streams:
                softmax_pv(st, kc)
        s_new = lax.dot_general(qfull, kfn, _NT, preferred_element_type=F32)
        qt = _mod_pow2(lax.broadcasted_iota(jnp.int32, (nrows, T), 0), T)
        kt = lax.broadcasted_iota(jnp.int32, (nrows, T), 1)
        s_new = jnp.where(kt <= qt, s_new, -jnp.inf)
        m_parts = [m_s[st][:, 0:1] for st in streams] + [jnp.max(s_new, axis=-1, keepdims=True)]
        m = functools.reduce(jnp.maximum, m_parts)
        p_new = jnp.exp(s_new - m)
        l = jnp.sum(p_new, axis=-1, keepdims=True)
        o = jnp.dot(p_new.astype(BF16), kfn[:, 0:KV_RANK], preferred_element_type=F32)
        for st in streams:
            w_st = jnp.exp(m_parts[st] - m)
            l = l + w_st * jnp.sum(l_s[st], axis=-1, keepdims=True)
            o = o + w_st * acc_s[st]
        o = (o / l).astype(BF16)
        outs = [jnp.dot(o[h * T:(h + 1) * T, :], wuv_ref[h], preferred_element_type=F32) for h in range(B_HEADS)]
        o_ref[sl] = jnp.concatenate(outs, axis=-1).astype(o_ref.dtype)

    @pl.when(i == 0)
    def _():
        issue(0, 0)

    issue(2 * i + 1, 1)
    wait(0)
    attend(0)

    @pl.when(i + 1 < n)
    def _():
        issue(2 * i + 2, 0)

    wait(1)
    attend(1)


def _mla_decode(page_table, qn, qr, kfn, wuk, wuv, cache_kv, cache_krt, layer):
    B, T, _ = qn.shape
    assert B % 2 == 0
    n_pages = page_table.shape[1]
    L = n_pages * PAGE_SIZE
    scale = (B_NOPE_DIM + B_ROPE_DIM) ** -0.5
    n_streams = math.gcd(n_pages // 2, 4)
    ck = math.gcd(L // (2 * n_streams), 4 * PAGE_SIZE)
    nrows = B_HEADS * T
    stat = pltpu.VMEM((n_streams, nrows, LANES), F32)
    tile = lambda c: pl.BlockSpec((2, T, c), lambda b, pt: (b, 0, 0))
    cst = lambda shape: pl.BlockSpec(shape, lambda b, pt: (0,) * len(shape))
    grid_spec = pltpu.PrefetchScalarGridSpec(
        num_scalar_prefetch=1,
        grid=(B // 2,),
        in_specs=[tile(A_WIDTH), tile(2 * LANES), tile(KFULL),
                  cst((B_HEADS, B_NOPE_DIM, KV_RANK)), cst((B_HEADS, KV_RANK, B_V_DIM)),
                  pl.BlockSpec(memory_space=pl.ANY), pl.BlockSpec(memory_space=pl.ANY)],
        out_specs=tile(B_WIDTH),
        scratch_shapes=[pltpu.VMEM((2, L, KV_RANK), F32), pltpu.VMEM((2, B_ROPE_DIM, L), F32),
                        pltpu.VMEM((n_streams, L // (n_streams * ck), ck, KV_RANK), BF16),
                        pltpu.VMEM((n_streams, L // (n_streams * ck), nrows, ck), F32),
                        stat, stat, pltpu.VMEM((n_streams, nrows, KV_RANK), F32), pltpu.SemaphoreType.DMA((2, 2))],
    )
    return pl.pallas_call(
        functools.partial(_mla_decode_kernel, layer=layer, n_pages=n_pages, T=T, scale=scale, ck=ck,
                          n_streams=n_streams),
        grid_spec=grid_spec,
        out_shape=jax.ShapeDtypeStruct((B, T, B_WIDTH), BF16),
        compiler_params=_cparams(("arbitrary",)),
        name="mla_decode",
    )(page_table, qn, qr, kfn, wuk, wuv, cache_kv, cache_krt)


def _pool_kernel(pc_ref, st_ref, wbd_ref, scale_ref, y_ref, newst_ref, e_s, s2_s, s4_s, s8_s, *, ts, T, pos0):
    L = T * ts
    P = (POOL_PREFIX + 1) * ts
    pc = pc_ref[...].reshape(L, C_WIDTH)
    e_s[P - POOL_PREFIX * ts:P, :] = st_ref[...].reshape(POOL_PREFIX * ts, C_WIDTH)
    e_s[P:P + L, :] = pc
    newst_ref[...] = e_s[P + L - POOL_PREFIX * ts:P + L, :].reshape(newst_ref.shape)
    s2_s[P - 14 * ts:P + L, :] = e_s[P - 14 * ts:P + L, :] + e_s[P - 15 * ts:P + L - ts, :]
    s4_s[P - 12 * ts:P + L, :] = s2_s[P - 12 * ts:P + L, :] + s2_s[P - 14 * ts:P + L - 2 * ts, :]
    s8_s[P - 8 * ts:P + L, :] = s4_s[P - 8 * ts:P + L, :] + s4_s[P - 12 * ts:P + L - 4 * ts, :]
    s16 = s8_s[P:P + L, :] + s8_s[P - 8 * ts:P + L - 8 * ts, :]
    lane = lax.broadcasted_iota(jnp.int32, (L, C_WIDTH), 1)
    grp = _div_pow2(lane, C_GROUP_DIM)
    win = jnp.where(grp == 0, s2_s[P:P + L, :],
                    jnp.where(grp == 1, s4_s[P:P + L, :], jnp.where(grp == 2, s8_s[P:P + L, :], s16)))
    wsize = jnp.where(grp == 0, 2, jnp.where(grp == 1, 4, jnp.where(grp == 2, 8, 16)))
    pos = pos0 + _div_pow2(lax.broadcasted_iota(jnp.int32, (L, C_WIDTH), 0), ts)
    cnt = jnp.minimum(pos + 1, wsize).astype(F32)
    d = win / cnt - pc
    y = jnp.dot(d.astype(BF16), wbd_ref[...], preferred_element_type=F32) * scale_ref[...]
    y_ref[...] = y.astype(y_ref.dtype).reshape(y_ref.shape)


def _pool(tok, pc, pool_prev, wbd, pscale, pos0):
    ts, T = tok.ts, tok.T
    rows = (POOL_PREFIX + 1 + T) * ts
    if tok.time_major:
        grid, blk = (tok.B // tok.bb, 1), tok.spec(C_WIDTH)
    else:
        grid, blk = (tok.B, 1), pl.BlockSpec((None, T, C_WIDTH), lambda i, j: (i, 0, 0))
    return pl.pallas_call(
        functools.partial(_pool_kernel, ts=ts, T=T, pos0=pos0),
        grid=grid,
        in_specs=[blk, tok.state_spec(POOL_PREFIX, C_WIDTH), _const_spec((C_WIDTH, C_WIDTH)),
                  _const_spec((1, C_WIDTH))],
        out_specs=[blk, tok.state_spec(POOL_PREFIX, C_WIDTH)],
        out_shape=[jax.ShapeDtypeStruct(tok.shape(C_WIDTH), BF16),
                   jax.ShapeDtypeStruct(tok.state_shape(POOL_PREFIX, C_WIDTH), F32)],
        scratch_shapes=[pltpu.VMEM((rows, C_WIDTH), F32)] * 4,
        compiler_params=_cparams(("arbitrary", "arbitrary")),
        name="pool",
    )(pc, pool_prev, wbd, pscale)


def _ffn_kernel(x_ref, ya_ref, g_ref, yb_ref, yc_ref, cst_ref, wout_ref, gffn_ref, wup_ref, cw_ref, cb_ref,
                wdn_ref, gfin_ref, xo_ref, cnew_ref, gbuf_s, *, tm, ts, final):
    taps = CONV_WIDTH - 1
    pad = max(SUBLANES, taps * ts)
    it = pl.program_id(1)

    @pl.when(it == 0)
    def _():
        gbuf_s[pad - taps * ts:pad, :] = cst_ref[...].reshape(taps * ts, D_FF)

    x = x_ref[...].reshape(tm, D_MODEL)
    ya = (ya_ref[...].reshape(tm, A_WIDTH) * g_ref[...].reshape(tm, A_WIDTH)).astype(BF16)
    ycat = jnp.concatenate([ya, yb_ref[...].reshape(tm, B_WIDTH), yc_ref[...].reshape(tm, C_WIDTH)], axis=-1)
    x1 = x + jnp.dot(ycat, wout_ref[...], preferred_element_type=F32)
    xn = _rms(x1, gffn_ref[...]).astype(BF16)
    acc = jnp.zeros((tm, D_MODEL), F32)
    for c in range(D_FF // FF_CHUNK):
        lo = c * FF_CHUNK
        up = jnp.dot(xn, wup_ref[:, lo:lo + FF_CHUNK], preferred_element_type=F32)
        gate = jnp.dot(xn, wup_ref[:, D_FF + lo:D_FF + lo + FF_CHUNK], preferred_element_type=F32)
        gbuf_s[pad:pad + tm, lo:lo + FF_CHUNK] = gate
        gc = cb_ref[:, lo:lo + FF_CHUNK] + cw_ref[2:3, lo:lo + FF_CHUNK] * gate
        for j in range(taps):
            off = pad - (taps - j) * ts
            gc = gc + cw_ref[j:j + 1, lo:lo + FF_CHUNK] * gbuf_s[off:off + tm, lo:lo + FF_CHUNK]
        hh = (gc * _sigmoid(gc) * up).astype(BF16)
        acc = acc + jnp.dot(hh, wdn_ref[lo:lo + FF_CHUNK, :], preferred_element_type=F32)
    last = gbuf_s[pad + tm - taps * ts:pad + tm, :]
    cnew_ref[...] = last.reshape(cnew_ref.shape)
    gbuf_s[pad - taps * ts:pad, :] = last
    x2 = x1 + acc
    if final:
        x2 = _rms(x2, gfin_ref[...])
    xo_ref[...] = x2.reshape(xo_ref.shape)


def _ffn(tok, x, ya, g, yb, yc, conv_prev, p, gfin, final):
    tm, ts = tok.tm, tok.ts
    taps = CONV_WIDTH - 1
    pad = max(SUBLANES, taps * ts)
    return pl.pallas_call(
        functools.partial(_ffn_kernel, tm=tm, ts=ts, final=final),
        grid=tok.grid,
        in_specs=[tok.spec(D_MODEL), tok.spec(A_WIDTH), tok.spec(A_WIDTH), tok.spec(B_WIDTH), tok.spec(C_WIDTH),
                  tok.state_spec(taps, D_FF), _const_spec((D_MODEL, D_MODEL)), _const_spec((1, D_MODEL)),
                  _const_spec((D_MODEL, 2 * D_FF)), _const_spec((CONV_WIDTH, D_FF)), _const_spec((1, D_FF)),
                  _const_spec((D_FF, D_MODEL)), _const_spec((1, D_MODEL))],
        out_specs=[tok.spec(D_MODEL), tok.state_spec(taps, D_FF)],
        out_shape=[jax.ShapeDtypeStruct(tok.shape(D_MODEL), F32),
                   jax.ShapeDtypeStruct(tok.state_shape(taps, D_FF), F32)],
        scratch_shapes=[pltpu.VMEM((pad + tm, D_FF), F32)],
        compiler_params=_cparams(("arbitrary", "arbitrary")),
        name="outproj_ffn",
    )(x, ya, g, yb, yc, conv_prev, p["w_out"], p["g_ffn"], p["w_up"], p["conv_w"], p["conv_b"], p["w_down"], gfin)


def _ext_columns():
    qn0 = A_COLS
    qr0 = qn0 + B_HEADS * B_NOPE_DIM
    ckv0 = qr0 + B_HEADS * B_ROPE_DIM
    kr0 = ckv0 + KV_RANK
    pc0 = A_COLS + B_COLS
    cols = list(range(A_COLS)) + list(range(qn0, qr0)) + list(range(ckv0, kr0)) + list(range(pc0, pc0 + C_WIDTH))
    for half in range(2):
        blk = []
        for h in range(B_HEADS):
            blk += [qr0 + h * B_ROPE_DIM + half * HALF_ROPE + i for i in range(HALF_ROPE)]
        blk += [kr0 + half * HALF_ROPE + i for i in range(HALF_ROPE)]
        cols += blk + [-1] * (LANES - len(blk))
    for half in range(2):
        blk = [kr0 + half * HALF_ROPE + i for _ in range(B_HEADS) for i in range(HALF_ROPE)]
        cols += blk + [-1] * (LANES - len(blk))
    assert len(cols) == IN_EXT
    return np.asarray(cols, np.int32)


def _layer_params(l, norm_mix_g, w_in, mu_shift, decay_w0, decay_w2, iclr_a0, iclr_a2, gate_g2, k_k, k_a, r_k,
                  lnx_w, lnx_b, kv_norm_g, w_uk, w_uv, pool_w, pool_scale, w_out, norm_ffn_g, w_up, conv_w,
                  conv_b, w_down):
    cols = _ext_columns()
    w_ext = jnp.where(cols[None, :] >= 0, w_in[l][:, np.maximum(cols, 0)], 0.0).astype(BF16)
    lora_cat = jnp.zeros((LANES, 2 * A_WIDTH), F32)
    lora_cat = lora_cat.at[0:DECAY_LORA, 0:A_WIDTH].set(decay_w2[l])
    lora_cat = lora_cat.at[DECAY_LORA:DECAY_LORA + ICLR_LORA, A_WIDTH:].set(iclr_a2[l])
    head = np.arange(A_WIDTH) // A_HEAD_DIM
    head_ones = jnp.asarray((head[:, None] == head[None, :]).astype(np.float32), BF16)
    wbd = jnp.zeros((C_WIDTH, C_WIDTH), F32)
    for gi in range(C_GROUPS):
        sl = slice(gi * C_GROUP_DIM, (gi + 1) * C_GROUP_DIM)
        wbd = wbd.at[sl, sl].set(pool_w[l, gi])
    row = lambda v: v.reshape(1, -1)
    return dict(
        g_mix=row(norm_mix_g[l]), w_ext=w_ext, kv_g=row(kv_norm_g[l]),
        mu=row(mu_shift[l]), lora_cat=lora_cat.astype(BF16), w0=row(decay_w0[l]), a0=row(iclr_a0[l]),
        g2=gate_g2[l].astype(BF16), k_k=row(k_k[l]), k_a=row(k_a[l]), head_ones=head_ones,
        r_k=r_k[l], lnx_w=lnx_w[l].reshape(A_HEADS, A_HEAD_DIM), lnx_b=lnx_b[l].reshape(A_HEADS, A_HEAD_DIM),
        wuk=jnp.transpose(w_uk[l], (1, 2, 0)).astype(BF16), wuv=jnp.transpose(w_uv[l], (1, 0, 2)).astype(BF16),
        pool_wbd=wbd.astype(BF16), pool_scale=row(pool_scale[l]),
        w_out=w_out[l].astype(BF16), g_ffn=row(norm_ffn_g[l]), w_up=w_up[l].astype(BF16),
        conv_w=conv_w[l], conv_b=row(conv_b[l]), w_down=w_down[l].astype(BF16),
    )


def _rope_tables(pos):
    inv = ROPE_BASE ** (-jnp.arange(0, B_ROPE_DIM, 2, dtype=F32) / B_ROPE_DIM)
    ang = pos.astype(F32)[:, None] * inv[None, :]
    reps = LANES // HALF_ROPE
    return jnp.tile(jnp.cos(ang), (1, reps)), jnp.tile(jnp.sin(ang), (1, reps))


def _to_scan(tok, x):
    assert tok.time_major
    x = x.reshape(tok.T, tok.B, A_HEADS, A_HEAD_DIM)
    return jnp.transpose(x, (2, 0, 3, 1))


def _lane_maps(nb):
    lane_of = np.asarray([[_prompt_lane(b, h, nb) for h in range(A_HEADS)] for b in range(nb)], np.int32)
    lanes_b = np.zeros(nb * A_HEADS, np.int32)
    lanes_h = np.zeros(nb * A_HEADS, np.int32)
    for b in range(nb):
        for h in range(A_HEADS):
            lanes_b[lane_of[b, h]], lanes_h[lane_of[b, h]] = b, h
    return lane_of, lanes_b, lanes_h


def _from_scan(tok, y):
    H, N = A_HEADS, A_HEAD_DIM
    if tok.time_major:
        return jnp.transpose(y, (1, 3, 0, 2)).reshape(tok.T, tok.B, H * N)
    lane_of, _, _ = _lane_maps(tok.B)
    y = y[0][:, :, lane_of]
    return jnp.transpose(y, (2, 0, 3, 1)).reshape(tok.B, tok.T, H * N)


def _pad_lanes(x):
    return jnp.pad(x, [(0, 0)] * (x.ndim - 1) + [(0, LANES - x.shape[-1])])


def _state_to_scan(tok, s):
    if tok.time_major:
        return jnp.transpose(s, (1, 3, 2, 0))
    _, lanes_b, lanes_h = _lane_maps(tok.B)
    return _pad_lanes(jnp.transpose(s, (3, 2, 0, 1))[:, :, lanes_b, lanes_h])[None]


def _state_from_scan(tok, s):
    if tok.time_major:
        return jnp.transpose(s, (3, 0, 2, 1))
    lane_of, _, _ = _lane_maps(tok.B)
    return jnp.transpose(s[0][:, :, lane_of], (2, 3, 1, 0))


def _head_consts(tok, p):
    def lay(v):
        if tok.time_major:
            return jnp.broadcast_to(v[:, :, None], (A_HEADS, A_HEAD_DIM, LANES))
        _, _, lanes_h = _lane_maps(tok.B)
        return _pad_lanes(v.T[:, lanes_h])[None]
    return lay(p["lnx_w"]), lay(p["lnx_b"]), lay(p["r_k"])


def _layer(tok, x, cos, sin, pos0, shift_prev, wkv_prev, pool_prev, conv_prev, p, gfin, final, scan_tc, attend):
    pa, qn, kvlat, pc, qr, kfull, krope = _inproj(tok, x, p["g_mix"], p["w_ext"], p["kv_g"], cos, sin)
    if tok.time_major:
        r, w, k, v, a, b, g, shift_new = _rwkv_prep(tok, pa, shift_prev, p)
        r, w, k, v, a, b = (_to_scan(tok, t) for t in (r, w, k, v, a, b))
    else:
        r, w, k, v, a, b, g, shift_new = _rwkv_prep_scan(pa, shift_prev, p, scan_tc)
    lnw, lnb, rk = _head_consts(tok, p)
    y, s_new = _rwkv_scan(w, a, b, k, v, r, _state_to_scan(tok, wkv_prev), lnw, lnb, rk, scan_tc)
    ya = _from_scan(tok, y)
    yb = attend(qn, qr, kfull)
    yc, pool_new = _pool(tok, pc, pool_prev, p["pool_wbd"], p["pool_scale"], pos0)
    x, conv_new = _ffn(tok, x, ya, g, yb, yc, conv_prev, p, gfin, final)
    return x, (kvlat, krope, _state_from_scan(tok, s_new), shift_new, pool_new, conv_new)


def kernel(x_prompt, x_sample, cache_kv_latent, cache_k_rope, page_table, state_wkv, state_shift, state_pool,
           state_conv, norm_mix_g, w_in, mu_shift, decay_w0, decay_w2, iclr_a0, iclr_a2, gate_g2, k_k, k_a, r_k,
           lnx_w, lnx_b, kv_norm_g, w_uk, w_uv, pool_w, pool_scale, w_out, norm_ffn_g, w_up, conv_w, conv_b,
           w_down, norm_final_g):
    Bp, Tp, _ = x_prompt.shape
    Bs, Ts, _ = x_sample.shape
    past_len = page_table.shape[1] * PAGE_SIZE
    tokp = _Tok(False, Bp, Tp, min(512, Tp))
    toks = _Tok(True, Bs, Ts, min(32, Bs))
    cos_p, sin_p = _rope_tables(jnp.arange(Tp, dtype=jnp.int32))
    pos_s = past_len + jnp.repeat(jnp.arange(Ts, dtype=jnp.int32), toks.bb)
    cos_s, sin_s = _rope_tables(pos_s)
    gfin = norm_final_g.reshape(1, -1)
    tm_first = lambda t: jnp.swapaxes(t, 0, 1)
    cache_krt = jnp.swapaxes(cache_k_rope, 2, 3)

    xp = x_prompt
    xs = tm_first(x_sample)
    zeros = lambda *shape: jnp.zeros(shape, F32)
    outs_p, outs_s = [], []
    for l in range(DEPTH):
        p = _layer_params(l, norm_mix_g, w_in, mu_shift, decay_w0, decay_w2, iclr_a0, iclr_a2, gate_g2, k_k, k_a,
                          r_k, lnx_w, lnx_b, kv_norm_g, w_uk, w_uv, pool_w, pool_scale, w_out, norm_ffn_g, w_up,
                          conv_w, conv_b, w_down)
        final = l == DEPTH - 1

        attend_p = lambda qn, qr, kfull: _mla_prompt(qn, qr, kfull, p["wuk"], p["wuv"], min(256, Tp))
        xp, st_p = _layer(tokp, xp, cos_p, sin_p, 0, zeros(Bp, A_COLS),
                          zeros(Bp, A_HEADS, A_HEAD_DIM, A_HEAD_DIM), zeros(Bp, POOL_PREFIX, C_WIDTH),
                          zeros(Bp, CONV_WIDTH - 1, D_FF), p, gfin, final, min(32, Tp), attend_p)

        def attend_s(qn, qr, kfull):
            yb = _mla_decode(page_table, tm_first(qn), tm_first(qr), tm_first(kfull), p["wuk"], p["wuv"],
                             cache_kv_latent, cache_krt, l)
            return tm_first(yb)
        xs, st_s = _layer(toks, xs, cos_s, sin_s, past_len, state_shift[l][None], state_wkv[l],
                          tm_first(state_pool[l]), tm_first(state_conv[l]), p, gfin, final, Ts, attend_s)
        outs_p.append(st_p)
        outs_s.append(st_s)

    stack = lambda outs, i, f: jnp.stack([f(o[i]) for o in outs], axis=0)
    ident = lambda t: t
    return (xp, tm_first(xs),
            stack(outs_p, 0, ident), stack(outs_p, 1, ident), stack(outs_p, 2, ident),
            stack(outs_p, 3, ident), stack(outs_p, 4, ident), stack(outs_p, 5, ident),
            stack(outs_s, 0, tm_first), stack(outs_s, 1, tm_first), stack(outs_s, 2, ident),
            stack(outs_s, 3, lambda t: t[0]), stack(outs_s, 4, tm_first), stack(outs_s, 5, tm_first))
```

```python
import functools
import math

import jax
import jax.numpy as jnp
import numpy as np
from jax import lax
from jax.experimental import pallas as pl
from jax.experimental.pallas import tpu as pltpu

F32 = jnp.float32
BF16 = jnp.bfloat16

D_MODEL = 1024
DEPTH = 2
PAGE_SIZE = 128
A_HEADS = 6
A_HEAD_DIM = 64
A_WIDTH = A_HEADS * A_HEAD_DIM
DECAY_LORA = 64
ICLR_LORA = 64
GATE_LORA = 128
A_COLS = 3 * A_WIDTH + DECAY_LORA + ICLR_LORA + GATE_LORA
B_HEADS = 6
B_NOPE_DIM = 64
B_ROPE_DIM = 32
B_V_DIM = 64
B_WIDTH = B_HEADS * B_V_DIM
KV_RANK = 256
ROPE_BASE = 10000.0
B_COLS = B_HEADS * (B_NOPE_DIM + B_ROPE_DIM) + KV_RANK + B_ROPE_DIM
C_GROUPS = 4
C_GROUP_DIM = 64
C_WIDTH = C_GROUPS * C_GROUP_DIM
POOL_WINDOWS = (2, 4, 8, 16)
POOL_PREFIX = 15
D_FF = 2816
CONV_WIDTH = 3
NORM_EPS = 1e-6
GN_EPS = A_HEAD_DIM * 1e-5

LANES = 128
SUBLANES = 8
HALF_ROPE = B_ROPE_DIM // 2
IN_EXT = A_COLS + B_HEADS * B_NOPE_DIM + KV_RANK + C_WIDTH + 4 * LANES
OFF_QN = A_COLS
OFF_CKV = OFF_QN + B_HEADS * B_NOPE_DIM
OFF_PC = OFF_CKV + KV_RANK
OFF_ROPE = OFF_PC + C_WIDTH
KFULL = KV_RANK + 2 * LANES
MXU_TILE = 256
FF_CHUNKS = (6 * MXU_TILE, 5 * MXU_TILE)
VMEM_LIMIT = 56 * 1024 * 1024


def _cparams(sem):
    return pltpu.CompilerParams(dimension_semantics=sem, vmem_limit_bytes=VMEM_LIMIT)


def _const_spec(shape):
    nd = len(shape)
    return pl.BlockSpec(shape, lambda *_: (0,) * nd, pipeline_mode=pl.Buffered(1))


class _Tok:
    def __init__(self, time_major, B, T, tile):
        self.time_major, self.B, self.T = time_major, B, T
        if time_major:
            self.bb = tile
            self.tm, self.ts = T * tile, tile
            self.grid = (B // tile, 1)
        else:
            self.tm, self.ts = tile, 1
            self.grid = (B, T // tile)
        self.n = B * T

    def shape(self, C):
        return (self.T, self.B, C) if self.time_major else (self.B, self.T, C)

    def spec(self, C):
        if self.time_major:
            return pl.BlockSpec((self.T, self.bb, C), lambda i, j: (0, i, 0))
        return pl.BlockSpec((None, self.tm, C), lambda i, j: (i, j, 0))

    def state_shape(self, steps, C):
        return (steps, self.B, C) if self.time_major else (self.B, steps, C)

    def state_spec(self, steps, C):
        if self.time_major:
            return pl.BlockSpec((steps, self.bb, C), lambda i, j: (0, i, 0))
        return pl.BlockSpec((None, steps, C), lambda i, j: (i, 0, 0))


def _rms(x, g):
    return x * lax.rsqrt(jnp.mean(x * x, axis=-1, keepdims=True) + NORM_EPS) * g


def _sigmoid(x):
    return 1.0 / (1.0 + jnp.exp(-x))


def _div_pow2(x, d):
    assert d & (d - 1) == 0
    return lax.shift_right_logical(x, int(math.log2(d)))


def _mod_pow2(x, d):
    assert d & (d - 1) == 0
    return lax.bitwise_and(x, d - 1)


def _inproj_kernel(x_ref, g_ref, w_ref, kvg_ref, cos_ref, sin_ref,
                   pa_ref, qn_ref, kvlat_ref, pc_ref, qr_ref, kfull_ref, krope_ref, *, tm):
    x = x_ref[...].reshape(tm, D_MODEL)
    xb = _rms(x, g_ref[...]).astype(BF16)

    def proj(lo, hi):
        return jnp.dot(xb, w_ref[:, lo:hi], preferred_element_type=F32)

    assert OFF_CKV % MXU_TILE == 0 and OFF_ROPE % MXU_TILE == 0 and IN_EXT % MXU_TILE == 0
    head = proj(0, OFF_CKV)
    pa_ref[...] = head[:, 0:A_COLS].reshape(pa_ref.shape)
    qn_ref[...] = head[:, OFF_QN:OFF_CKV].astype(BF16).reshape(qn_ref.shape)
    mid = proj(OFF_CKV, OFF_ROPE)
    ckv = _rms(mid[:, 0:KV_RANK], kvg_ref[...])
    kvlat_ref[...] = ckv.reshape(kvlat_ref.shape)
    pc_ref[...] = mid[:, KV_RANK:KV_RANK + C_WIDTH].reshape(pc_ref.shape)
    rr = proj(OFF_ROPE, IN_EXT)
    cos = cos_ref[...]
    sin = sin_ref[...]
    q1, q2 = rr[:, 0:LANES], rr[:, LANES:2 * LANES]
    k1, k2 = rr[:, 2 * LANES:3 * LANES], rr[:, 3 * LANES:4 * LANES]
    o1 = q1 * cos - q2 * sin
    o2 = q1 * sin + q2 * cos
    qr = jnp.concatenate([o1, o2], axis=-1).astype(BF16)
    qr_ref[...] = qr.reshape(qr_ref.shape)
    kf = jnp.concatenate([ckv, k1 * cos - k2 * sin, k1 * sin + k2 * cos], axis=-1).astype(BF16)
    kfull_ref[...] = kf.reshape(kfull_ref.shape)
    kr_lo = B_HEADS * HALF_ROPE
    krope = jnp.concatenate([o1[:, kr_lo:kr_lo + HALF_ROPE], o2[:, kr_lo:kr_lo + HALF_ROPE]], axis=-1)
    krope_ref[...] = krope.reshape(krope_ref.shape)


def _inproj(tok, x, g, w_ext, kvg, cos, sin):
    tm = tok.tm
    n_tab = cos.shape[0] // tm
    tab_spec = pl.BlockSpec((tm, LANES), lambda i, j: (j % n_tab, 0))
    widths = (A_COLS, A_WIDTH, KV_RANK, C_WIDTH, 2 * LANES, KFULL, B_ROPE_DIM)
    dtypes = (F32, BF16, F32, F32, BF16, BF16, F32)
    return pl.pallas_call(
        functools.partial(_inproj_kernel, tm=tm),
        grid=tok.grid,
        in_specs=[tok.spec(D_MODEL), _const_spec((1, D_MODEL)), _const_spec((D_MODEL, IN_EXT)),
                  _const_spec((1, KV_RANK)), tab_spec, tab_spec],
        out_specs=[tok.spec(c) for c in widths],
        out_shape=[jax.ShapeDtypeStruct(tok.shape(c), d) for c, d in zip(widths, dtypes)],
        compiler_params=_cparams(("arbitrary", "arbitrary")),
        name="inproj",
    )(x, g, w_ext, kvg, cos, sin)


def _rwkv_token_math(pa, prev, mu_ref, wcat_ref, w0_ref, a0_ref, g2_ref, kk_ref, ka_ref, ones_ref):
    s = pa + (prev - pa) * mu_ref[...]
    W = A_WIDTH
    r, k, v = s[:, 0:W], s[:, W:2 * W], s[:, 2 * W:3 * W]
    lo_in = s[:, 3 * W:3 * W + LANES]
    gd = s[:, 3 * W + LANES:3 * W + 2 * LANES]
    lane = lax.broadcasted_iota(jnp.int32, lo_in.shape, 1)
    lo_in = jnp.where(lane < DECAY_LORA, jnp.tanh(lo_in), lo_in)
    lo = jnp.dot(lo_in.astype(BF16), wcat_ref[...], preferred_element_type=F32)
    z = -(w0_ref[...] + lo[:, 0:W])
    softplus = jnp.maximum(z, 0.0) + jnp.log(1.0 + jnp.exp(-jnp.abs(z)))
    decay = jnp.exp(-jnp.exp(-softplus - 0.5))
    a = _sigmoid(a0_ref[...] + lo[:, W:2 * W])
    g = jnp.dot(_sigmoid(gd).astype(BF16), g2_ref[...], preferred_element_type=F32)
    kk = k * kk_ref[...]
    kk2 = kk * kk
    hi = kk2.astype(BF16)
    lo2 = (kk2 - hi.astype(F32)).astype(BF16)
    ones = ones_ref[...]
    ss = (jnp.dot(hi, ones, preferred_element_type=F32) + jnp.dot(lo2, ones, preferred_element_type=F32))
    kk = kk * lax.rsqrt(jnp.maximum(ss, 1e-24))
    kmod = k * (1.0 + (a - 1.0) * ka_ref[...])
    return r, decay, kmod, v, -kk, kk * a, g


def _rwkv_prep_kernel(pa_ref, st_ref, mu_ref, wcat_ref, w0_ref, a0_ref, g2_ref, kk_ref, ka_ref, ones_ref,
                      r_ref, w_ref, k_ref, v_ref, a_ref, b_ref, g_ref, newst_ref, buf_ref, *, tm, ts):
    pad = max(SUBLANES, ts)
    it = pl.program_id(1)

    @pl.when(it == 0)
    def _():
        buf_ref[pad - ts:pad, :] = st_ref[...].reshape(ts, A_COLS)

    pa = pa_ref[...].reshape(tm, A_COLS)
    buf_ref[pad:pad + tm, :] = pa
    prev = buf_ref[pad - ts:pad - ts + tm, :]
    last = pa[tm - ts:tm, :]
    buf_ref[pad - ts:pad, :] = last
    newst_ref[...] = last.reshape(newst_ref.shape)
    vals = _rwkv_token_math(pa, prev, mu_ref, wcat_ref, w0_ref, a0_ref, g2_ref, kk_ref, ka_ref, ones_ref)
    for ref, val in zip((r_ref, w_ref, k_ref, v_ref, a_ref, b_ref, g_ref), vals):
        ref[...] = val.reshape(ref.shape)


def _prompt_lane(b, h, nb):
    return (h % 2) * (A_HEADS // 2) * nb + (h // 2) * nb + b


def _rwkv_prep_scan_kernel(pa_ref, st_ref, mu_ref, wcat_ref, w0_ref, a0_ref, g2_ref, kk_ref, ka_ref, ones_ref,
                           r_ref, w_ref, k_ref, v_ref, a_ref, b_ref, g_ref, newst_ref, buf_ref, q_ref, *, nb, tt):
    it = pl.program_id(0)

    @pl.when(it == 0)
    def _():
        buf_ref[0:nb, :] = st_ref[...]

    for t in range(tt):
        buf_ref[(t + 1) * nb:(t + 2) * nb, :] = pa_ref[:, t, :]
    pa = buf_ref[nb:(tt + 1) * nb, :]
    prev = buf_ref[0:tt * nb, :]
    vals = _rwkv_token_math(pa, prev, mu_ref, wcat_ref, w0_ref, a0_ref, g2_ref, kk_ref, ka_ref, ones_ref)
    last = buf_ref[tt * nb:(tt + 1) * nb, :]
    buf_ref[0:nb, :] = last
    newst_ref[...] = last
    g = vals[6]
    for t in range(tt):
        g_ref[:, t, :] = g[t * nb:(t + 1) * nb, :]
    for qi in range(6):
        q_ref[qi] = vals[qi]
    outs = (r_ref, w_ref, k_ref, v_ref, a_ref, b_ref)
    half = A_HEADS // 2
    win = half * nb
    N = A_HEAD_DIM

    def pair(pi, carry):
        for qi, o_ref in enumerate(outs):
            rows = q_ref[qi, pl.ds(pl.multiple_of(pi * 2 * nb, 2 * nb), 2 * nb), :]
            pieces = [rows[s * nb:(s + 1) * nb, hp * LANES:(hp + 1) * LANES] for s in range(2) for hp in range(half)]
            m = jnp.concatenate(pieces + [jnp.zeros((LANES - 2 * win, LANES), F32)], axis=0)
            mt = m.T
            unused = jnp.full((N, LANES - 2 * win), 1.0 if o_ref is w_ref else 0.0, F32)
            for s in range(2):
                tile = jnp.concatenate([mt[0:N, s * win:(s + 1) * win], mt[N:2 * N, s * win:(s + 1) * win], unused],
                                       axis=1)
                o_ref[0, pi * 2 + s] = tile
        return carry

    lax.fori_loop(0, tt // 2, pair, 0)


def _rwkv_prep_scan(pa, shift_prev, p, tt):
    nb, T, _ = pa.shape
    assert 2 * (A_HEADS // 2) * nb <= LANES and tt % 2 == 0
    vec = _const_spec((1, A_WIDTH))
    seq = pl.BlockSpec((1, tt, A_HEAD_DIM, LANES), lambda i: (0, i, 0, 0))
    st = pl.BlockSpec((nb, A_COLS), lambda i: (0, 0))
    return pl.pallas_call(
        functools.partial(_rwkv_prep_scan_kernel, nb=nb, tt=tt),
        grid=(T // tt,),
        in_specs=[pl.BlockSpec((nb, tt, A_COLS), lambda i: (0, i, 0)), st, _const_spec((1, A_COLS)),
                  _const_spec((LANES, 2 * A_WIDTH)), vec, vec, _const_spec((GATE_LORA, A_WIDTH)), vec, vec,
                  _const_spec((A_WIDTH, A_WIDTH))],
        out_specs=[seq] * 6 + [pl.BlockSpec((nb, tt, A_WIDTH), lambda i: (0, i, 0)), st],
        out_shape=[jax.ShapeDtypeStruct((1, T, A_HEAD_DIM, LANES), F32)] * 6
        + [jax.ShapeDtypeStruct((nb, T, A_WIDTH), F32), jax.ShapeDtypeStruct((nb, A_COLS), F32)],
        scratch_shapes=[pltpu.VMEM(((tt + 1) * nb, A_COLS), F32), pltpu.VMEM((6, tt * nb, A_WIDTH), F32)],
        compiler_params=_cparams(("arbitrary",)),
        name="rwkv_prep_scan",
    )(pa, shift_prev, p["mu"], p["lora_cat"], p["w0"], p["a0"], p["g2"], p["k_k"], p["k_a"], p["head_ones"])


def _rwkv_prep(tok, pa, shift_prev, p):
    tm, ts = tok.tm, tok.ts
    pad = max(SUBLANES, ts)
    vec = _const_spec((1, A_WIDTH))
    return pl.pallas_call(
        functools.partial(_rwkv_prep_kernel, tm=tm, ts=ts),
        grid=tok.grid,
        in_specs=[tok.spec(A_COLS), tok.state_spec(1, A_COLS), _const_spec((1, A_COLS)),
                  _const_spec((LANES, 2 * A_WIDTH)), vec, vec, _const_spec((GATE_LORA, A_WIDTH)), vec, vec,
                  _const_spec((A_WIDTH, A_WIDTH))],
        out_specs=[tok.spec(A_WIDTH)] * 7 + [tok.state_spec(1, A_COLS)],
        out_shape=[jax.ShapeDtypeStruct(tok.shape(A_WIDTH), F32)] * 7
        + [jax.ShapeDtypeStruct(tok.state_shape(1, A_COLS), F32)],
        scratch_shapes=[pltpu.VMEM((pad + tm, A_COLS), F32)],
        compiler_params=_cparams(("arbitrary", "arbitrary")),
        name="rwkv_prep",
    )(pa, shift_prev, p["mu"], p["lora_cat"], p["w0"], p["a0"], p["g2"], p["k_k"], p["k_a"], p["head_ones"])


def _rwkv_scan_kernel(w_ref, a_ref, b_ref, k_ref, v_ref, r_ref, s0_ref, lnw_ref, lnb_ref, rk_ref,
                      y_ref, sT_ref, S_ref, op_ref, *, tc):
    N = A_HEAD_DIM
    it = pl.program_id(1)

    @pl.when(it == 0)
    def _():
        S_ref[...] = s0_ref[0]

    def step(t, p_prev):
        w = w_ref[0, t]
        r = r_ref[0, t]
        k = k_ref[0, t]
        v = v_ref[0, t]
        p_new = p_prev * w
        inv = 1.0 / p_new
        rt = r * p_new
        bt = b_ref[0, t] * inv
        kt = k * inv
        op_ref[0] = a_ref[0, t] * p_prev
        op_ref[1] = rt
        op_ref[2] = bt
        op_ref[3] = kt
        sa = [jnp.zeros((N, LANES), F32) for _ in range(2)]
        yp = [jnp.zeros((N, LANES), F32) for _ in range(2)]
        for j in range(N):
            Zj = S_ref[j]
            sa[j % 2] = sa[j % 2] + Zj * op_ref[0, pl.ds(j, 1), :]
            yp[j % 2] = yp[j % 2] + Zj * op_ref[1, pl.ds(j, 1), :]
        sa = sa[0] + sa[1]
        yp = yp[0] + yp[1]
        for j in range(N):
            S_ref[j] = S_ref[j] + sa * op_ref[2, pl.ds(j, 1), :] + v * op_ref[3, pl.ds(j, 1), :]
        br = jnp.sum(bt * rt, axis=0, keepdims=True)
        kr = jnp.sum(kt * rt, axis=0, keepdims=True)
        y = yp + sa * br + v * kr
        mu = jnp.mean(y, axis=0, keepdims=True)
        d = y - mu
        var = jnp.mean(d * d, axis=0, keepdims=True)
        yn = d * lax.rsqrt(var + GN_EPS) * lnw_ref[0] + lnb_ref[0]
        bonus = jnp.sum(r * k * rk_ref[0], axis=0, keepdims=True)
        y_ref[0, t] = yn + bonus * v
        return p_new

    op_ref[0] = lax.fori_loop(0, tc, step, jnp.ones((N, LANES), F32))
    for j in range(N):
        S_ref[j] = S_ref[j] * op_ref[0, pl.ds(j, 1), :]

    @pl.when(it == pl.num_programs(1) - 1)
    def _():
        sT_ref[0] = S_ref[...]


def _rwkv_scan(w, a, b, k, v, r, s0, lnw, lnb, rk, tc):
    G, T, N, _ = w.shape
    seq = pl.BlockSpec((1, tc, N, LANES), lambda g, i: (g, i, 0, 0))
    st = pl.BlockSpec((1, N, N, LANES), lambda g, i: (g, 0, 0, 0))
    cst = pl.BlockSpec((1, N, LANES), lambda g, i: (g, 0, 0))
    return pl.pallas_call(
        functools.partial(_rwkv_scan_kernel, tc=tc),
        grid=(G, T // tc),
        in_specs=[seq] * 6 + [st, cst, cst, cst],
        out_specs=[seq, st],
        out_shape=[jax.ShapeDtypeStruct((G, T, N, LANES), F32), jax.ShapeDtypeStruct((G, N, N, LANES), F32)],
        scratch_shapes=[pltpu.VMEM((N, N, LANES), F32), pltpu.VMEM((4, N, LANES), F32)],
        compiler_params=_cparams(("arbitrary", "arbitrary")),
        name="rwkv_scan",
    )(w, a, b, k, v, r, s0, lnw, lnb, rk)


_NT = (((1,), (1,)), ((), ()))


def _build_queries(qn, qr, wuk_ref, scale, rows):
    lane = lax.broadcasted_iota(jnp.int32, (rows, 2 * LANES), 1)
    slot = _div_pow2(_mod_pow2(lane, LANES), HALF_ROPE)
    qrf = qr.astype(F32) * scale
    parts = []
    for h in range(B_HEADS):
        ql = jnp.dot(qn[:, h * B_NOPE_DIM:(h + 1) * B_NOPE_DIM], wuk_ref[h], preferred_element_type=F32) * scale
        parts.append(jnp.concatenate([ql, jnp.where(slot == h, qrf, 0.0)], axis=-1).astype(BF16))
    return parts


def _mla_prompt_kernel(qn_ref, qr_ref, kf_ref, wuk_ref, wuv_ref, o_ref, q_s, s_s, p_s, m_s, l_s, a_s, acc_s,
                       *, tq, scale, row_block):
    qi = pl.program_id(1)
    parts = _build_queries(qn_ref[...], qr_ref[...], wuk_ref, scale, tq)
    for h in range(B_HEADS):
        q_s[h * tq:(h + 1) * tq, :] = parts[h]
    rows = B_HEADS * tq
    m_s[...] = jnp.full((rows, LANES), -jnp.inf, F32)
    l_s[...] = jnp.zeros((rows, LANES), F32)
    acc_s[...] = jnp.zeros((rows, KV_RANK), F32)

    def keys(kc):
        return pl.ds(pl.multiple_of(kc * tq, tq), tq)

    def scores(kc, slot):
        s_s[slot] = lax.dot_general(q_s[...], kf_ref[keys(kc), :], _NT, preferred_element_type=F32)

    def lane_tiles(s):
        return [s[:, c * LANES:(c + 1) * LANES] for c in range(tq // LANES)]

    def softmax_pv(kc, slot, masked):
        blocks = [slice(rb * row_block, (rb + 1) * row_block) for rb in range(rows // row_block)]
        for rb, blk in enumerate(blocks):
            s = s_s[slot, blk, :]
            if masked:
                qt = _mod_pow2(rb * row_block + lax.broadcasted_iota(jnp.int32, (row_block, tq), 0), tq)
                kpos = lax.broadcasted_iota(jnp.int32, (row_block, tq), 1)
                s = jnp.where(kpos <= qt, s, -jnp.inf)
                s_s[slot, blk, :] = s
            mx = functools.reduce(jnp.maximum, lane_tiles(s))
            m_prev = m_s[blk, :]
            m_new = jnp.maximum(m_prev, jnp.max(mx, axis=-1, keepdims=True))
            a_s[blk, :] = jnp.exp2(m_prev - m_new)
            m_s[blk, :] = m_new
        for blk in blocks:
            m_new = m_s[blk, :]
            ps = [jnp.exp2(t - m_new) for t in lane_tiles(s_s[slot, blk, :])]
            l_s[blk, :] = a_s[blk, :] * l_s[blk, :] + functools.reduce(jnp.add, ps)
            p_s[blk, :] = jnp.concatenate(ps, axis=-1).astype(BF16)
        pv = jnp.dot(p_s[...], kf_ref[keys(kc), 0:KV_RANK], preferred_element_type=F32)
        alpha = a_s[...]
        acc_s[...] = acc_s[...] * jnp.concatenate([alpha] * (KV_RANK // LANES), axis=-1) + pv

    scores(0, 0)

    def pair(i, carry):
        scores(2 * i + 1, 1)
        softmax_pv(2 * i, 0, False)
        scores(2 * i + 2, 0)
        softmax_pv(2 * i + 1, 1, False)
        return carry

    lax.fori_loop(0, qi // 2, pair, 0)

    @pl.when(qi % 2 == 1)
    def _():
        scores(qi, 1)
        softmax_pv(qi - 1, 0, False)
        softmax_pv(qi, 1, True)

    @pl.when(qi % 2 == 0)
    def _():
        softmax_pv(qi, 0, True)

    o = (acc_s[...] * (1.0 / jnp.sum(l_s[...], axis=-1, keepdims=True))).astype(BF16)
    outs = [jnp.dot(o[h * tq:(h + 1) * tq, :], wuv_ref[h], preferred_element_type=F32) for h in range(B_HEADS)]
    o_ref[...] = jnp.concatenate(outs, axis=-1).astype(o_ref.dtype)


def _mla_prompt(qn, qr, kfull, wuk, wuv, tq):
    B, T, _ = qn.shape
    scale = (B_NOPE_DIM + B_ROPE_DIM) ** -0.5 * math.log2(math.e)
    rows = B_HEADS * tq
    tile = lambda c: pl.BlockSpec((None, tq, c), lambda b, i: (b, i, 0))
    stat = pltpu.VMEM((rows, LANES), F32)
    return pl.pallas_call(
        functools.partial(_mla_prompt_kernel, tq=tq, scale=scale, row_block=min(128, tq)),
        grid=(B, T // tq),
        in_specs=[tile(A_WIDTH), tile(2 * LANES), pl.BlockSpec((None, T, KFULL), lambda b, i: (b, 0, 0)),
                  _const_spec((B_HEADS, B_NOPE_DIM, KV_RANK)), _const_spec((B_HEADS, KV_RANK, B_V_DIM))],
        out_specs=tile(B_WIDTH),
        out_shape=jax.ShapeDtypeStruct((B, T, B_WIDTH), BF16),
        scratch_shapes=[pltpu.VMEM((rows, KFULL), BF16), pltpu.VMEM((2, rows, tq), F32),
                        pltpu.VMEM((rows, tq), BF16), stat, stat, stat, pltpu.VMEM((rows, KV_RANK), F32)],
        compiler_params=_cparams(("arbitrary", "arbitrary")),
        name="mla_prompt",
    )(qn, qr, kfull, wuk, wuv)


def _mla_decode_kernel(pt_ref, qn_ref, qr_ref, kfn_ref, wuk_ref, wuv_ref, ckv_hbm, krt_hbm,
                       o_ref, ckv_buf, krt_buf, kv16, s_s, m_s, l_s, acc_s, sem,
                       *, layer, n_pages, T, scale, ck, n_streams):
    i = pl.program_id(0)
    n = pl.num_programs(0)
    L = n_pages * PAGE_SIZE

    def page_copies(sl, p, page):
        rows = pl.ds(pl.multiple_of(p * PAGE_SIZE, PAGE_SIZE), PAGE_SIZE)
        return (pltpu.make_async_copy(ckv_hbm.at[layer, page], ckv_buf.at[sl, rows], sem.at[0, sl]),
                pltpu.make_async_copy(krt_hbm.at[layer, page], krt_buf.at[sl, :, rows], sem.at[1, sl]))

    def issue(bb, sl):
        def body(p, carry):
            for cp in page_copies(sl, p, pt_ref[bb, p]):
                cp.start()
            return carry
        lax.fori_loop(0, n_pages, body, 0, unroll=math.gcd(n_pages, 8))

    def wait(sl):
        for p in range(n_pages):
            for cp in page_copies(sl, p, 0):
                cp.wait()

    def attend(sl):
        qr = qr_ref[sl]
        kfn = kfn_ref[sl]
        parts = _build_queries(qn_ref[sl], qr, wuk_ref, scale, T)
        qfull = jnp.concatenate(parts, axis=0)
        qlat = qfull[:, 0:KV_RANK]
        qrs = (qr.astype(F32) * scale)
        qrope = jnp.concatenate(
            [jnp.concatenate([qrs[:, h * HALF_ROPE:(h + 1) * HALF_ROPE],
                              qrs[:, LANES + h * HALF_ROPE:LANES + (h + 1) * HALF_ROPE]], axis=-1)
             for h in range(B_HEADS)], axis=0).astype(BF16)
        nrows = B_HEADS * T
        per_stream = L // n_streams
        n_ck = per_stream // ck
        m_s[...] = jnp.full(m_s.shape, -jnp.inf, F32)
        l_s[...] = jnp.zeros(l_s.shape, F32)
        acc_s[...] = jnp.zeros(acc_s.shape, F32)

        def scores(st, kc, slot):
            keys = slice(st * per_stream + kc * ck, st * per_stream + (kc + 1) * ck)
            kv16[st, slot] = ckv_buf[sl, keys, :].astype(BF16)
            s_s[st, slot] = (lax.dot_general(qlat, kv16[st, slot], _NT, preferred_element_type=F32)
                             + jnp.dot(qrope, krt_buf[sl, :, keys].astype(BF16), preferred_element_type=F32))

        def softmax_pv(st, slot):
            s = s_s[st, slot]
            tiles = [s[:, c * LANES:(c + 1) * LANES] for c in range(ck // LANES)]
            m_prev = m_s[st]
            m_new = jnp.maximum(m_prev, jnp.max(functools.reduce(jnp.maximum, tiles), axis=-1, keepdims=True))
            alpha = jnp.exp(m_prev - m_new)
            ps = [jnp.exp(t - m_new) for t in tiles]
            l_s[st] = alpha * l_s[st] + functools.reduce(jnp.add, ps)
            m_s[st] = m_new
            pv = jnp.dot(jnp.concatenate(ps, axis=-1).astype(BF16), kv16[st, slot], preferred_element_type=F32)
            acc_s[st] = acc_s[st] * jnp.concatenate([alpha] * (KV_RANK // LANES), axis=-1) + pv

        streams = range(n_streams)
        for st in streams:
            scores(st, 0, 0)
        for kc in range(n_ck):
            if kc + 1 < n_ck:
                for st in streams:
                    scores(st, kc + 1, kc + 1)
            for st in streams:
                softmax_pv(st, kc)
        s_new = lax.dot_general(qfull, kfn, _NT, preferred_element_type=F32)
        qt = _mod_pow2(lax.broadcasted_iota(jnp.int32, (nrows, T), 0), T)
        kt = lax.broadcasted_iota(jnp.int32, (nrows, T), 1)
        s_new = jnp.where(kt <= qt, s_new, -jnp.inf)
        m_parts = [m_s[st][:, 0:1] for st in streams] + [jnp.max(s_new, axis=-1, keepdims=True)]
        m = functools.reduce(jnp.maximum, m_parts)
        p_new = jnp.exp(s_new - m)
        l = jnp.sum(p_new, axis=-1, keepdims=True)
        o = jnp.dot(p_new.astype(BF16), kfn[:, 0:KV_RANK], preferred_element_type=F32)
        for st in streams:
            w_st = jnp.exp(m_parts[st] - m)
            l = l + w_st * jnp.sum(l_s[st], axis=-1, keepdims=True)
            o = o + w_st * acc_s[st]
        o = (o / l).astype(BF16)
        outs = [jnp.dot(o[h * T:(h + 1) * T, :], wuv_ref[h], preferred_element_type=F32) for h in range(B_HEADS)]
        o_ref[sl] = jnp.concatenate(outs, axis=-1).astype(o_ref.dtype)

    @pl.when(i == 0)
    def _():
        issue(0, 0)

    issue(2 * i + 1, 1)
    wait(0)
    attend(0)

    @pl.when(i + 1 < n)
    def _():
        issue(2 * i + 2, 0)

    wait(1)
    attend(1)


def _mla_decode(page_table, qn, qr, kfn, wuk, wuv, cache_kv, cache_krt, layer):
    B, T, _ = qn.shape
    assert B % 2 == 0
    n_pages = page_table.shape[1]
    L = n_pages * PAGE_SIZE
    scale = (B_NOPE_DIM + B_ROPE_DIM) ** -0.5
    n_streams = math.gcd(n_pages // 2, 4)
    ck = math.gcd(L // (2 * n_streams), 4 * PAGE_SIZE)
    nrows = B_HEADS * T
    stat = pltpu.VMEM((n_streams, nrows, LANES), F32)
    tile = lambda c: pl.BlockSpec((2, T, c), lambda b, pt: (b, 0, 0))
    cst = lambda shape: pl.BlockSpec(shape, lambda b, pt: (0,) * len(shape))
    grid_spec = pltpu.PrefetchScalarGridSpec(
        num_scalar_prefetch=1,
        grid=(B // 2,),
        in_specs=[tile(A_WIDTH), tile(2 * LANES), tile(KFULL),
                  cst((B_HEADS, B_NOPE_DIM, KV_RANK)), cst((B_HEADS, KV_RANK, B_V_DIM)),
                  pl.BlockSpec(memory_space=pl.ANY), pl.BlockSpec(memory_space=pl.ANY)],
        out_specs=tile(B_WIDTH),
        scratch_shapes=[pltpu.VMEM((2, L, KV_RANK), F32), pltpu.VMEM((2, B_ROPE_DIM, L), F32),
                        pltpu.VMEM((n_streams, L // (n_streams * ck), ck, KV_RANK), BF16),
                        pltpu.VMEM((n_streams, L // (n_streams * ck), nrows, ck), F32),
                        stat, stat, pltpu.VMEM((n_streams, nrows, KV_RANK), F32), pltpu.SemaphoreType.DMA((2, 2))],
    )
    return pl.pallas_call(
        functools.partial(_mla_decode_kernel, layer=layer, n_pages=n_pages, T=T, scale=scale, ck=ck,
                          n_streams=n_streams),
        grid_spec=grid_spec,
        out_shape=jax.ShapeDtypeStruct((B, T, B_WIDTH), BF16),
        compiler_params=_cparams(("arbitrary",)),
        name="mla_decode",
    )(page_table, qn, qr, kfn, wuk, wuv, cache_kv, cache_krt)


def _pool_kernel(pc_ref, st_ref, wbd_ref, scale_ref, y_ref, newst_ref, e_s, s2_s, s4_s, s8_s, *, ts, T, pos0):
    L = T * ts
    P = (POOL_PREFIX + 1) * ts
    pc = pc_ref[...].reshape(L, C_WIDTH)
    e_s[P - POOL_PREFIX * ts:P, :] = st_ref[...].reshape(POOL_PREFIX * ts, C_WIDTH)
    e_s[P:P + L, :] = pc
    newst_ref[...] = e_s[P + L - POOL_PREFIX * ts:P + L, :].reshape(newst_ref.shape)
    s2_s[P - 14 * ts:P + L, :] = e_s[P - 14 * ts:P + L, :] + e_s[P - 15 * ts:P + L - ts, :]
    s4_s[P - 12 * ts:P + L, :] = s2_s[P - 12 * ts:P + L, :] + s2_s[P - 14 * ts:P + L - 2 * ts, :]
    s8_s[P - 8 * ts:P + L, :] = s4_s[P - 8 * ts:P + L, :] + s4_s[P - 12 * ts:P + L - 4 * ts, :]
    s16 = s8_s[P:P + L, :] + s8_s[P - 8 * ts:P + L - 8 * ts, :]
    lane = lax.broadcasted_iota(jnp.int32, (L, C_WIDTH), 1)
    grp = _div_pow2(lane, C_GROUP_DIM)
    win = jnp.where(grp == 0, s2_s[P:P + L, :],
                    jnp.where(grp == 1, s4_s[P:P + L, :], jnp.where(grp == 2, s8_s[P:P + L, :], s16)))
    wsize = jnp.where(grp == 0, 2, jnp.where(grp == 1, 4, jnp.where(grp == 2, 8, 16)))
    pos = pos0 + _div_pow2(lax.broadcasted_iota(jnp.int32, (L, C_WIDTH), 0), ts)
    cnt = jnp.minimum(pos + 1, wsize).astype(F32)
    d = win / cnt - pc
    y = jnp.dot(d.astype(BF16), wbd_ref[...], preferred_element_type=F32) * scale_ref[...]
    y_ref[...] = y.astype(y_ref.dtype).reshape(y_ref.shape)


def _pool(tok, pc, pool_prev, wbd, pscale, pos0):
    ts, T = tok.ts, tok.T
    rows = (POOL_PREFIX + 1 + T) * ts
    if tok.time_major:
        grid, blk = (tok.B // tok.bb, 1), tok.spec(C_WIDTH)
    else:
        grid, blk = (tok.B, 1), pl.BlockSpec((None, T, C_WIDTH), lambda i, j: (i, 0, 0))
    return pl.pallas_call(
        functools.partial(_pool_kernel, ts=ts, T=T, pos0=pos0),
        grid=grid,
        in_specs=[blk, tok.state_spec(POOL_PREFIX, C_WIDTH), _const_spec((C_WIDTH, C_WIDTH)),
                  _const_spec((1, C_WIDTH))],
        out_specs=[blk, tok.state_spec(POOL_PREFIX, C_WIDTH)],
        out_shape=[jax.ShapeDtypeStruct(tok.shape(C_WIDTH), BF16),
                   jax.ShapeDtypeStruct(tok.state_shape(POOL_PREFIX, C_WIDTH), F32)],
        scratch_shapes=[pltpu.VMEM((rows, C_WIDTH), F32)] * 4,
        compiler_params=_cparams(("arbitrary", "arbitrary")),
        name="pool",
    )(pc, pool_prev, wbd, pscale)


def _ffn_kernel(x_ref, ya_ref, g_ref, yb_ref, yc_ref, cst_ref, wout_ref, gffn_ref, wup_ref, cw_ref, cb_ref,
                wdn_ref, gfin_ref, xo_ref, cnew_ref, gbuf_s, *, tm, ts, final):
    taps = CONV_WIDTH - 1
    pad = max(SUBLANES, taps * ts)
    it = pl.program_id(1)

    @pl.when(it == 0)
    def _():
        gbuf_s[pad - taps * ts:pad, :] = cst_ref[...].reshape(taps * ts, D_FF)

    x = x_ref[...].reshape(tm, D_MODEL)
    ya = (ya_ref[...].reshape(tm, A_WIDTH) * g_ref[...].reshape(tm, A_WIDTH)).astype(BF16)
    ycat = jnp.concatenate([ya, yb_ref[...].reshape(tm, B_WIDTH), yc_ref[...].reshape(tm, C_WIDTH)], axis=-1)
    x1 = x + jnp.dot(ycat, wout_ref[...], preferred_element_type=F32)
    xn = _rms(x1, gffn_ref[...]).astype(BF16)
    acc = jnp.zeros((tm, D_MODEL), F32)
    assert sum(FF_CHUNKS) == D_FF
    lo = 0
    for width in FF_CHUNKS:
        hi = lo + width
        up = jnp.dot(xn, wup_ref[:, lo:hi], preferred_element_type=F32)
        gate = jnp.dot(xn, wup_ref[:, D_FF + lo:D_FF + hi], preferred_element_type=F32)
        gbuf_s[pad:pad + tm, lo:hi] = gate
        gc = cb_ref[:, lo:hi] + cw_ref[2:3, lo:hi] * gate
        for j in range(taps):
            off = pad - (taps - j) * ts
            gc = gc + cw_ref[j:j + 1, lo:hi] * gbuf_s[off:off + tm, lo:hi]
        hh = (gc * _sigmoid(gc) * up).astype(BF16)
        acc = acc + jnp.dot(hh, wdn_ref[lo:hi, :], preferred_element_type=F32)
        lo = hi
    last = gbuf_s[pad + tm - taps * ts:pad + tm, :]
    cnew_ref[...] = last.reshape(cnew_ref.shape)
    gbuf_s[pad - taps * ts:pad, :] = last
    x2 = x1 + acc
    if final:
        x2 = _rms(x2, gfin_ref[...])
    xo_ref[...] = x2.reshape(xo_ref.shape)


def _ffn(tok, x, ya, g, yb, yc, conv_prev, p, gfin, final):
    tm, ts = tok.tm, tok.ts
    taps = CONV_WIDTH - 1
    pad = max(SUBLANES, taps * ts)
    return pl.pallas_call(
        functools.partial(_ffn_kernel, tm=tm, ts=ts, final=final),
        grid=tok.grid,
        in_specs=[tok.spec(D_MODEL), tok.spec(A_WIDTH), tok.spec(A_WIDTH), tok.spec(B_WIDTH), tok.spec(C_WIDTH),
                  tok.state_spec(taps, D_FF), _const_spec((D_MODEL, D_MODEL)), _const_spec((1, D_MODEL)),
                  _const_spec((D_MODEL, 2 * D_FF)), _const_spec((CONV_WIDTH, D_FF)), _const_spec((1, D_FF)),
                  _const_spec((D_FF, D_MODEL)), _const_spec((1, D_MODEL))],
        out_specs=[tok.spec(D_MODEL), tok.state_spec(taps, D_FF)],
        out_shape=[jax.ShapeDtypeStruct(tok.shape(D_MODEL), F32),
                   jax.ShapeDtypeStruct(tok.state_shape(taps, D_FF), F32)],
        scratch_shapes=[pltpu.VMEM((pad + tm, D_FF), F32)],
        compiler_params=_cparams(("arbitrary", "arbitrary")),
        name="outproj_ffn",
    )(x, ya, g, yb, yc, conv_prev, p["w_out"], p["g_ffn"], p["w_up"], p["conv_w"], p["conv_b"], p["w_down"], gfin)


def _ext_columns():
    qn0 = A_COLS
    qr0 = qn0 + B_HEADS * B_NOPE_DIM
    ckv0 = qr0 + B_HEADS * B_ROPE_DIM
    kr0 = ckv0 + KV_RANK
    pc0 = A_COLS + B_COLS
    cols = list(range(A_COLS)) + list(range(qn0, qr0)) + list(range(ckv0, kr0)) + list(range(pc0, pc0 + C_WIDTH))
    for half in range(2):
        blk = []
        for h in range(B_HEADS):
            blk += [qr0 + h * B_ROPE_DIM + half * HALF_ROPE + i for i in range(HALF_ROPE)]
        blk += [kr0 + half * HALF_ROPE + i for i in range(HALF_ROPE)]
        cols += blk + [-1] * (LANES - len(blk))
    for half in range(2):
        blk = [kr0 + half * HALF_ROPE + i for _ in range(B_HEADS) for i in range(HALF_ROPE)]
        cols += blk + [-1] * (LANES - len(blk))
    assert len(cols) == IN_EXT
    return np.asarray(cols, np.int32)


def _layer_params(l, norm_mix_g, w_in, mu_shift, decay_w0, decay_w2, iclr_a0, iclr_a2, gate_g2, k_k, k_a, r_k,
                  lnx_w, lnx_b, kv_norm_g, w_uk, w_uv, pool_w, pool_scale, w_out, norm_ffn_g, w_up, conv_w,
                  conv_b, w_down):
    cols = _ext_columns()
    w_ext = jnp.where(cols[None, :] >= 0, w_in[l][:, np.maximum(cols, 0)], 0.0).astype(BF16)
    lora_cat = jnp.zeros((LANES, 2 * A_WIDTH), F32)
    lora_cat = lora_cat.at[0:DECAY_LORA, 0:A_WIDTH].set(decay_w2[l])
    lora_cat = lora_cat.at[DECAY_LORA:DECAY_LORA + ICLR_LORA, A_WIDTH:].set(iclr_a2[l])
    head = np.arange(A_WIDTH) // A_HEAD_DIM
    head_ones = jnp.asarray((head[:, None] == head[None, :]).astype(np.float32), BF16)
    wbd = jnp.zeros((C_WIDTH, C_WIDTH), F32)
    for gi in range(C_GROUPS):
        sl = slice(gi * C_GROUP_DIM, (gi + 1) * C_GROUP_DIM)
        wbd = wbd.at[sl, sl].set(pool_w[l, gi])
    row = lambda v: v.reshape(1, -1)
    return dict(
        g_mix=row(norm_mix_g[l]), w_ext=w_ext, kv_g=row(kv_norm_g[l]),
        mu=row(mu_shift[l]), lora_cat=lora_cat.astype(BF16), w0=row(decay_w0[l]), a0=row(iclr_a0[l]),
        g2=gate_g2[l].astype(BF16), k_k=row(k_k[l]), k_a=row(k_a[l]), head_ones=head_ones,
        r_k=r_k[l], lnx_w=lnx_w[l].reshape(A_HEADS, A_HEAD_DIM), lnx_b=lnx_b[l].reshape(A_HEADS, A_HEAD_DIM),
        wuk=jnp.transpose(w_uk[l], (1, 2, 0)).astype(BF16), wuv=jnp.transpose(w_uv[l], (1, 0, 2)).astype(BF16),
        pool_wbd=wbd.astype(BF16), pool_scale=row(pool_scale[l]),
        w_out=w_out[l].astype(BF16), g_ffn=row(norm_ffn_g[l]), w_up=w_up[l].astype(BF16),
        conv_w=conv_w[l], conv_b=row(conv_b[l]), w_down=w_down[l].astype(BF16),
    )


def _rope_tables(pos):
    inv = ROPE_BASE ** (-jnp.arange(0, B_ROPE_DIM, 2, dtype=F32) / B_ROPE_DIM)
    ang = pos.astype(F32)[:, None] * inv[None, :]
    reps = LANES // HALF_ROPE
    return jnp.tile(jnp.cos(ang), (1, reps)), jnp.tile(jnp.sin(ang), (1, reps))


def _to_scan(tok, x):
    assert tok.time_major
    x = x.reshape(tok.T, tok.B, A_HEADS, A_HEAD_DIM)
    return jnp.transpose(x, (2, 0, 3, 1))


def _lane_maps(nb):
    lane_of = np.asarray([[_prompt_lane(b, h, nb) for h in range(A_HEADS)] for b in range(nb)], np.int32)
    lanes_b = np.zeros(nb * A_HEADS, np.int32)
    lanes_h = np.zeros(nb * A_HEADS, np.int32)
    for b in range(nb):
        for h in range(A_HEADS):
            lanes_b[lane_of[b, h]], lanes_h[lane_of[b, h]] = b, h
    return lane_of, lanes_b, lanes_h


def _from_scan(tok, y):
    H, N = A_HEADS, A_HEAD_DIM
    if tok.time_major:
        return jnp.transpose(y, (1, 3, 0, 2)).reshape(tok.T, tok.B, H * N)
    lane_of, _, _ = _lane_maps(tok.B)
    y = y[0][:, :, lane_of]
    return jnp.transpose(y, (2, 0, 3, 1)).reshape(tok.B, tok.T, H * N)


def _pad_lanes(x):
    return jnp.pad(x, [(0, 0)] * (x.ndim - 1) + [(0, LANES - x.shape[-1])])


def _state_to_scan(tok, s):
    if tok.time_major:
        return jnp.transpose(s, (1, 3, 2, 0))
    _, lanes_b, lanes_h = _lane_maps(tok.B)
    return _pad_lanes(jnp.transpose(s, (3, 2, 0, 1))[:, :, lanes_b, lanes_h])[None]


def _state_from_scan(tok, s):
    if tok.time_major:
        return jnp.transpose(s, (3, 0, 2, 1))
    lane_of, _, _ = _lane_maps(tok.B)
    return jnp.transpose(s[0][:, :, lane_of], (2, 3, 1, 0))


def _head_consts(tok, p):
    def lay(v):
        if tok.time_major:
            return jnp.broadcast_to(v[:, :, None], (A_HEADS, A_HEAD_DIM, LANES))
        _, _, lanes_h = _lane_maps(tok.B)
        return _pad_lanes(v.T[:, lanes_h])[None]
    return lay(p["lnx_w"]), lay(p["lnx_b"]), lay(p["r_k"])


def _layer(tok, x, cos, sin, pos0, shift_prev, wkv_prev, pool_prev, conv_prev, p, gfin, final, scan_tc, attend):
    pa, qn, kvlat, pc, qr, kfull, krope = _inproj(tok, x, p["g_mix"], p["w_ext"], p["kv_g"], cos, sin)
    if tok.time_major:
        r, w, k, v, a, b, g, shift_new = _rwkv_prep(tok, pa, shift_prev, p)
        r, w, k, v, a, b = (_to_scan(tok, t) for t in (r, w, k, v, a, b))
    else:
        r, w, k, v, a, b, g, shift_new = _rwkv_prep_scan(pa, shift_prev, p, scan_tc)
    lnw, lnb, rk = _head_consts(tok, p)
    y, s_new = _rwkv_scan(w, a, b, k, v, r, _state_to_scan(tok, wkv_prev), lnw, lnb, rk, scan_tc)
    ya = _from_scan(tok, y)
    yb = attend(qn, qr, kfull)
    yc, pool_new = _pool(tok, pc, pool_prev, p["pool_wbd"], p["pool_scale"], pos0)
    x, conv_new = _ffn(tok, x, ya, g, yb, yc, conv_prev, p, gfin, final)
    return x, (kvlat, krope, _state_from_scan(tok, s_new), shift_new, pool_new, conv_new)


def kernel(x_prompt, x_sample, cache_kv_latent, cache_k_rope, page_table, state_wkv, state_shift, state_pool,
           state_conv, norm_mix_g, w_in, mu_shift, decay_w0, decay_w2, iclr_a0, iclr_a2, gate_g2, k_k, k_a, r_k,
           lnx_w, lnx_b, kv_norm_g, w_uk, w_uv, pool_w, pool_scale, w_out, norm_ffn_g, w_up, conv_w, conv_b,
           w_down, norm_final_g):
    Bp, Tp, _ = x_prompt.shape
    Bs, Ts, _ = x_sample.shape
    past_len = page_table.shape[1] * PAGE_SIZE
    tokp = _Tok(False, Bp, Tp, min(512, Tp))
    toks = _Tok(True, Bs, Ts, min(32, Bs))
    cos_p, sin_p = _rope_tables(jnp.arange(Tp, dtype=jnp.int32))
    pos_s = past_len + jnp.repeat(jnp.arange(Ts, dtype=jnp.int32), toks.bb)
    cos_s, sin_s = _rope_tables(pos_s)
    gfin = norm_final_g.reshape(1, -1)
    tm_first = lambda t: jnp.swapaxes(t, 0, 1)
    cache_krt = jnp.swapaxes(cache_k_rope, 2, 3)

    xp = x_prompt
    xs = tm_first(x_sample)
    zeros = lambda *shape: jnp.zeros(shape, F32)
    outs_p, outs_s = [], []
    for l in range(DEPTH):
        p = _layer_params(l, norm_mix_g, w_in, mu_shift, decay_w0, decay_w2, iclr_a0, iclr_a2, gate_g2, k_k, k_a,
                          r_k, lnx_w, lnx_b, kv_norm_g, w_uk, w_uv, pool_w, pool_scale, w_out, norm_ffn_g, w_up,
                          conv_w, conv_b, w_down)
        final = l == DEPTH - 1

        attend_p = lambda qn, qr, kfull: _mla_prompt(qn, qr, kfull, p["wuk"], p["wuv"], min(256, Tp))
        xp, st_p = _layer(tokp, xp, cos_p, sin_p, 0, zeros(Bp, A_COLS),
                          zeros(Bp, A_HEADS, A_HEAD_DIM, A_HEAD_DIM), zeros(Bp, POOL_PREFIX, C_WIDTH),
                          zeros(Bp, CONV_WIDTH - 1, D_FF), p, gfin, final, min(32, Tp), attend_p)

        def attend_s(qn, qr, kfull):
            yb = _mla_decode(page_table, tm_first(qn), tm_first(qr), tm_first(kfull), p["wuk"], p["wuv"],
                             cache_kv_latent, cache_krt, l)
            return tm_first(yb)
        xs, st_s = _layer(toks, xs, cos_s, sin_s, past_len, state_shift[l][None], state_wkv[l],
                          tm_first(state_pool[l]), tm_first(state_conv[l]), p, gfin, final, Ts, attend_s)
        outs_p.append(st_p)
        outs_s.append(st_s)

    stack = lambda outs, i, f: jnp.stack([f(o[i]) for o in outs], axis=0)
    ident = lambda t: t
    return (xp, tm_first(xs),
            stack(outs_p, 0, ident), stack(outs_p, 1, ident), stack(outs_p, 2, ident),
            stack(outs_p, 3, ident), stack(outs_p, 4, ident), stack(outs_p, 5, ident),
            stack(outs_s, 0, tm_first), stack(outs_s, 1, tm_first), stack(outs_s, 2, ident),
            stack(outs_s, 3, lambda t: t[0]), stack(outs_s, 4, tm_first), stack(outs_s, 5, tm_first))
```

```python
import functools
import math

import jax
import jax.numpy as jnp
import numpy as np
from jax import lax
from jax.experimental import pallas as pl
from jax.experimental.pallas import tpu as pltpu

F32 = jnp.float32
BF16 = jnp.bfloat16

D_MODEL = 1024
DEPTH = 2
PAGE_SIZE = 128
A_HEADS = 6
A_HEAD_DIM = 64
A_WIDTH = A_HEADS * A_HEAD_DIM
DECAY_LORA = 64
ICLR_LORA = 64
GATE_LORA = 128
A_COLS = 3 * A_WIDTH + DECAY_LORA + ICLR_LORA + GATE_LORA
B_HEADS = 6
B_NOPE_DIM = 64
B_ROPE_DIM = 32
B_V_DIM = 64
B_WIDTH = B_HEADS * B_V_DIM
KV_RANK = 256
ROPE_BASE = 10000.0
B_COLS = B_HEADS * (B_NOPE_DIM + B_ROPE_DIM) + KV_RANK + B_ROPE_DIM
C_GROUPS = 4
C_GROUP_DIM = 64
C_WIDTH = C_GROUPS * C_GROUP_DIM
POOL_WINDOWS = (2, 4, 8, 16)
POOL_PREFIX = 15
D_FF = 2816
CONV_WIDTH = 3
NORM_EPS = 1e-6
GN_EPS = A_HEAD_DIM * 1e-5

LANES = 128
SUBLANES = 8
HALF_ROPE = B_ROPE_DIM // 2
IN_EXT = A_COLS + B_HEADS * B_NOPE_DIM + KV_RANK + C_WIDTH + 4 * LANES
OFF_QN = A_COLS
OFF_CKV = OFF_QN + B_HEADS * B_NOPE_DIM
OFF_PC = OFF_CKV + KV_RANK
OFF_ROPE = OFF_PC + C_WIDTH
KFULL = KV_RANK + 2 * LANES
MXU_TILE = 256
FF_CHUNKS = (6 * MXU_TILE, 5 * MXU_TILE)
VMEM_LIMIT = 56 * 1024 * 1024


def _cparams(sem):
    return pltpu.CompilerParams(dimension_semantics=sem, vmem_limit_bytes=VMEM_LIMIT)


def _const_spec(shape):
    nd = len(shape)
    return pl.BlockSpec(shape, lambda *_: (0,) * nd, pipeline_mode=pl.Buffered(1))


class _Tok:
    def __init__(self, time_major, B, T, tile):
        self.time_major, self.B, self.T = time_major, B, T
        if time_major:
            self.bb = tile
            self.tm, self.ts = T * tile, tile
            self.grid = (B // tile, 1)
        else:
            self.tm, self.ts = tile, 1
            self.grid = (B, T // tile)
        self.n = B * T

    def shape(self, C):
        return (self.T, self.B, C) if self.time_major else (self.B, self.T, C)

    def spec(self, C):
        if self.time_major:
            return pl.BlockSpec((self.T, self.bb, C), lambda i, j: (0, i, 0))
        return pl.BlockSpec((None, self.tm, C), lambda i, j: (i, j, 0))

    def state_shape(self, steps, C):
        return (steps, self.B, C) if self.time_major else (self.B, steps, C)

    def state_spec(self, steps, C):
        if self.time_major:
            return pl.BlockSpec((steps, self.bb, C), lambda i, j: (0, i, 0))
        return pl.BlockSpec((None, steps, C), lambda i, j: (i, 0, 0))


def _rms(x, g):
    return x * lax.rsqrt(jnp.mean(x * x, axis=-1, keepdims=True) + NORM_EPS) * g


def _sigmoid(x):
    return 1.0 / (1.0 + jnp.exp(-x))


def _div_pow2(x, d):
    assert d & (d - 1) == 0
    return lax.shift_right_logical(x, int(math.log2(d)))


def _mod_pow2(x, d):
    assert d & (d - 1) == 0
    return lax.bitwise_and(x, d - 1)


def _inproj_kernel(x_ref, g_ref, w_ref, kvg_ref, cos_ref, sin_ref,
                   pa_ref, qn_ref, kvlat_ref, pc_ref, qr_ref, kfull_ref, krope_ref, *, tm):
    x = x_ref[...].reshape(tm, D_MODEL)
    xb = _rms(x, g_ref[...]).astype(BF16)

    def proj(lo, hi):
        return jnp.dot(xb, w_ref[:, lo:hi], preferred_element_type=F32)

    assert OFF_CKV % MXU_TILE == 0 and OFF_ROPE % MXU_TILE == 0 and IN_EXT % MXU_TILE == 0
    head = proj(0, OFF_CKV)
    pa_ref[...] = head[:, 0:A_COLS].reshape(pa_ref.shape)
    qn_ref[...] = head[:, OFF_QN:OFF_CKV].astype(BF16).reshape(qn_ref.shape)
    mid = proj(OFF_CKV, OFF_ROPE)
    ckv = _rms(mid[:, 0:KV_RANK], kvg_ref[...])
    kvlat_ref[...] = ckv.reshape(kvlat_ref.shape)
    pc_ref[...] = mid[:, KV_RANK:KV_RANK + C_WIDTH].reshape(pc_ref.shape)
    rr = proj(OFF_ROPE, IN_EXT)
    cos = cos_ref[...]
    sin = sin_ref[...]
    q1, q2 = rr[:, 0:LANES], rr[:, LANES:2 * LANES]
    k1, k2 = rr[:, 2 * LANES:3 * LANES], rr[:, 3 * LANES:4 * LANES]
    o1 = q1 * cos - q2 * sin
    o2 = q1 * sin + q2 * cos
    qr = jnp.concatenate([o1, o2], axis=-1).astype(BF16)
    qr_ref[...] = qr.reshape(qr_ref.shape)
    kf = jnp.concatenate([ckv, k1 * cos - k2 * sin, k1 * sin + k2 * cos], axis=-1).astype(BF16)
    kfull_ref[...] = kf.reshape(kfull_ref.shape)
    kr_lo = B_HEADS * HALF_ROPE
    krope = jnp.concatenate([o1[:, kr_lo:kr_lo + HALF_ROPE], o2[:, kr_lo:kr_lo + HALF_ROPE]], axis=-1)
    krope_ref[...] = krope.reshape(krope_ref.shape)


def _inproj(tok, x, g, w_ext, kvg, cos, sin):
    tm = tok.tm
    n_tab = cos.shape[0] // tm
    tab_spec = pl.BlockSpec((tm, LANES), lambda i, j: (j % n_tab, 0))
    widths = (A_COLS, A_WIDTH, KV_RANK, C_WIDTH, 2 * LANES, KFULL, B_ROPE_DIM)
    dtypes = (F32, BF16, F32, F32, BF16, BF16, F32)
    return pl.pallas_call(
        functools.partial(_inproj_kernel, tm=tm),
        grid=tok.grid,
        in_specs=[tok.spec(D_MODEL), _const_spec((1, D_MODEL)), _const_spec((D_MODEL, IN_EXT)),
                  _const_spec((1, KV_RANK)), tab_spec, tab_spec],
        out_specs=[tok.spec(c) for c in widths],
        out_shape=[jax.ShapeDtypeStruct(tok.shape(c), d) for c, d in zip(widths, dtypes)],
        compiler_params=_cparams(("arbitrary", "arbitrary")),
        name="inproj",
    )(x, g, w_ext, kvg, cos, sin)


def _rwkv_token_math(pa, prev, mu_ref, wcat_ref, w0_ref, a0_ref, g2_ref, kk_ref, ka_ref, ones_ref):
    s = pa + (prev - pa) * mu_ref[...]
    W = A_WIDTH
    r, k, v = s[:, 0:W], s[:, W:2 * W], s[:, 2 * W:3 * W]
    lo_in = s[:, 3 * W:3 * W + LANES]
    gd = s[:, 3 * W + LANES:3 * W + 2 * LANES]
    lane = lax.broadcasted_iota(jnp.int32, lo_in.shape, 1)
    lo_in = jnp.where(lane < DECAY_LORA, jnp.tanh(lo_in), lo_in)
    lo = jnp.dot(lo_in.astype(BF16), wcat_ref[...], preferred_element_type=F32)
    z = -(w0_ref[...] + lo[:, 0:W])
    softplus = jnp.maximum(z, 0.0) + jnp.log(1.0 + jnp.exp(-jnp.abs(z)))
    decay = jnp.exp(-jnp.exp(-softplus - 0.5))
    a = _sigmoid(a0_ref[...] + lo[:, W:2 * W])
    g = jnp.dot(_sigmoid(gd).astype(BF16), g2_ref[...], preferred_element_type=F32)
    kk = k * kk_ref[...]
    kk2 = kk * kk
    hi = kk2.astype(BF16)
    lo2 = (kk2 - hi.astype(F32)).astype(BF16)
    ones = ones_ref[...]
    ss = (jnp.dot(hi, ones, preferred_element_type=F32) + jnp.dot(lo2, ones, preferred_element_type=F32))
    kk = kk * lax.rsqrt(jnp.maximum(ss, 1e-24))
    kmod = k * (1.0 + (a - 1.0) * ka_ref[...])
    return r, decay, kmod, v, -kk, kk * a, g


def _rwkv_prep_kernel(pa_ref, st_ref, mu_ref, wcat_ref, w0_ref, a0_ref, g2_ref, kk_ref, ka_ref, ones_ref,
                      r_ref, w_ref, k_ref, v_ref, a_ref, b_ref, g_ref, newst_ref, buf_ref, *, tm, ts):
    pad = max(SUBLANES, ts)
    it = pl.program_id(1)

    @pl.when(it == 0)
    def _():
        buf_ref[pad - ts:pad, :] = st_ref[...].reshape(ts, A_COLS)

    pa = pa_ref[...].reshape(tm, A_COLS)
    buf_ref[pad:pad + tm, :] = pa
    prev = buf_ref[pad - ts:pad - ts + tm, :]
    last = pa[tm - ts:tm, :]
    buf_ref[pad - ts:pad, :] = last
    newst_ref[...] = last.reshape(newst_ref.shape)
    vals = _rwkv_token_math(pa, prev, mu_ref, wcat_ref, w0_ref, a0_ref, g2_ref, kk_ref, ka_ref, ones_ref)
    for ref, val in zip((r_ref, w_ref, k_ref, v_ref, a_ref, b_ref, g_ref), vals):
        ref[...] = val.reshape(ref.shape)


def _rwkv_prep_scan_kernel(pa_ref, st_ref, mu_ref, wcat_ref, w0_ref, a0_ref, g2_ref, kk_ref, ka_ref, ones_ref,
                           r_ref, w_ref, k_ref, v_ref, a_ref, b_ref, g_ref, newst_ref, buf_ref, q_ref, *, nb, tt):
    it = pl.program_id(0)

    @pl.when(it == 0)
    def _():
        buf_ref[0:nb, :] = st_ref[...]

    for t in range(tt):
        buf_ref[(t + 1) * nb:(t + 2) * nb, :] = pa_ref[:, t, :]
    pa = buf_ref[nb:(tt + 1) * nb, :]
    prev = buf_ref[0:tt * nb, :]
    vals = _rwkv_token_math(pa, prev, mu_ref, wcat_ref, w0_ref, a0_ref, g2_ref, kk_ref, ka_ref, ones_ref)
    last = buf_ref[tt * nb:(tt + 1) * nb, :]
    buf_ref[0:nb, :] = last
    newst_ref[...] = last
    g = vals[6]
    for t in range(tt):
        g_ref[:, t, :] = g[t * nb:(t + 1) * nb, :]
    for qi in range(6):
        q_ref[qi] = vals[qi]
    outs = (r_ref, w_ref, k_ref, v_ref, a_ref, b_ref)
    half = A_HEADS // 2
    win = half * nb
    N = A_HEAD_DIM

    def pair(pi, carry):
        for qi, o_ref in enumerate(outs):
            rows = q_ref[qi, pl.ds(pl.multiple_of(pi * 2 * nb, 2 * nb), 2 * nb), :]
            pieces = [rows[s * nb:(s + 1) * nb, hp * LANES:(hp + 1) * LANES] for s in range(2) for hp in range(half)]
            m = jnp.concatenate(pieces + [jnp.zeros((LANES - 2 * win, LANES), F32)], axis=0)
            mt = m.T
            unused = jnp.full((N, LANES - 2 * win), 1.0 if o_ref is w_ref else 0.0, F32)
            for s in range(2):
                tile = jnp.concatenate([mt[0:N, s * win:(s + 1) * win], mt[N:2 * N, s * win:(s + 1) * win], unused],
                                       axis=1)
                o_ref[0, pi * 2 + s] = tile
        return carry

    lax.fori_loop(0, tt // 2, pair, 0)


def _rwkv_prep_scan(pa, shift_prev, p, tt):
    nb, T, _ = pa.shape
    assert 2 * (A_HEADS // 2) * nb <= LANES and tt % 2 == 0
    vec = _const_spec((1, A_WIDTH))
    seq = pl.BlockSpec((1, tt, A_HEAD_DIM, LANES), lambda i: (0, i, 0, 0))
    st = pl.BlockSpec((nb, A_COLS), lambda i: (0, 0))
    return pl.pallas_call(
        functools.partial(_rwkv_prep_scan_kernel, nb=nb, tt=tt),
        grid=(T // tt,),
        in_specs=[pl.BlockSpec((nb, tt, A_COLS), lambda i: (0, i, 0)), st, _const_spec((1, A_COLS)),
                  _const_spec((LANES, 2 * A_WIDTH)), vec, vec, _const_spec((GATE_LORA, A_WIDTH)), vec, vec,
                  _const_spec((A_WIDTH, A_WIDTH))],
        out_specs=[seq] * 6 + [pl.BlockSpec((nb, tt, A_WIDTH), lambda i: (0, i, 0)), st],
        out_shape=[jax.ShapeDtypeStruct((1, T, A_HEAD_DIM, LANES), F32)] * 6
        + [jax.ShapeDtypeStruct((nb, T, A_WIDTH), F32), jax.ShapeDtypeStruct((nb, A_COLS), F32)],
        scratch_shapes=[pltpu.VMEM(((tt + 1) * nb, A_COLS), F32), pltpu.VMEM((6, tt * nb, A_WIDTH), F32)],
        compiler_params=_cparams(("arbitrary",)),
        name="rwkv_prep_scan",
    )(pa, shift_prev, p["mu"], p["lora_cat"], p["w0"], p["a0"], p["g2"], p["k_k"], p["k_a"], p["head_ones"])


def _rwkv_prep(tok, pa, shift_prev, p):
    tm, ts = tok.tm, tok.ts
    pad = max(SUBLANES, ts)
    vec = _const_spec((1, A_WIDTH))
    return pl.pallas_call(
        functools.partial(_rwkv_prep_kernel, tm=tm, ts=ts),
        grid=tok.grid,
        in_specs=[tok.spec(A_COLS), tok.state_spec(1, A_COLS), _const_spec((1, A_COLS)),
                  _const_spec((LANES, 2 * A_WIDTH)), vec, vec, _const_spec((GATE_LORA, A_WIDTH)), vec, vec,
                  _const_spec((A_WIDTH, A_WIDTH))],
        out_specs=[tok.spec(A_WIDTH)] * 7 + [tok.state_spec(1, A_COLS)],
        out_shape=[jax.ShapeDtypeStruct(tok.shape(A_WIDTH), F32)] * 7
        + [jax.ShapeDtypeStruct(tok.state_shape(1, A_COLS), F32)],
        scratch_shapes=[pltpu.VMEM((pad + tm, A_COLS), F32)],
        compiler_params=_cparams(("arbitrary", "arbitrary")),
        name="rwkv_prep",
    )(pa, shift_prev, p["mu"], p["lora_cat"], p["w0"], p["a0"], p["g2"], p["k_k"], p["k_a"], p["head_ones"])


def _rwkv_scan_kernel(w_ref, a_ref, b_ref, k_ref, v_ref, r_ref, s0_ref, lnw_ref, lnb_ref, rk_ref,
                      y_ref, sT_ref, S_ref, op_ref, *, tc):
    N = A_HEAD_DIM
    it = pl.program_id(1)

    @pl.when(it == 0)
    def _():
        S_ref[...] = s0_ref[0]

    def step(t, p_prev):
        w = w_ref[0, t]
        r = r_ref[0, t]
        k = k_ref[0, t]
        v = v_ref[0, t]
        p_new = p_prev * w
        inv = 1.0 / p_new
        rt = r * p_new
        bt = b_ref[0, t] * inv
        kt = k * inv
        op_ref[0] = a_ref[0, t] * p_prev
        op_ref[1] = rt
        op_ref[2] = bt
        op_ref[3] = kt
        sa = [jnp.zeros((N, LANES), F32) for _ in range(2)]
        yp = [jnp.zeros((N, LANES), F32) for _ in range(2)]
        for j in range(N):
            Zj = S_ref[j]
            sa[j % 2] = sa[j % 2] + Zj * op_ref[0, pl.ds(j, 1), :]
            yp[j % 2] = yp[j % 2] + Zj * op_ref[1, pl.ds(j, 1), :]
        sa = sa[0] + sa[1]
        yp = yp[0] + yp[1]
        for j in range(N):
            S_ref[j] = S_ref[j] + sa * op_ref[2, pl.ds(j, 1), :] + v * op_ref[3, pl.ds(j, 1), :]
        br = jnp.sum(bt * rt, axis=0, keepdims=True)
        kr = jnp.sum(kt * rt, axis=0, keepdims=True)
        y = yp + sa * br + v * kr
        mu = jnp.mean(y, axis=0, keepdims=True)
        d = y - mu
        var = jnp.mean(d * d, axis=0, keepdims=True)
        yn = d * lax.rsqrt(var + GN_EPS) * lnw_ref[0] + lnb_ref[0]
        bonus = jnp.sum(r * k * rk_ref[0], axis=0, keepdims=True)
        y_ref[0, t] = yn + bonus * v
        return p_new

    op_ref[0] = lax.fori_loop(0, tc, step, jnp.ones((N, LANES), F32))
    for j in range(N):
        S_ref[j] = S_ref[j] * op_ref[0, pl.ds(j, 1), :]

    @pl.when(it == pl.num_programs(1) - 1)
    def _():
        sT_ref[0] = S_ref[...]


def _rwkv_scan(w, a, b, k, v, r, s0, lnw, lnb, rk, tc):
    G, T, N, _ = w.shape
    seq = pl.BlockSpec((1, tc, N, LANES), lambda g, i: (g, i, 0, 0))
    st = pl.BlockSpec((1, N, N, LANES), lambda g, i: (g, 0, 0, 0))
    cst = pl.BlockSpec((1, N, LANES), lambda g, i: (g, 0, 0))
    return pl.pallas_call(
        functools.partial(_rwkv_scan_kernel, tc=tc),
        grid=(G, T // tc),
        in_specs=[seq] * 6 + [st, cst, cst, cst],
        out_specs=[seq, st],
        out_shape=[jax.ShapeDtypeStruct((G, T, N, LANES), F32), jax.ShapeDtypeStruct((G, N, N, LANES), F32)],
        scratch_shapes=[pltpu.VMEM((N, N, LANES), F32), pltpu.VMEM((4, N, LANES), F32)],
        compiler_params=_cparams(("arbitrary", "arbitrary")),
        name="rwkv_scan",
    )(w, a, b, k, v, r, s0, lnw, lnb, rk)


_NT = (((1,), (1,)), ((), ()))


def _build_queries(qn, qr, wuk_ref, scale, rows):
    lane = lax.broadcasted_iota(jnp.int32, (rows, 2 * LANES), 1)
    slot = _div_pow2(_mod_pow2(lane, LANES), HALF_ROPE)
    qrf = qr.astype(F32) * scale
    parts = []
    for h in range(B_HEADS):
        ql = jnp.dot(qn[:, h * B_NOPE_DIM:(h + 1) * B_NOPE_DIM], wuk_ref[h], preferred_element_type=F32) * scale
        parts.append(jnp.concatenate([ql, jnp.where(slot == h, qrf, 0.0)], axis=-1).astype(BF16))
    return parts


def _mla_prompt_kernel(qn_ref, qr_ref, kf_ref, wuk_ref, wuv_ref, o_ref, q_s, s_s, p_s, m_s, l_s, a_s, acc_s,
                       *, tq, scale, row_block):
    qi = pl.program_id(1)
    parts = _build_queries(qn_ref[...], qr_ref[...], wuk_ref, scale, tq)
    for h in range(B_HEADS):
        q_s[h * tq:(h + 1) * tq, :] = parts[h]
    rows = B_HEADS * tq
    m_s[...] = jnp.full((rows, LANES), -jnp.inf, F32)
    l_s[...] = jnp.zeros((rows, LANES), F32)
    acc_s[...] = jnp.zeros((rows, KV_RANK), F32)

    def keys(kc):
        return pl.ds(pl.multiple_of(kc * tq, tq), tq)

    def scores(kc, slot):
        s_s[slot] = lax.dot_general(q_s[...], kf_ref[keys(kc), :], _NT, preferred_element_type=F32)

    def lane_tiles(s):
        return [s[:, c * LANES:(c + 1) * LANES] for c in range(tq // LANES)]

    def softmax_pv(kc, slot, masked):
        blocks = [slice(rb * row_block, (rb + 1) * row_block) for rb in range(rows // row_block)]
        for rb, blk in enumerate(blocks):
            s = s_s[slot, blk, :]
            if masked:
                qt = _mod_pow2(rb * row_block + lax.broadcasted_iota(jnp.int32, (row_block, tq), 0), tq)
                kpos = lax.broadcasted_iota(jnp.int32, (row_block, tq), 1)
                s = jnp.where(kpos <= qt, s, -jnp.inf)
                s_s[slot, blk, :] = s
            mx = functools.reduce(jnp.maximum, lane_tiles(s))
            m_prev = m_s[blk, :]
            m_new = jnp.maximum(m_prev, jnp.max(mx, axis=-1, keepdims=True))
            a_s[blk, :] = jnp.exp2(m_prev - m_new)
            m_s[blk, :] = m_new
        for blk in blocks:
            m_new = m_s[blk, :]
            ps = [jnp.exp2(t - m_new) for t in lane_tiles(s_s[slot, blk, :])]
            l_s[blk, :] = a_s[blk, :] * l_s[blk, :] + functools.reduce(jnp.add, ps)
            p_s[blk, :] = jnp.concatenate(ps, axis=-1).astype(BF16)
        pv = jnp.dot(p_s[...], kf_ref[keys(kc), 0:KV_RANK], preferred_element_type=F32)
        alpha = a_s[...]
        acc_s[...] = acc_s[...] * jnp.concatenate([alpha] * (KV_RANK // LANES), axis=-1) + pv

    scores(0, 0)

    def pair(i, carry):
        scores(2 * i + 1, 1)
        softmax_pv(2 * i, 0, False)
        scores(2 * i + 2, 0)
        softmax_pv(2 * i + 1, 1, False)
        return carry

    lax.fori_loop(0, qi // 2, pair, 0)

    @pl.when(qi % 2 == 1)
    def _():
        scores(qi, 1)
        softmax_pv(qi - 1, 0, False)
        softmax_pv(qi, 1, True)

    @pl.when(qi % 2 == 0)
    def _():
        softmax_pv(qi, 0, True)

    o = (acc_s[...] * (1.0 / jnp.sum(l_s[...], axis=-1, keepdims=True))).astype(BF16)
    outs = [jnp.dot(o[h * tq:(h + 1) * tq, :], wuv_ref[h], preferred_element_type=F32) for h in range(B_HEADS)]
    o_ref[...] = jnp.concatenate(outs, axis=-1).astype(o_ref.dtype)


def _mla_prompt(qn, qr, kfull, wuk, wuv, tq):
    B, T, _ = qn.shape
    scale = (B_NOPE_DIM + B_ROPE_DIM) ** -0.5 * math.log2(math.e)
    rows = B_HEADS * tq
    tile = lambda c: pl.BlockSpec((None, tq, c), lambda b, i: (b, i, 0))
    stat = pltpu.VMEM((rows, LANES), F32)
    return pl.pallas_call(
        functools.partial(_mla_prompt_kernel, tq=tq, scale=scale, row_block=min(128, tq)),
        grid=(B, T // tq),
        in_specs=[tile(A_WIDTH), tile(2 * LANES), pl.BlockSpec((None, T, KFULL), lambda b, i: (b, 0, 0)),
                  _const_spec((B_HEADS, B_NOPE_DIM, KV_RANK)), _const_spec((B_HEADS, KV_RANK, B_V_DIM))],
        out_specs=tile(B_WIDTH),
        out_shape=jax.ShapeDtypeStruct((B, T, B_WIDTH), BF16),
        scratch_shapes=[pltpu.VMEM((rows, KFULL), BF16), pltpu.VMEM((2, rows, tq), F32),
                        pltpu.VMEM((rows, tq), BF16), stat, stat, stat, pltpu.VMEM((rows, KV_RANK), F32)],
        compiler_params=_cparams(("arbitrary", "arbitrary")),
        name="mla_prompt",
    )(qn, qr, kfull, wuk, wuv)


def _mla_decode_kernel(pt_ref, qn_ref, qr_ref, kfn_ref, wuk_ref, wuv_ref, ckv_hbm, krt_hbm,
                       o_ref, ckv_buf, krt_buf, kv16, s_s, m_s, l_s, acc_s, sem,
                       *, layer, n_pages, T, scale, ck, n_streams):
    i = pl.program_id(0)
    n = pl.num_programs(0)
    L = n_pages * PAGE_SIZE

    def page_copies(sl, p, page):
        rows = pl.ds(pl.multiple_of(p * PAGE_SIZE, PAGE_SIZE), PAGE_SIZE)
        return (pltpu.make_async_copy(ckv_hbm.at[layer, page], ckv_buf.at[sl, rows], sem.at[0, sl]),
                pltpu.make_async_copy(krt_hbm.at[layer, page], krt_buf.at[sl, :, rows], sem.at[1, sl]))

    def issue(bb, sl):
        def body(p, carry):
            for cp in page_copies(sl, p, pt_ref[bb, p]):
                cp.start()
            return carry
        lax.fori_loop(0, n_pages, body, 0, unroll=math.gcd(n_pages, 8))

    def wait(sl):
        for p in range(n_pages):
            for cp in page_copies(sl, p, 0):
                cp.wait()

    def attend(sl):
        qr = qr_ref[sl]
        kfn = kfn_ref[sl]
        parts = _build_queries(qn_ref[sl], qr, wuk_ref, scale, T)
        qfull = jnp.concatenate(parts, axis=0)
        qlat = qfull[:, 0:KV_RANK]
        qrs = (qr.astype(F32) * scale)
        qrope = jnp.concatenate(
            [jnp.concatenate([qrs[:, h * HALF_ROPE:(h + 1) * HALF_ROPE],
                              qrs[:, LANES + h * HALF_ROPE:LANES + (h + 1) * HALF_ROPE]], axis=-1)
             for h in range(B_HEADS)], axis=0).astype(BF16)
        nrows = B_HEADS * T
        per_stream = L // n_streams
        n_ck = per_stream // ck
        m_s[...] = jnp.full(m_s.shape, -jnp.inf, F32)
        l_s[...] = jnp.zeros(l_s.shape, F32)
        acc_s[...] = jnp.zeros(acc_s.shape, F32)

        def scores(st, kc, slot):
            keys = slice(st * per_stream + kc * ck, st * per_stream + (kc + 1) * ck)
            kv16[st, slot] = ckv_buf[sl, keys, :].astype(BF16)
            s_s[st, slot] = (lax.dot_general(qlat, kv16[st, slot], _NT, preferred_element_type=F32)
                             + jnp.dot(qrope, krt_buf[sl, :, keys].astype(BF16), preferred_element_type=F32))

        def softmax_pv(st, slot):
            s = s_s[st, slot]
            tiles = [s[:, c * LANES:(c + 1) * LANES] for c in range(ck // LANES)]
            m_prev = m_s[st]
            m_new = jnp.maximum(m_prev, jnp.max(functools.reduce(jnp.maximum, tiles), axis=-1, keepdims=True))
            alpha = jnp.exp(m_prev - m_new)
            ps = [jnp.exp(t - m_new) for t in tiles]
            l_s[st] = alpha * l_s[st] + functools.reduce(jnp.add, ps)
            m_s[st] = m_new
            pv = jnp.dot(jnp.concatenate(ps, axis=-1).astype(BF16), kv16[st, slot], preferred_element_type=F32)
            acc_s[st] = acc_s[st] * jnp.concatenate([alpha] * (KV_RANK // LANES), axis=-1) + pv

        streams = range(n_streams)
        for st in streams:
            scores(st, 0, 0)
        for kc in range(n_ck):
            if kc + 1 < n_ck:
                for st in streams:
                    scores(st, kc + 1, kc + 1)
            for st in streams:
                softmax_pv(st, kc)
        s_new = lax.dot_general(qfull, kfn, _NT, preferred_element_type=F32)
        qt = _mod_pow2(lax.broadcasted_iota(jnp.int32, (nrows, T), 0), T)
        kt = lax.broadcasted_iota(jnp.int32, (nrows, T), 1)
        s_new = jnp.where(kt <= qt, s_new, -jnp.inf)
        m_parts = [m_s[st][:, 0:1] for st in streams] + [jnp.max(s_new, axis=-1, keepdims=True)]
        m = functools.reduce(jnp.maximum, m_parts)
        p_new = jnp.exp(s_new - m)
        l = jnp.sum(p_new, axis=-1, keepdims=True)
        o = jnp.dot(p_new.astype(BF16), kfn[:, 0:KV_RANK], preferred_element_type=F32)
        for st in streams:
            w_st = jnp.exp(m_parts[st] - m)
            l = l + w_st * jnp.sum(l_s[st], axis=-1, keepdims=True)
            o = o + w_st * acc_s[st]
        o = (o / l).astype(BF16)
        outs = [jnp.dot(o[h * T:(h + 1) * T, :], wuv_ref[h], preferred_element_type=F32) for h in range(B_HEADS)]
        o_ref[sl] = jnp.concatenate(outs, axis=-1).astype(o_ref.dtype)

    @pl.when(i == 0)
    def _():
        issue(0, 0)

    issue(2 * i + 1, 1)
    wait(0)
    attend(0)

    @pl.when(i + 1 < n)
    def _():
        issue(2 * i + 2, 0)

    wait(1)
    attend(1)


def _mla_decode(page_table, qn, qr, kfn, wuk, wuv, cache_kv, cache_krt, layer):
    B, T, _ = qn.shape
    assert B % 2 == 0
    n_pages = page_table.shape[1]
    L = n_pages * PAGE_SIZE
    scale = (B_NOPE_DIM + B_ROPE_DIM) ** -0.5
    n_streams = math.gcd(n_pages // 2, 4)
    ck = math.gcd(L // (2 * n_streams), 4 * PAGE_SIZE)
    nrows = B_HEADS * T
    stat = pltpu.VMEM((n_streams, nrows, LANES), F32)
    tile = lambda c: pl.BlockSpec((2, T, c), lambda b, pt: (b, 0, 0))
    cst = lambda shape: pl.BlockSpec(shape, lambda b, pt: (0,) * len(shape))
    grid_spec = pltpu.PrefetchScalarGridSpec(
        num_scalar_prefetch=1,
        grid=(B // 2,),
        in_specs=[tile(A_WIDTH), tile(2 * LANES), tile(KFULL),
                  cst((B_HEADS, B_NOPE_DIM, KV_RANK)), cst((B_HEADS, KV_RANK, B_V_DIM)),
                  pl.BlockSpec(memory_space=pl.ANY), pl.BlockSpec(memory_space=pl.ANY)],
        out_specs=tile(B_WIDTH),
        scratch_shapes=[pltpu.VMEM((2, L, KV_RANK), F32), pltpu.VMEM((2, B_ROPE_DIM, L), F32),
                        pltpu.VMEM((n_streams, L // (n_streams * ck), ck, KV_RANK), BF16),
                        pltpu.VMEM((n_streams, L // (n_streams * ck), nrows, ck), F32),
                        stat, stat, pltpu.VMEM((n_streams, nrows, KV_RANK), F32), pltpu.SemaphoreType.DMA((2, 2))],
    )
    return pl.pallas_call(
        functools.partial(_mla_decode_kernel, layer=layer, n_pages=n_pages, T=T, scale=scale, ck=ck,
                          n_streams=n_streams),
        grid_spec=grid_spec,
        out_shape=jax.ShapeDtypeStruct((B, T, B_WIDTH), BF16),
        compiler_params=_cparams(("arbitrary",)),
        name="mla_decode",
    )(page_table, qn, qr, kfn, wuk, wuv, cache_kv, cache_krt)


def _pool_kernel(pc_ref, st_ref, wbd_ref, scale_ref, y_ref, newst_ref, e_s, s2_s, s4_s, s8_s, *, ts, T, pos0):
    L = T * ts
    P = (POOL_PREFIX + 1) * ts
    pc = pc_ref[...].reshape(L, C_WIDTH)
    e_s[P - POOL_PREFIX * ts:P, :] = st_ref[...].reshape(POOL_PREFIX * ts, C_WIDTH)
    e_s[P:P + L, :] = pc
    newst_ref[...] = e_s[P + L - POOL_PREFIX * ts:P + L, :].reshape(newst_ref.shape)
    s2_s[P - 14 * ts:P + L, :] = e_s[P - 14 * ts:P + L, :] + e_s[P - 15 * ts:P + L - ts, :]
    s4_s[P - 12 * ts:P + L, :] = s2_s[P - 12 * ts:P + L, :] + s2_s[P - 14 * ts:P + L - 2 * ts, :]
    s8_s[P - 8 * ts:P + L, :] = s4_s[P - 8 * ts:P + L, :] + s4_s[P - 12 * ts:P + L - 4 * ts, :]
    s16 = s8_s[P:P + L, :] + s8_s[P - 8 * ts:P + L - 8 * ts, :]
    lane = lax.broadcasted_iota(jnp.int32, (L, C_WIDTH), 1)
    grp = _div_pow2(lane, C_GROUP_DIM)
    win = jnp.where(grp == 0, s2_s[P:P + L, :],
                    jnp.where(grp == 1, s4_s[P:P + L, :], jnp.where(grp == 2, s8_s[P:P + L, :], s16)))
    wsize = jnp.where(grp == 0, 2, jnp.where(grp == 1, 4, jnp.where(grp == 2, 8, 16)))
    pos = pos0 + _div_pow2(lax.broadcasted_iota(jnp.int32, (L, C_WIDTH), 0), ts)
    cnt = jnp.minimum(pos + 1, wsize).astype(F32)
    d = win / cnt - pc
    y = jnp.dot(d.astype(BF16), wbd_ref[...], preferred_element_type=F32) * scale_ref[...]
    y_ref[...] = y.astype(y_ref.dtype).reshape(y_ref.shape)


def _pool(tok, pc, pool_prev, wbd, pscale, pos0):
    ts, T = tok.ts, tok.T
    rows = (POOL_PREFIX + 1 + T) * ts
    if tok.time_major:
        grid, blk = (tok.B // tok.bb, 1), tok.spec(C_WIDTH)
    else:
        grid, blk = (tok.B, 1), pl.BlockSpec((None, T, C_WIDTH), lambda i, j: (i, 0, 0))
    return pl.pallas_call(
        functools.partial(_pool_kernel, ts=ts, T=T, pos0=pos0),
        grid=grid,
        in_specs=[blk, tok.state_spec(POOL_PREFIX, C_WIDTH), _const_spec((C_WIDTH, C_WIDTH)),
                  _const_spec((1, C_WIDTH))],
        out_specs=[blk, tok.state_spec(POOL_PREFIX, C_WIDTH)],
        out_shape=[jax.ShapeDtypeStruct(tok.shape(C_WIDTH), BF16),
                   jax.ShapeDtypeStruct(tok.state_shape(POOL_PREFIX, C_WIDTH), F32)],
        scratch_shapes=[pltpu.VMEM((rows, C_WIDTH), F32)] * 4,
        compiler_params=_cparams(("arbitrary", "arbitrary")),
        name="pool",
    )(pc, pool_prev, wbd, pscale)


def _ffn_kernel(x_ref, ya_ref, g_ref, yb_ref, yc_ref, cst_ref, wout_ref, gffn_ref, wup_ref, cw_ref, cb_ref,
                wdn_ref, gfin_ref, xo_ref, cnew_ref, gbuf_s, *, tm, ts, final):
    taps = CONV_WIDTH - 1
    pad = max(SUBLANES, taps * ts)
    it = pl.program_id(1)

    @pl.when(it == 0)
    def _():
        gbuf_s[pad - taps * ts:pad, :] = cst_ref[...].reshape(taps * ts, D_FF)

    x = x_ref[...].reshape(tm, D_MODEL)
    ya = (ya_ref[...].reshape(tm, A_WIDTH) * g_ref[...].reshape(tm, A_WIDTH)).astype(BF16)
    ycat = jnp.concatenate([ya, yb_ref[...].reshape(tm, B_WIDTH), yc_ref[...].reshape(tm, C_WIDTH)], axis=-1)
    x1 = x + jnp.dot(ycat, wout_ref[...], preferred_element_type=F32)
    xn = _rms(x1, gffn_ref[...]).astype(BF16)
    acc = jnp.zeros((tm, D_MODEL), F32)
    assert sum(FF_CHUNKS) == D_FF
    lo = 0
    for width in FF_CHUNKS:
        hi = lo + width
        up = jnp.dot(xn, wup_ref[:, lo:hi], preferred_element_type=F32)
        gate = jnp.dot(xn, wup_ref[:, D_FF + lo:D_FF + hi], preferred_element_type=F32)
        gbuf_s[pad:pad + tm, lo:hi] = gate
        gc = cb_ref[:, lo:hi] + cw_ref[2:3, lo:hi] * gate
        for j in range(taps):
            off = pad - (taps - j) * ts
            gc = gc + cw_ref[j:j + 1, lo:hi] * gbuf_s[off:off + tm, lo:hi]
        hh = (gc * _sigmoid(gc) * up).astype(BF16)
        acc = acc + jnp.dot(hh, wdn_ref[lo:hi, :], preferred_element_type=F32)
        lo = hi
    last = gbuf_s[pad + tm - taps * ts:pad + tm, :]
    cnew_ref[...] = last.reshape(cnew_ref.shape)
    gbuf_s[pad - taps * ts:pad, :] = last
    x2 = x1 + acc
    if final:
        x2 = _rms(x2, gfin_ref[...])
    xo_ref[...] = x2.reshape(xo_ref.shape)


def _ffn(tok, x, ya, g, yb, yc, conv_prev, p, gfin, final):
    tm, ts = tok.tm, tok.ts
    taps = CONV_WIDTH - 1
    pad = max(SUBLANES, taps * ts)
    return pl.pallas_call(
        functools.partial(_ffn_kernel, tm=tm, ts=ts, final=final),
        grid=tok.grid,
        in_specs=[tok.spec(D_MODEL), tok.spec(A_WIDTH), tok.spec(A_WIDTH), tok.spec(B_WIDTH), tok.spec(C_WIDTH),
                  tok.state_spec(taps, D_FF), _const_spec((D_MODEL, D_MODEL)), _const_spec((1, D_MODEL)),
                  _const_spec((D_MODEL, 2 * D_FF)), _const_spec((CONV_WIDTH, D_FF)), _const_spec((1, D_FF)),
                  _const_spec((D_FF, D_MODEL)), _const_spec((1, D_MODEL))],
        out_specs=[tok.spec(D_MODEL), tok.state_spec(taps, D_FF)],
        out_shape=[jax.ShapeDtypeStruct(tok.shape(D_MODEL), F32),
                   jax.ShapeDtypeStruct(tok.state_shape(taps, D_FF), F32)],
        scratch_shapes=[pltpu.VMEM((pad + tm, D_FF), F32)],
        compiler_params=_cparams(("arbitrary", "arbitrary")),
        name="outproj_ffn",
    )(x, ya, g, yb, yc, conv_prev, p["w_out"], p["g_ffn"], p["w_up"], p["conv_w"], p["conv_b"], p["w_down"], gfin)


def _extend_w_in(w):
    qr0 = A_COLS + B_HEADS * B_NOPE_DIM
    ckv0 = qr0 + B_HEADS * B_ROPE_DIM
    kr0 = ckv0 + KV_RANK
    pc0 = A_COLS + B_COLS
    d = w.shape[0]
    q_rope = w[:, qr0:ckv0].reshape(d, B_HEADS, 2, HALF_ROPE)
    k_rope = w[:, kr0:pc0].reshape(d, 2, HALF_ROPE)
    zeros = lambda n: jnp.zeros((d, n), w.dtype)
    blocks = [w[:, 0:qr0], w[:, ckv0:kr0], w[:, pc0:pc0 + C_WIDTH]]
    for half in range(2):
        blocks += [q_rope[:, :, half, :].reshape(d, B_HEADS * HALF_ROPE), k_rope[:, half, :], zeros(HALF_ROPE)]
    for half in range(2):
        blocks += [jnp.tile(k_rope[:, half, :], (1, B_HEADS)), zeros(LANES - B_HEADS * HALF_ROPE)]
    w_ext = jnp.concatenate(blocks, axis=1)
    assert w_ext.shape[1] == IN_EXT
    return w_ext


def _layer_params(l, norm_mix_g, w_in, mu_shift, decay_w0, decay_w2, iclr_a0, iclr_a2, gate_g2, k_k, k_a, r_k,
                  lnx_w, lnx_b, kv_norm_g, w_uk, w_uv, pool_w, pool_scale, w_out, norm_ffn_g, w_up, conv_w,
                  conv_b, w_down):
    w_ext = _extend_w_in(w_in[l]).astype(BF16)
    lora_cat = jnp.zeros((LANES, 2 * A_WIDTH), F32)
    lora_cat = lora_cat.at[0:DECAY_LORA, 0:A_WIDTH].set(decay_w2[l])
    lora_cat = lora_cat.at[DECAY_LORA:DECAY_LORA + ICLR_LORA, A_WIDTH:].set(iclr_a2[l])
    head = np.arange(A_WIDTH) // A_HEAD_DIM
    head_ones = jnp.asarray((head[:, None] == head[None, :]).astype(np.float32), BF16)
    wbd = jnp.zeros((C_WIDTH, C_WIDTH), F32)
    for gi in range(C_GROUPS):
        sl = slice(gi * C_GROUP_DIM, (gi + 1) * C_GROUP_DIM)
        wbd = wbd.at[sl, sl].set(pool_w[l, gi])
    row = lambda v: v.reshape(1, -1)
    return dict(
        g_mix=row(norm_mix_g[l]), w_ext=w_ext, kv_g=row(kv_norm_g[l]),
        mu=row(mu_shift[l]), lora_cat=lora_cat.astype(BF16), w0=row(decay_w0[l]), a0=row(iclr_a0[l]),
        g2=gate_g2[l].astype(BF16), k_k=row(k_k[l]), k_a=row(k_a[l]), head_ones=head_ones,
        r_k=r_k[l], lnx_w=lnx_w[l].reshape(A_HEADS, A_HEAD_DIM), lnx_b=lnx_b[l].reshape(A_HEADS, A_HEAD_DIM),
        wuk=jnp.transpose(w_uk[l], (1, 2, 0)).astype(BF16), wuv=jnp.transpose(w_uv[l], (1, 0, 2)).astype(BF16),
        pool_wbd=wbd.astype(BF16), pool_scale=row(pool_scale[l]),
        w_out=w_out[l].astype(BF16), g_ffn=row(norm_ffn_g[l]), w_up=w_up[l].astype(BF16),
        conv_w=conv_w[l], conv_b=row(conv_b[l]), w_down=w_down[l].astype(BF16),
    )


def _rope_tables(pos):
    inv = ROPE_BASE ** (-jnp.arange(0, B_ROPE_DIM, 2, dtype=F32) / B_ROPE_DIM)
    ang = pos.astype(F32)[:, None] * inv[None, :]
    reps = LANES // HALF_ROPE
    return jnp.tile(jnp.cos(ang), (1, reps)), jnp.tile(jnp.sin(ang), (1, reps))


def _to_scan(tok, x):
    assert tok.time_major
    x = x.reshape(tok.T, tok.B, A_HEADS, A_HEAD_DIM)
    return jnp.transpose(x, (2, 0, 3, 1))


def _split_lanes(tok, x):
    half = A_HEADS // 2
    return x[..., :A_HEADS * tok.B].reshape(x.shape[:-1] + (2, half, tok.B))


def _merge_lanes(x):
    x = x.reshape(x.shape[:-3] + (-1,))
    return jnp.pad(x, [(0, 0)] * (x.ndim - 1) + [(0, LANES - x.shape[-1])])


def _from_scan(tok, y):
    H, N = A_HEADS, A_HEAD_DIM
    if tok.time_major:
        return jnp.transpose(y, (1, 3, 0, 2)).reshape(tok.T, tok.B, H * N)
    y = _split_lanes(tok, y[0])
    return jnp.transpose(y, (4, 0, 3, 2, 1)).reshape(tok.B, tok.T, H * N)


def _state_to_scan(tok, s):
    if tok.time_major:
        return jnp.transpose(s, (1, 3, 2, 0))
    s = s.reshape(tok.B, A_HEADS // 2, 2, A_HEAD_DIM, A_HEAD_DIM)
    return _merge_lanes(jnp.transpose(s, (4, 3, 2, 1, 0)))[None]


def _state_from_scan(tok, s):
    if tok.time_major:
        return jnp.transpose(s, (3, 0, 2, 1))
    s = jnp.transpose(_split_lanes(tok, s[0]), (4, 3, 2, 1, 0))
    return s.reshape(tok.B, A_HEADS, A_HEAD_DIM, A_HEAD_DIM)


def _head_consts(tok, p):
    def lay(v):
        if tok.time_major:
            return jnp.broadcast_to(v[:, :, None], (A_HEADS, A_HEAD_DIM, LANES))
        v = jnp.transpose(v.reshape(A_HEADS // 2, 2, A_HEAD_DIM), (2, 1, 0))
        return _merge_lanes(jnp.broadcast_to(v[..., None], v.shape + (tok.B,)))[None]
    return lay(p["lnx_w"]), lay(p["lnx_b"]), lay(p["r_k"])


def _layer(tok, x, cos, sin, pos0, shift_prev, wkv_prev, pool_prev, conv_prev, p, gfin, final, scan_tc, attend):
    pa, qn, kvlat, pc, qr, kfull, krope = _inproj(tok, x, p["g_mix"], p["w_ext"], p["kv_g"], cos, sin)
    if tok.time_major:
        r, w, k, v, a, b, g, shift_new = _rwkv_prep(tok, pa, shift_prev, p)
        r, w, k, v, a, b = (_to_scan(tok, t) for t in (r, w, k, v, a, b))
    else:
        r, w, k, v, a, b, g, shift_new = _rwkv_prep_scan(pa, shift_prev, p, min(scan_tc, 32))
    lnw, lnb, rk = _head_consts(tok, p)
    y, s_new = _rwkv_scan(w, a, b, k, v, r, _state_to_scan(tok, wkv_prev), lnw, lnb, rk, scan_tc)
    ya = _from_scan(tok, y)
    yb = attend(qn, qr, kfull)
    yc, pool_new = _pool(tok, pc, pool_prev, p["pool_wbd"], p["pool_scale"], pos0)
    x, conv_new = _ffn(tok, x, ya, g, yb, yc, conv_prev, p, gfin, final)
    return x, (kvlat, krope, _state_from_scan(tok, s_new), shift_new, pool_new, conv_new)


def kernel(x_prompt, x_sample, cache_kv_latent, cache_k_rope, page_table, state_wkv, state_shift, state_pool,
           state_conv, norm_mix_g, w_in, mu_shift, decay_w0, decay_w2, iclr_a0, iclr_a2, gate_g2, k_k, k_a, r_k,
           lnx_w, lnx_b, kv_norm_g, w_uk, w_uv, pool_w, pool_scale, w_out, norm_ffn_g, w_up, conv_w, conv_b,
           w_down, norm_final_g):
    Bp, Tp, _ = x_prompt.shape
    Bs, Ts, _ = x_sample.shape
    past_len = page_table.shape[1] * PAGE_SIZE
    tokp = _Tok(False, Bp, Tp, min(512, Tp))
    toks = _Tok(True, Bs, Ts, min(32, Bs))
    cos_p, sin_p = _rope_tables(jnp.arange(Tp, dtype=jnp.int32))
    pos_s = past_len + jnp.repeat(jnp.arange(Ts, dtype=jnp.int32), toks.bb)
    cos_s, sin_s = _rope_tables(pos_s)
    gfin = norm_final_g.reshape(1, -1)
    tm_first = lambda t: jnp.swapaxes(t, 0, 1)
    cache_krt = jnp.swapaxes(cache_k_rope, 2, 3)

    xp = x_prompt
    xs = tm_first(x_sample)
    zeros = lambda *shape: jnp.zeros(shape, F32)
    outs_p, outs_s = [], []
    for l in range(DEPTH):
        p = _layer_params(l, norm_mix_g, w_in, mu_shift, decay_w0, decay_w2, iclr_a0, iclr_a2, gate_g2, k_k, k_a,
                          r_k, lnx_w, lnx_b, kv_norm_g, w_uk, w_uv, pool_w, pool_scale, w_out, norm_ffn_g, w_up,
                          conv_w, conv_b, w_down)
        final = l == DEPTH - 1

        attend_p = lambda qn, qr, kfull: _mla_prompt(qn, qr, kfull, p["wuk"], p["wuv"], min(256, Tp))
        xp, st_p = _layer(tokp, xp, cos_p, sin_p, 0, zeros(Bp, A_COLS),
                          zeros(Bp, A_HEADS, A_HEAD_DIM, A_HEAD_DIM), zeros(Bp, POOL_PREFIX, C_WIDTH),
                          zeros(Bp, CONV_WIDTH - 1, D_FF), p, gfin, final, min(64, Tp), attend_p)

        def attend_s(qn, qr, kfull):
            yb = _mla_decode(page_table, tm_first(qn), tm_first(qr), tm_first(kfull), p["wuk"], p["wuv"],
                             cache_kv_latent, cache_krt, l)
            return tm_first(yb)
        xs, st_s = _layer(toks, xs, cos_s, sin_s, past_len, state_shift[l][None], state_wkv[l],
                          tm_first(state_pool[l]), tm_first(state_conv[l]), p, gfin, final, Ts, attend_s)
        outs_p.append(st_p)
        outs_s.append(st_s)

    stack = lambda outs, i, f: jnp.stack([f(o[i]) for o in outs], axis=0)
    ident = lambda t: t
    return (xp, tm_first(xs),
            stack(outs_p, 0, ident), stack(outs_p, 1, ident), stack(outs_p, 2, ident),
            stack(outs_p, 3, ident), stack(outs_p, 4, ident), stack(outs_p, 5, ident),
            stack(outs_s, 0, tm_first), stack(outs_s, 1, tm_first), stack(outs_s, 2, ident),
            stack(outs_s, 3, lambda t: t[0]), stack(outs_s, 4, tm_first), stack(outs_s, 5, tm_first))
```

```python
import functools
import math

import jax
import jax.numpy as jnp
import numpy as np
from jax import lax
from jax.experimental import pallas as pl
from jax.experimental.pallas import tpu as pltpu

F32 = jnp.float32
BF16 = jnp.bfloat16

D_MODEL = 1024
DEPTH = 2
PAGE_SIZE = 128
A_HEADS = 6
A_HEAD_DIM = 64
A_WIDTH = A_HEADS * A_HEAD_DIM
DECAY_LORA = 64
ICLR_LORA = 64
GATE_LORA = 128
A_COLS = 3 * A_WIDTH + DECAY_LORA + ICLR_LORA + GATE_LORA
B_HEADS = 6
B_NOPE_DIM = 64
B_ROPE_DIM = 32
B_V_DIM = 64
B_WIDTH = B_HEADS * B_V_DIM
KV_RANK = 256
ROPE_BASE = 10000.0
B_COLS = B_HEADS * (B_NOPE_DIM + B_ROPE_DIM) + KV_RANK + B_ROPE_DIM
C_GROUPS = 4
C_GROUP_DIM = 64
C_WIDTH = C_GROUPS * C_GROUP_DIM
POOL_WINDOWS = (2, 4, 8, 16)
POOL_PREFIX = 15
D_FF = 2816
CONV_WIDTH = 3
NORM_EPS = 1e-6
GN_EPS = A_HEAD_DIM * 1e-5

LANES = 128
SUBLANES = 8
HALF_ROPE = B_ROPE_DIM // 2
IN_EXT = A_COLS + B_HEADS * B_NOPE_DIM + KV_RANK + C_WIDTH + 4 * LANES
OFF_QN = A_COLS
OFF_CKV = OFF_QN + B_HEADS * B_NOPE_DIM
OFF_PC = OFF_CKV + KV_RANK
OFF_ROPE = OFF_PC + C_WIDTH
KFULL = KV_RANK + 2 * LANES
MXU_TILE = 256
FF_CHUNKS = (6 * MXU_TILE, 5 * MXU_TILE)
VMEM_LIMIT = 56 * 1024 * 1024


def _cparams(sem):
    return pltpu.CompilerParams(dimension_semantics=sem, vmem_limit_bytes=VMEM_LIMIT)


def _const_spec(shape):
    nd = len(shape)
    return pl.BlockSpec(shape, lambda *_: (0,) * nd, pipeline_mode=pl.Buffered(1))


class _Tok:
    def __init__(self, time_major, B, T, tile):
        self.time_major, self.B, self.T = time_major, B, T
        if time_major:
            self.bb = tile
            self.tm, self.ts = T * tile, tile
            self.grid = (B // tile, 1)
        else:
            self.tm, self.ts = tile, 1
            self.grid = (B, T // tile)
        self.n = B * T

    def shape(self, C):
        return (self.T, self.B, C) if self.time_major else (self.B, self.T, C)

    def spec(self, C):
        if self.time_major:
            return pl.BlockSpec((self.T, self.bb, C), lambda i, j: (0, i, 0))
        return pl.BlockSpec((None, self.tm, C), lambda i, j: (i, j, 0))

    def state_shape(self, steps, C):
        return (steps, self.B, C) if self.time_major else (self.B, steps, C)

    def state_spec(self, steps, C):
        if self.time_major:
            return pl.BlockSpec((steps, self.bb, C), lambda i, j: (0, i, 0))
        return pl.BlockSpec((None, steps, C), lambda i, j: (i, 0, 0))


def _rms(x, g):
    return x * lax.rsqrt(jnp.mean(x * x, axis=-1, keepdims=True) + NORM_EPS) * g


def _sigmoid(x):
    return 1.0 / (1.0 + jnp.exp(-x))


def _div_pow2(x, d):
    assert d & (d - 1) == 0
    return lax.shift_right_logical(x, int(math.log2(d)))


def _mod_pow2(x, d):
    assert d & (d - 1) == 0
    return lax.bitwise_and(x, d - 1)


def _inproj_kernel(x_ref, g_ref, w_ref, kvg_ref, cos_ref, sin_ref,
                   pa_ref, qn_ref, kvlat_ref, pc_ref, qr_ref, kfull_ref, krope_ref, *, tm):
    x = x_ref[...].reshape(tm, D_MODEL)
    xb = _rms(x, g_ref[...]).astype(BF16)

    def proj(lo, hi):
        return jnp.dot(xb, w_ref[:, lo:hi], preferred_element_type=F32)

    assert OFF_CKV % MXU_TILE == 0 and OFF_ROPE % MXU_TILE == 0 and IN_EXT % MXU_TILE == 0
    head = proj(0, OFF_CKV)
    pa_ref[...] = head[:, 0:A_COLS].reshape(pa_ref.shape)
    qn_ref[...] = head[:, OFF_QN:OFF_CKV].astype(BF16).reshape(qn_ref.shape)
    mid = proj(OFF_CKV, OFF_ROPE)
    ckv = _rms(mid[:, 0:KV_RANK], kvg_ref[...])
    kvlat_ref[...] = ckv.reshape(kvlat_ref.shape)
    pc_ref[...] = mid[:, KV_RANK:KV_RANK + C_WIDTH].reshape(pc_ref.shape)
    rr = proj(OFF_ROPE, IN_EXT)
    cos = cos_ref[...]
    sin = sin_ref[...]
    q1, q2 = rr[:, 0:LANES], rr[:, LANES:2 * LANES]
    k1, k2 = rr[:, 2 * LANES:3 * LANES], rr[:, 3 * LANES:4 * LANES]
    o1 = q1 * cos - q2 * sin
    o2 = q1 * sin + q2 * cos
    qr = jnp.concatenate([o1, o2], axis=-1).astype(BF16)
    qr_ref[...] = qr.reshape(qr_ref.shape)
    kf = jnp.concatenate([ckv, k1 * cos - k2 * sin, k1 * sin + k2 * cos], axis=-1).astype(BF16)
    kfull_ref[...] = kf.reshape(kfull_ref.shape)
    kr_lo = B_HEADS * HALF_ROPE
    krope = jnp.concatenate([o1[:, kr_lo:kr_lo + HALF_ROPE], o2[:, kr_lo:kr_lo + HALF_ROPE]], axis=-1)
    krope_ref[...] = krope.reshape(krope_ref.shape)


def _inproj(tok, x, g, w_ext, kvg, cos, sin):
    tm = tok.tm
    n_tab = cos.shape[0] // tm
    tab_spec = pl.BlockSpec((tm, LANES), lambda i, j: (j % n_tab, 0))
    widths = (A_COLS, A_WIDTH, KV_RANK, C_WIDTH, 2 * LANES, KFULL, B_ROPE_DIM)
    dtypes = (F32, BF16, F32, F32, BF16, BF16, F32)
    return pl.pallas_call(
        functools.partial(_inproj_kernel, tm=tm),
        grid=tok.grid,
        in_specs=[tok.spec(D_MODEL), _const_spec((1, D_MODEL)), _const_spec((D_MODEL, IN_EXT)),
                  _const_spec((1, KV_RANK)), tab_spec, tab_spec],
        out_specs=[tok.spec(c) for c in widths],
        out_shape=[jax.ShapeDtypeStruct(tok.shape(c), d) for c, d in zip(widths, dtypes)],
        compiler_params=_cparams(("arbitrary", "arbitrary")),
        name="inproj",
    )(x, g, w_ext, kvg, cos, sin)


def _rwkv_token_math(pa, prev, mu_ref, wcat_ref, w0_ref, a0_ref, g2_ref, kk_ref, ka_ref, ones_ref):
    s = pa + (prev - pa) * mu_ref[...]
    W = A_WIDTH
    r, k, v = s[:, 0:W], s[:, W:2 * W], s[:, 2 * W:3 * W]
    lo_in = s[:, 3 * W:3 * W + LANES]
    gd = s[:, 3 * W + LANES:3 * W + 2 * LANES]
    lane = lax.broadcasted_iota(jnp.int32, lo_in.shape, 1)
    lo_in = jnp.where(lane < DECAY_LORA, jnp.tanh(lo_in), lo_in)
    lo = jnp.dot(lo_in.astype(BF16), wcat_ref[...], preferred_element_type=F32)
    z = -(w0_ref[...] + lo[:, 0:W])
    softplus = jnp.maximum(z, 0.0) + jnp.log(1.0 + jnp.exp(-jnp.abs(z)))
    decay = jnp.exp(-jnp.exp(-softplus - 0.5))
    a = _sigmoid(a0_ref[...] + lo[:, W:2 * W])
    g = jnp.dot(_sigmoid(gd).astype(BF16), g2_ref[...], preferred_element_type=F32)
    kk = k * kk_ref[...]
    kk2 = kk * kk
    hi = kk2.astype(BF16)
    lo2 = (kk2 - hi.astype(F32)).astype(BF16)
    ones = ones_ref[...]
    ss = (jnp.dot(hi, ones, preferred_element_type=F32) + jnp.dot(lo2, ones, preferred_element_type=F32))
    kk = kk * lax.rsqrt(jnp.maximum(ss, 1e-24))
    kmod = k * (1.0 + (a - 1.0) * ka_ref[...])
    return r, decay, kmod, v, -kk, kk * a, g


def _rwkv_prep_kernel(pa_ref, st_ref, mu_ref, wcat_ref, w0_ref, a0_ref, g2_ref, kk_ref, ka_ref, ones_ref,
                      r_ref, w_ref, k_ref, v_ref, a_ref, b_ref, g_ref, newst_ref, buf_ref, *, tm, ts):
    pad = max(SUBLANES, ts)
    it = pl.program_id(1)

    @pl.when(it == 0)
    def _():
        buf_ref[pad - ts:pad, :] = st_ref[...].reshape(ts, A_COLS)

    pa = pa_ref[...].reshape(tm, A_COLS)
    buf_ref[pad:pad + tm, :] = pa
    prev = buf_ref[pad - ts:pad - ts + tm, :]
    last = pa[tm - ts:tm, :]
    buf_ref[pad - ts:pad, :] = last
    newst_ref[...] = last.reshape(newst_ref.shape)
    vals = _rwkv_token_math(pa, prev, mu_ref, wcat_ref, w0_ref, a0_ref, g2_ref, kk_ref, ka_ref, ones_ref)
    for ref, val in zip((r_ref, w_ref, k_ref, v_ref, a_ref, b_ref, g_ref), vals):
        ref[...] = val.reshape(ref.shape)


def _prompt_lane(b, h, nb):
    return (h % 2) * (A_HEADS // 2) * nb + (h // 2) * nb + b


def _rwkv_prep_scan_kernel(pa_ref, st_ref, mu_ref, wcat_ref, w0_ref, a0_ref, g2_ref, kk_ref, ka_ref, ones_ref,
                           r_ref, w_ref, k_ref, v_ref, a_ref, b_ref, g_ref, newst_ref, buf_ref, q_ref, *, nb, tt):
    it = pl.program_id(0)

    @pl.when(it == 0)
    def _():
        buf_ref[0:nb, :] = st_ref[...]

    for t in range(tt):
        buf_ref[(t + 1) * nb:(t + 2) * nb, :] = pa_ref[:, t, :]
    pa = buf_ref[nb:(tt + 1) * nb, :]
    prev = buf_ref[0:tt * nb, :]
    vals = _rwkv_token_math(pa, prev, mu_ref, wcat_ref, w0_ref, a0_ref, g2_ref, kk_ref, ka_ref, ones_ref)
    last = buf_ref[tt * nb:(tt + 1) * nb, :]
    buf_ref[0:nb, :] = last
    newst_ref[...] = last
    g = vals[6]
    for t in range(tt):
        g_ref[:, t, :] = g[t * nb:(t + 1) * nb, :]
    for qi in range(6):
        q_ref[qi] = vals[qi]
    outs = (r_ref, w_ref, k_ref, v_ref, a_ref, b_ref)
    half = A_HEADS // 2
    win = half * nb
    N = A_HEAD_DIM

    def pair(pi, carry):
        for qi, o_ref in enumerate(outs):
            rows = q_ref[qi, pl.ds(pl.multiple_of(pi * 2 * nb, 2 * nb), 2 * nb), :]
            pieces = [rows[s * nb:(s + 1) * nb, hp * LANES:(hp + 1) * LANES] for s in range(2) for hp in range(half)]
            m = jnp.concatenate(pieces + [jnp.zeros((LANES - 2 * win, LANES), F32)], axis=0)
            mt = m.T
            unused = jnp.full((N, LANES - 2 * win), 1.0 if o_ref is w_ref else 0.0, F32)
            for s in range(2):
                tile = jnp.concatenate([mt[0:N, s * win:(s + 1) * win], mt[N:2 * N, s * win:(s + 1) * win], unused],
                                       axis=1)
                o_ref[0, pi * 2 + s] = tile
        return carry

    lax.fori_loop(0, tt // 2, pair, 0, unroll=8)


def _rwkv_prep_scan(pa, shift_prev, p, tt):
    nb, T, _ = pa.shape
    assert 2 * (A_HEADS // 2) * nb <= LANES and tt % 2 == 0
    vec = _const_spec((1, A_WIDTH))
    seq = pl.BlockSpec((1, tt, A_HEAD_DIM, LANES), lambda i: (0, i, 0, 0))
    st = pl.BlockSpec((nb, A_COLS), lambda i: (0, 0))
    return pl.pallas_call(
        functools.partial(_rwkv_prep_scan_kernel, nb=nb, tt=tt),
        grid=(T // tt,),
        in_specs=[pl.BlockSpec((nb, tt, A_COLS), lambda i: (0, i, 0)), st, _const_spec((1, A_COLS)),
                  _const_spec((LANES, 2 * A_WIDTH)), vec, vec, _const_spec((GATE_LORA, A_WIDTH)), vec, vec,
                  _const_spec((A_WIDTH, A_WIDTH))],
        out_specs=[seq] * 6 + [pl.BlockSpec((nb, tt, A_WIDTH), lambda i: (0, i, 0)), st],
        out_shape=[jax.ShapeDtypeStruct((1, T, A_HEAD_DIM, LANES), F32)] * 6
        + [jax.ShapeDtypeStruct((nb, T, A_WIDTH), F32), jax.ShapeDtypeStruct((nb, A_COLS), F32)],
        scratch_shapes=[pltpu.VMEM(((tt + 1) * nb, A_COLS), F32), pltpu.VMEM((6, tt * nb, A_WIDTH), F32)],
        compiler_params=_cparams(("arbitrary",)),
        name="rwkv_prep_scan",
    )(pa, shift_prev, p["mu"], p["lora_cat"], p["w0"], p["a0"], p["g2"], p["k_k"], p["k_a"], p["head_ones"])


def _rwkv_prep(tok, pa, shift_prev, p):
    tm, ts = tok.tm, tok.ts
    pad = max(SUBLANES, ts)
    vec = _const_spec((1, A_WIDTH))
    return pl.pallas_call(
        functools.partial(_rwkv_prep_kernel, tm=tm, ts=ts),
        grid=tok.grid,
        in_specs=[tok.spec(A_COLS), tok.state_spec(1, A_COLS), _const_spec((1, A_COLS)),
                  _const_spec((LANES, 2 * A_WIDTH)), vec, vec, _const_spec((GATE_LORA, A_WIDTH)), vec, vec,
                  _const_spec((A_WIDTH, A_WIDTH))],
        out_specs=[tok.spec(A_WIDTH)] * 7 + [tok.state_spec(1, A_COLS)],
        out_shape=[jax.ShapeDtypeStruct(tok.shape(A_WIDTH), F32)] * 7
        + [jax.ShapeDtypeStruct(tok.state_shape(1, A_COLS), F32)],
        scratch_shapes=[pltpu.VMEM((pad + tm, A_COLS), F32)],
        compiler_params=_cparams(("arbitrary", "arbitrary")),
        name="rwkv_prep",
    )(pa, shift_prev, p["mu"], p["lora_cat"], p["w0"], p["a0"], p["g2"], p["k_k"], p["k_a"], p["head_ones"])


def _rwkv_scan_kernel(w_ref, a_ref, b_ref, k_ref, v_ref, r_ref, s0_ref, lnw_ref, lnb_ref, rk_ref,
                      y_ref, sT_ref, S_ref, op_ref, *, tc):
    N = A_HEAD_DIM
    it = pl.program_id(1)

    @pl.when(it == 0)
    def _():
        S_ref[...] = s0_ref[0]

    def step(t, p_prev):
        w = w_ref[0, t]
        r = r_ref[0, t]
        k = k_ref[0, t]
        v = v_ref[0, t]
        p_new = p_prev * w
        inv = 1.0 / p_new
        rt = r * p_new
        bt = b_ref[0, t] * inv
        kt = k * inv
        op_ref[0] = a_ref[0, t] * p_prev
        op_ref[1] = rt
        op_ref[2] = bt
        op_ref[3] = kt
        sa = [jnp.zeros((N, LANES), F32) for _ in range(2)]
        yp = [jnp.zeros((N, LANES), F32) for _ in range(2)]
        for j in range(N):
            Zj = S_ref[j]
            sa[j % 2] = sa[j % 2] + Zj * op_ref[0, pl.ds(j, 1), :]
            yp[j % 2] = yp[j % 2] + Zj * op_ref[1, pl.ds(j, 1), :]
        sa = sa[0] + sa[1]
        yp = yp[0] + yp[1]
        for j in range(N):
            S_ref[j] = S_ref[j] + sa * op_ref[2, pl.ds(j, 1), :] + v * op_ref[3, pl.ds(j, 1), :]
        br = jnp.sum(bt * rt, axis=0, keepdims=True)
        kr = jnp.sum(kt * rt, axis=0, keepdims=True)
        y = yp + sa * br + v * kr
        mu = jnp.mean(y, axis=0, keepdims=True)
        d = y - mu
        var = jnp.mean(d * d, axis=0, keepdims=True)
        yn = d * lax.rsqrt(var + GN_EPS) * lnw_ref[0] + lnb_ref[0]
        bonus = jnp.sum(r * k * rk_ref[0], axis=0, keepdims=True)
        y_ref[0, t] = yn + bonus * v
        return p_new

    op_ref[0] = lax.fori_loop(0, tc, step, jnp.ones((N, LANES), F32))
    for j in range(N):
        S_ref[j] = S_ref[j] * op_ref[0, pl.ds(j, 1), :]

    @pl.when(it == pl.num_programs(1) - 1)
    def _():
        sT_ref[0] = S_ref[...]


def _rwkv_scan(w, a, b, k, v, r, s0, lnw, lnb, rk, tc):
    G, T, N, _ = w.shape
    seq = pl.BlockSpec((1, tc, N, LANES), lambda g, i: (g, i, 0, 0))
    st = pl.BlockSpec((1, N, N, LANES), lambda g, i: (g, 0, 0, 0))
    cst = pl.BlockSpec((1, N, LANES), lambda g, i: (g, 0, 0))
    return pl.pallas_call(
        functools.partial(_rwkv_scan_kernel, tc=tc),
        grid=(G, T // tc),
        in_specs=[seq] * 6 + [st, cst, cst, cst],
        out_specs=[seq, st],
        out_shape=[jax.ShapeDtypeStruct((G, T, N, LANES), F32), jax.ShapeDtypeStruct((G, N, N, LANES), F32)],
        scratch_shapes=[pltpu.VMEM((N, N, LANES), F32), pltpu.VMEM((4, N, LANES), F32)],
        compiler_params=_cparams(("arbitrary", "arbitrary")),
        name="rwkv_scan",
    )(w, a, b, k, v, r, s0, lnw, lnb, rk)


_NT = (((1,), (1,)), ((), ()))


def _build_queries(qn, qr, wuk_ref, scale, rows):
    lane = lax.broadcasted_iota(jnp.int32, (rows, 2 * LANES), 1)
    slot = _div_pow2(_mod_pow2(lane, LANES), HALF_ROPE)
    qrf = qr.astype(F32) * scale
    parts = []
    for h in range(B_HEADS):
        ql = jnp.dot(qn[:, h * B_NOPE_DIM:(h + 1) * B_NOPE_DIM], wuk_ref[h], preferred_element_type=F32) * scale
        parts.append(jnp.concatenate([ql, jnp.where(slot == h, qrf, 0.0)], axis=-1).astype(BF16))
    return parts


def _mla_prompt_kernel(qn_ref, qr_ref, kf_ref, wuk_ref, wuv_ref, o_ref, q_s, s_s, p_s, m_s, l_s, a_s, acc_s,
                       *, tq, scale, row_block):
    qi = pl.program_id(1)
    parts = _build_queries(qn_ref[...], qr_ref[...], wuk_ref, scale, tq)
    for h in range(B_HEADS):
        q_s[h * tq:(h + 1) * tq, :] = parts[h]
    rows = B_HEADS * tq
    m_s[...] = jnp.full((rows, LANES), -jnp.inf, F32)
    l_s[...] = jnp.zeros((rows, LANES), F32)
    acc_s[...] = jnp.zeros((rows, KV_RANK), F32)

    def keys(kc):
        return pl.ds(pl.multiple_of(kc * tq, tq), tq)

    def scores(kc, slot):
        s_s[slot] = lax.dot_general(q_s[...], kf_ref[keys(kc), :], _NT, preferred_element_type=F32)

    def lane_tiles(s):
        return [s[:, c * LANES:(c + 1) * LANES] for c in range(tq // LANES)]

    def softmax_pv(kc, slot, masked):
        blocks = [slice(rb * row_block, (rb + 1) * row_block) for rb in range(rows // row_block)]
        for rb, blk in enumerate(blocks):
            s = s_s[slot, blk, :]
            if masked:
                qt = _mod_pow2(rb * row_block + lax.broadcasted_iota(jnp.int32, (row_block, tq), 0), tq)
                kpos = lax.broadcasted_iota(jnp.int32, (row_block, tq), 1)
                s = jnp.where(kpos <= qt, s, -jnp.inf)
                s_s[slot, blk, :] = s
            mx = functools.reduce(jnp.maximum, lane_tiles(s))
            m_prev = m_s[blk, :]
            m_new = jnp.maximum(m_prev, jnp.max(mx, axis=-1, keepdims=True))
            a_s[blk, :] = jnp.exp2(m_prev - m_new)
            m_s[blk, :] = m_new
        for blk in blocks:
            m_new = m_s[blk, :]
            ps = [jnp.exp2(t - m_new) for t in lane_tiles(s_s[slot, blk, :])]
            l_s[blk, :] = a_s[blk, :] * l_s[blk, :] + functools.reduce(jnp.add, ps)
            p_s[blk, :] = jnp.concatenate(ps, axis=-1).astype(BF16)
        pv = jnp.dot(p_s[...], kf_ref[keys(kc), 0:KV_RANK], preferred_element_type=F32)
        alpha = a_s[...]
        acc_s[...] = acc_s[...] * jnp.concatenate([alpha] * (KV_RANK // LANES), axis=-1) + pv

    scores(0, 0)

    def pair(i, carry):
        scores(2 * i + 1, 1)
        softmax_pv(2 * i, 0, False)
        scores(2 * i + 2, 0)
        softmax_pv(2 * i + 1, 1, False)
        return carry

    lax.fori_loop(0, qi // 2, pair, 0)

    @pl.when(qi % 2 == 1)
    def _():
        scores(qi, 1)
        softmax_pv(qi - 1, 0, False)
        softmax_pv(qi, 1, True)

    @pl.when(qi % 2 == 0)
    def _():
        softmax_pv(qi, 0, True)

    o = (acc_s[...] * (1.0 / jnp.sum(l_s[...], axis=-1, keepdims=True))).astype(BF16)
    outs = [jnp.dot(o[h * tq:(h + 1) * tq, :], wuv_ref[h], preferred_element_type=F32) for h in range(B_HEADS)]
    o_ref[...] = jnp.concatenate(outs, axis=-1).astype(o_ref.dtype)


def _mla_prompt(qn, qr, kfull, wuk, wuv, tq):
    B, T, _ = qn.shape
    scale = (B_NOPE_DIM + B_ROPE_DIM) ** -0.5 * math.log2(math.e)
    rows = B_HEADS * tq
    tile = lambda c: pl.BlockSpec((None, tq, c), lambda b, i: (b, i, 0))
    stat = pltpu.VMEM((rows, LANES), F32)
    return pl.pallas_call(
        functools.partial(_mla_prompt_kernel, tq=tq, scale=scale, row_block=min(128, tq)),
        grid=(B, T // tq),
        in_specs=[tile(A_WIDTH), tile(2 * LANES), pl.BlockSpec((None, T, KFULL), lambda b, i: (b, 0, 0)),
                  _const_spec((B_HEADS, B_NOPE_DIM, KV_RANK)), _const_spec((B_HEADS, KV_RANK, B_V_DIM))],
        out_specs=tile(B_WIDTH),
        out_shape=jax.ShapeDtypeStruct((B, T, B_WIDTH), BF16),
        scratch_shapes=[pltpu.VMEM((rows, KFULL), BF16), pltpu.VMEM((2, rows, tq), F32),
                        pltpu.VMEM((rows, tq), BF16), stat, stat, stat, pltpu.VMEM((rows, KV_RANK), F32)],
        compiler_params=_cparams(("arbitrary", "arbitrary")),
        name="mla_prompt",
    )(qn, qr, kfull, wuk, wuv)


def _mla_decode_kernel(pt_ref, qn_ref, qr_ref, kfn_ref, wuk_ref, wuv_ref, ckv_hbm, krt_hbm,
                       o_ref, ckv_buf, krt_buf, kv16, s_s, m_s, l_s, acc_s, sem,
                       *, layer, n_pages, T, scale, ck, n_streams):
    i = pl.program_id(0)
    n = pl.num_programs(0)
    L = n_pages * PAGE_SIZE

    def page_copies(sl, p, page):
        rows = pl.ds(pl.multiple_of(p * PAGE_SIZE, PAGE_SIZE), PAGE_SIZE)
        return (pltpu.make_async_copy(ckv_hbm.at[layer, page], ckv_buf.at[sl, rows], sem.at[0, sl]),
                pltpu.make_async_copy(krt_hbm.at[layer, page], krt_buf.at[sl, :, rows], sem.at[1, sl]))

    def issue(bb, sl):
        def body(p, carry):
            for cp in page_copies(sl, p, pt_ref[bb, p]):
                cp.start()
            return carry
        lax.fori_loop(0, n_pages, body, 0, unroll=math.gcd(n_pages, 8))

    def wait(sl):
        for p in range(n_pages):
            for cp in page_copies(sl, p, 0):
                cp.wait()

    def attend(sl):
        qr = qr_ref[sl]
        kfn = kfn_ref[sl]
        parts = _build_queries(qn_ref[sl], qr, wuk_ref, scale, T)
        qfull = jnp.concatenate(parts, axis=0)
        qlat = qfull[:, 0:KV_RANK]
        qrs = (qr.astype(F32) * scale)
        qrope = jnp.concatenate(
            [jnp.concatenate([qrs[:, h * HALF_ROPE:(h + 1) * HALF_ROPE],
                              qrs[:, LANES + h * HALF_ROPE:LANES + (h + 1) * HALF_ROPE]], axis=-1)
             for h in range(B_HEADS)], axis=0).astype(BF16)
        nrows = B_HEADS * T
        per_stream = L // n_streams
        n_ck = per_stream // ck
        m_s[...] = jnp.full(m_s.shape, -jnp.inf, F32)
        l_s[...] = jnp.zeros(l_s.shape, F32)
        acc_s[...] = jnp.zeros(acc_s.shape, F32)

        def scores(st, kc, slot):
            keys = slice(st * per_stream + kc * ck, st * per_stream + (kc + 1) * ck)
            kv16[st, slot] = ckv_buf[sl, keys, :].astype(BF16)
            s_s[st, slot] = (lax.dot_general(qlat, kv16[st, slot], _NT, preferred_element_type=F32)
                             + jnp.dot(qrope, krt_buf[sl, :, keys].astype(BF16), preferred_element_type=F32))

        def softmax_pv(st, slot):
            s = s_s[st, slot]
            tiles = [s[:, c * LANES:(c + 1) * LANES] for c in range(ck // LANES)]
            m_prev = m_s[st]
            m_new = jnp.maximum(m_prev, jnp.max(functools.reduce(jnp.maximum, tiles), axis=-1, keepdims=True))
            alpha = jnp.exp(m_prev - m_new)
            ps = [jnp.exp(t - m_new) for t in tiles]
            l_s[st] = alpha * l_s[st] + functools.reduce(jnp.add, ps)
            m_s[st] = m_new
            pv = jnp.dot(jnp.concatenate(ps, axis=-1).astype(BF16), kv16[st, slot], preferred_element_type=F32)
            acc_s[st] = acc_s[st] * jnp.concatenate([alpha] * (KV_RANK // LANES), axis=-1) + pv

        streams = range(n_streams)
        for st in streams:
            scores(st, 0, 0)
        for kc in range(n_ck):
            if kc + 1 < n_ck:
                for st in streams:
                    scores(st, kc + 1, kc + 1)
            for st in streams:
                softmax_pv(st, kc)
        s_new = lax.dot_general(qfull, kfn, _NT, preferred_element_type=F32)
        qt = _mod_pow2(lax.broadcasted_iota(jnp.int32, (nrows, T), 0), T)
        kt = lax.broadcasted_iota(jnp.int32, (nrows, T), 1)
        s_new = jnp.where(kt <= qt, s_new, -jnp.inf)
        m_parts = [m_s[st][:, 0:1] for st in streams] + [jnp.max(s_new, axis=-1, keepdims=True)]
        m = functools.reduce(jnp.maximum, m_parts)
        p_new = jnp.exp(s_new - m)
        l = jnp.sum(p_new, axis=-1, keepdims=True)
        o = jnp.dot(p_new.astype(BF16), kfn[:, 0:KV_RANK], preferred_element_type=F32)
        for st in streams:
            w_st = jnp.exp(m_parts[st] - m)
            l = l + w_st * jnp.sum(l_s[st], axis=-1, keepdims=True)
            o = o + w_st * acc_s[st]
        o = (o / l).astype(BF16)
        outs = [jnp.dot(o[h * T:(h + 1) * T, :], wuv_ref[h], preferred_element_type=F32) for h in range(B_HEADS)]
        o_ref[sl] = jnp.concatenate(outs, axis=-1).astype(o_ref.dtype)

    @pl.when(i == 0)
    def _():
        issue(0, 0)

    issue(2 * i + 1, 1)
    wait(0)
    attend(0)

    @pl.when(i + 1 < n)
    def _():
        issue(2 * i + 2, 0)

    wait(1)
    attend(1)


def _mla_decode(page_table, qn, qr, kfn, wuk, wuv, cache_kv, cache_krt, layer):
    B, T, _ = qn.shape
    assert B % 2 == 0
    n_pages = page_table.shape[1]
    L = n_pages * PAGE_SIZE
    scale = (B_NOPE_DIM + B_ROPE_DIM) ** -0.5
    n_streams = math.gcd(n_pages // 2, 4)
    ck = math.gcd(L // (2 * n_streams), 4 * PAGE_SIZE)
    nrows = B_HEADS * T
    stat = pltpu.VMEM((n_streams, nrows, LANES), F32)
    tile = lambda c: pl.BlockSpec((2, T, c), lambda b, pt: (b, 0, 0))
    cst = lambda shape: pl.BlockSpec(shape, lambda b, pt: (0,) * len(shape))
    grid_spec = pltpu.PrefetchScalarGridSpec(
        num_scalar_prefetch=1,
        grid=(B // 2,),
        in_specs=[tile(A_WIDTH), tile(2 * LANES), tile(KFULL),
                  cst((B_HEADS, B_NOPE_DIM, KV_RANK)), cst((B_HEADS, KV_RANK, B_V_DIM)),
                  pl.BlockSpec(memory_space=pl.ANY), pl.BlockSpec(memory_space=pl.ANY)],
        out_specs=tile(B_WIDTH),
        scratch_shapes=[pltpu.VMEM((2, L, KV_RANK), F32), pltpu.VMEM((2, B_ROPE_DIM, L), F32),
                        pltpu.VMEM((n_streams, L // (n_streams * ck), ck, KV_RANK), BF16),
                        pltpu.VMEM((n_streams, L // (n_streams * ck), nrows, ck), F32),
                        stat, stat, pltpu.VMEM((n_streams, nrows, KV_RANK), F32), pltpu.SemaphoreType.DMA((2, 2))],
    )
    return pl.pallas_call(
        functools.partial(_mla_decode_kernel, layer=layer, n_pages=n_pages, T=T, scale=scale, ck=ck,
                          n_streams=n_streams),
        grid_spec=grid_spec,
        out_shape=jax.ShapeDtypeStruct((B, T, B_WIDTH), BF16),
        compiler_params=_cparams(("arbitrary",)),
        name="mla_decode",
    )(page_table, qn, qr, kfn, wuk, wuv, cache_kv, cache_krt)


def _pool_kernel(pc_ref, st_ref, wbd_ref, scale_ref, y_ref, newst_ref, e_s, s2_s, s4_s, s8_s, *, ts, T, pos0):
    L = T * ts
    P = (POOL_PREFIX + 1) * ts
    pc = pc_ref[...].reshape(L, C_WIDTH)
    e_s[P - POOL_PREFIX * ts:P, :] = st_ref[...].reshape(POOL_PREFIX * ts, C_WIDTH)
    e_s[P:P + L, :] = pc
    newst_ref[...] = e_s[P + L - POOL_PREFIX * ts:P + L, :].reshape(newst_ref.shape)
    s2_s[P - 14 * ts:P + L, :] = e_s[P - 14 * ts:P + L, :] + e_s[P - 15 * ts:P + L - ts, :]
    s4_s[P - 12 * ts:P + L, :] = s2_s[P - 12 * ts:P + L, :] + s2_s[P - 14 * ts:P + L - 2 * ts, :]
    s8_s[P - 8 * ts:P + L, :] = s4_s[P - 8 * ts:P + L, :] + s4_s[P - 12 * ts:P + L - 4 * ts, :]
    s16 = s8_s[P:P + L, :] + s8_s[P - 8 * ts:P + L - 8 * ts, :]
    lane = lax.broadcasted_iota(jnp.int32, (L, C_WIDTH), 1)
    grp = _div_pow2(lane, C_GROUP_DIM)
    win = jnp.where(grp == 0, s2_s[P:P + L, :],
                    jnp.where(grp == 1, s4_s[P:P + L, :], jnp.where(grp == 2, s8_s[P:P + L, :], s16)))
    wsize = jnp.where(grp == 0, 2, jnp.where(grp == 1, 4, jnp.where(grp == 2, 8, 16)))
    pos = pos0 + _div_pow2(lax.broadcasted_iota(jnp.int32, (L, C_WIDTH), 0), ts)
    cnt = jnp.minimum(pos + 1, wsize).astype(F32)
    d = win / cnt - pc
    y = jnp.dot(d.astype(BF16), wbd_ref[...], preferred_element_type=F32) * scale_ref[...]
    y_ref[...] = y.astype(y_ref.dtype).reshape(y_ref.shape)


def _pool(tok, pc, pool_prev, wbd, pscale, pos0):
    ts, T = tok.ts, tok.T
    rows = (POOL_PREFIX + 1 + T) * ts
    if tok.time_major:
        grid, blk = (tok.B // tok.bb, 1), tok.spec(C_WIDTH)
    else:
        grid, blk = (tok.B, 1), pl.BlockSpec((None, T, C_WIDTH), lambda i, j: (i, 0, 0))
    return pl.pallas_call(
        functools.partial(_pool_kernel, ts=ts, T=T, pos0=pos0),
        grid=grid,
        in_specs=[blk, tok.state_spec(POOL_PREFIX, C_WIDTH), _const_spec((C_WIDTH, C_WIDTH)),
                  _const_spec((1, C_WIDTH))],
        out_specs=[blk, tok.state_spec(POOL_PREFIX, C_WIDTH)],
        out_shape=[jax.ShapeDtypeStruct(tok.shape(C_WIDTH), BF16),
                   jax.ShapeDtypeStruct(tok.state_shape(POOL_PREFIX, C_WIDTH), F32)],
        scratch_shapes=[pltpu.VMEM((rows, C_WIDTH), F32)] * 4,
        compiler_params=_cparams(("arbitrary", "arbitrary")),
        name="pool",
    )(pc, pool_prev, wbd, pscale)


def _ffn_kernel(x_ref, ya_ref, g_ref, yb_ref, yc_ref, cst_ref, wout_ref, gffn_ref, wup_ref, cw_ref, cb_ref,
                wdn_ref, gfin_ref, xo_ref, cnew_ref, gbuf_s, *, tm, ts, final):
    taps = CONV_WIDTH - 1
    pad = max(SUBLANES, taps * ts)
    it = pl.program_id(1)

    @pl.when(it == 0)
    def _():
        gbuf_s[pad - taps * ts:pad, :] = cst_ref[...].reshape(taps * ts, D_FF)

    x = x_ref[...].reshape(tm, D_MODEL)
    ya = (ya_ref[...].reshape(tm, A_WIDTH) * g_ref[...].reshape(tm, A_WIDTH)).astype(BF16)
    ycat = jnp.concatenate([ya, yb_ref[...].reshape(tm, B_WIDTH), yc_ref[...].reshape(tm, C_WIDTH)], axis=-1)
    x1 = x + jnp.dot(ycat, wout_ref[...], preferred_element_type=F32)
    xn = _rms(x1, gffn_ref[...]).astype(BF16)
    acc = jnp.zeros((tm, D_MODEL), F32)
    assert sum(FF_CHUNKS) == D_FF
    lo = 0
    for width in FF_CHUNKS:
        hi = lo + width
        up = jnp.dot(xn, wup_ref[:, lo:hi], preferred_element_type=F32)
        gate = jnp.dot(xn, wup_ref[:, D_FF + lo:D_FF + hi], preferred_element_type=F32)
        gbuf_s[pad:pad + tm, lo:hi] = gate
        gc = cb_ref[:, lo:hi] + cw_ref[2:3, lo:hi] * gate
        for j in range(taps):
            off = pad - (taps - j) * ts
            gc = gc + cw_ref[j:j + 1, lo:hi] * gbuf_s[off:off + tm, lo:hi]
        hh = (gc * _sigmoid(gc) * up).astype(BF16)
        acc = acc + jnp.dot(hh, wdn_ref[lo:hi, :], preferred_element_type=F32)
        lo = hi
    last = gbuf_s[pad + tm - taps * ts:pad + tm, :]
    cnew_ref[...] = last.reshape(cnew_ref.shape)
    gbuf_s[pad - taps * ts:pad, :] = last
    x2 = x1 + acc
    if final:
        x2 = _rms(x2, gfin_ref[...])
    xo_ref[...] = x2.reshape(xo_ref.shape)


def _ffn(tok, x, ya, g, yb, yc, conv_prev, p, gfin, final):
    tm, ts = tok.tm, tok.ts
    taps = CONV_WIDTH - 1
    pad = max(SUBLANES, taps * ts)
    return pl.pallas_call(
        functools.partial(_ffn_kernel, tm=tm, ts=ts, final=final),
        grid=tok.grid,
        in_specs=[tok.spec(D_MODEL), tok.spec(A_WIDTH), tok.spec(A_WIDTH), tok.spec(B_WIDTH), tok.spec(C_WIDTH),
                  tok.state_spec(taps, D_FF), _const_spec((D_MODEL, D_MODEL)), _const_spec((1, D_MODEL)),
                  _const_spec((D_MODEL, 2 * D_FF)), _const_spec((CONV_WIDTH, D_FF)), _const_spec((1, D_FF)),
                  _const_spec((D_FF, D_MODEL)), _const_spec((1, D_MODEL))],
        out_specs=[tok.spec(D_MODEL), tok.state_spec(taps, D_FF)],
        out_shape=[jax.ShapeDtypeStruct(tok.shape(D_MODEL), F32),
                   jax.ShapeDtypeStruct(tok.state_shape(taps, D_FF), F32)],
        scratch_shapes=[pltpu.VMEM((pad + tm, D_FF), F32)],
        compiler_params=_cparams(("arbitrary", "arbitrary")),
        name="outproj_ffn",
    )(x, ya, g, yb, yc, conv_prev, p["w_out"], p["g_ffn"], p["w_up"], p["conv_w"], p["conv_b"], p["w_down"], gfin)


def _ext_columns():
    qn0 = A_COLS
    qr0 = qn0 + B_HEADS * B_NOPE_DIM
    ckv0 = qr0 + B_HEADS * B_ROPE_DIM
    kr0 = ckv0 + KV_RANK
    pc0 = A_COLS + B_COLS
    cols = list(range(A_COLS)) + list(range(qn0, qr0)) + list(range(ckv0, kr0)) + list(range(pc0, pc0 + C_WIDTH))
    for half in range(2):
        blk = []
        for h in range(B_HEADS):
            blk += [qr0 + h * B_ROPE_DIM + half * HALF_ROPE + i for i in range(HALF_ROPE)]
        blk += [kr0 + half * HALF_ROPE + i for i in range(HALF_ROPE)]
        cols += blk + [-1] * (LANES - len(blk))
    for half in range(2):
        blk = [kr0 + half * HALF_ROPE + i for _ in range(B_HEADS) for i in range(HALF_ROPE)]
        cols += blk + [-1] * (LANES - len(blk))
    assert len(cols) == IN_EXT
    return np.asarray(cols, np.int32)


def _layer_params(l, norm_mix_g, w_in, mu_shift, decay_w0, decay_w2, iclr_a0, iclr_a2, gate_g2, k_k, k_a, r_k,
                  lnx_w, lnx_b, kv_norm_g, w_uk, w_uv, pool_w, pool_scale, w_out, norm_ffn_g, w_up, conv_w,
                  conv_b, w_down):
    cols = _ext_columns()
    w_ext = jnp.where(cols[None, :] >= 0, w_in[l][:, np.maximum(cols, 0)], 0.0).astype(BF16)
    lora_cat = jnp.zeros((LANES, 2 * A_WIDTH), F32)
    lora_cat = lora_cat.at[0:DECAY_LORA, 0:A_WIDTH].set(decay_w2[l])
    lora_cat = lora_cat.at[DECAY_LORA:DECAY_LORA + ICLR_LORA, A_WIDTH:].set(iclr_a2[l])
    head = np.arange(A_WIDTH) // A_HEAD_DIM
    head_ones = jnp.asarray((head[:, None] == head[None, :]).astype(np.float32), BF16)
    wbd = jnp.zeros((C_WIDTH, C_WIDTH), F32)
    for gi in range(C_GROUPS):
        sl = slice(gi * C_GROUP_DIM, (gi + 1) * C_GROUP_DIM)
        wbd = wbd.at[sl, sl].set(pool_w[l, gi])
    row = lambda v: v.reshape(1, -1)
    return dict(
        g_mix=row(norm_mix_g[l]), w_ext=w_ext, kv_g=row(kv_norm_g[l]),
        mu=row(mu_shift[l]), lora_cat=lora_cat.astype(BF16), w0=row(decay_w0[l]), a0=row(iclr_a0[l]),
        g2=gate_g2[l].astype(BF16), k_k=row(k_k[l]), k_a=row(k_a[l]), head_ones=head_ones,
        r_k=r_k[l], lnx_w=lnx_w[l].reshape(A_HEADS, A_HEAD_DIM), lnx_b=lnx_b[l].reshape(A_HEADS, A_HEAD_DIM),
        wuk=jnp.transpose(w_uk[l], (1, 2, 0)).astype(BF16), wuv=jnp.transpose(w_uv[l], (1, 0, 2)).astype(BF16),
        pool_wbd=wbd.astype(BF16), pool_scale=row(pool_scale[l]),
        w_out=w_out[l].astype(BF16), g_ffn=row(norm_ffn_g[l]), w_up=w_up[l].astype(BF16),
        conv_w=conv_w[l], conv_b=row(conv_b[l]), w_down=w_down[l].astype(BF16),
    )


def _rope_tables(pos):
    inv = ROPE_BASE ** (-jnp.arange(0, B_ROPE_DIM, 2, dtype=F32) / B_ROPE_DIM)
    ang = pos.astype(F32)[:, None] * inv[None, :]
    reps = LANES // HALF_ROPE
    return jnp.tile(jnp.cos(ang), (1, reps)), jnp.tile(jnp.sin(ang), (1, reps))


def _to_scan(tok, x):
    assert tok.time_major
    x = x.reshape(tok.T, tok.B, A_HEADS, A_HEAD_DIM)
    return jnp.transpose(x, (2, 0, 3, 1))


def _lane_maps(nb):
    lane_of = np.asarray([[_prompt_lane(b, h, nb) for h in range(A_HEADS)] for b in range(nb)], np.int32)
    lanes_b = np.zeros(nb * A_HEADS, np.int32)
    lanes_h = np.zeros(nb * A_HEADS, np.int32)
    for b in range(nb):
        for h in range(A_HEADS):
            lanes_b[lane_of[b, h]], lanes_h[lane_of[b, h]] = b, h
    return lane_of, lanes_b, lanes_h


def _from_scan(tok, y):
    H, N = A_HEADS, A_HEAD_DIM
    if tok.time_major:
        return jnp.transpose(y, (1, 3, 0, 2)).reshape(tok.T, tok.B, H * N)
    lane_of, _, _ = _lane_maps(tok.B)
    y = y[0][:, :, lane_of]
    return jnp.transpose(y, (2, 0, 3, 1)).reshape(tok.B, tok.T, H * N)


def _pad_lanes(x):
    return jnp.pad(x, [(0, 0)] * (x.ndim - 1) + [(0, LANES - x.shape[-1])])


def _state_to_scan(tok, s):
    if tok.time_major:
        return jnp.transpose(s, (1, 3, 2, 0))
    _, lanes_b, lanes_h = _lane_maps(tok.B)
    return _pad_lanes(jnp.transpose(s, (3, 2, 0, 1))[:, :, lanes_b, lanes_h])[None]


def _state_from_scan(tok, s):
    if tok.time_major:
        return jnp.transpose(s, (3, 0, 2, 1))
    lane_of, _, _ = _lane_maps(tok.B)
    return jnp.transpose(s[0][:, :, lane_of], (2, 3, 1, 0))


def _head_consts(tok, p):
    def lay(v):
        if tok.time_major:
            return jnp.broadcast_to(v[:, :, None], (A_HEADS, A_HEAD_DIM, LANES))
        _, _, lanes_h = _lane_maps(tok.B)
        return _pad_lanes(v.T[:, lanes_h])[None]
    return lay(p["lnx_w"]), lay(p["lnx_b"]), lay(p["r_k"])


def _layer(tok, x, cos, sin, pos0, shift_prev, wkv_prev, pool_prev, conv_prev, p, gfin, final, scan_tc, attend):
    pa, qn, kvlat, pc, qr, kfull, krope = _inproj(tok, x, p["g_mix"], p["w_ext"], p["kv_g"], cos, sin)
    if tok.time_major:
        r, w, k, v, a, b, g, shift_new = _rwkv_prep(tok, pa, shift_prev, p)
        r, w, k, v, a, b = (_to_scan(tok, t) for t in (r, w, k, v, a, b))
    else:
        r, w, k, v, a, b, g, shift_new = _rwkv_prep_scan(pa, shift_prev, p, scan_tc)
    lnw, lnb, rk = _head_consts(tok, p)
    y, s_new = _rwkv_scan(w, a, b, k, v, r, _state_to_scan(tok, wkv_prev), lnw, lnb, rk, scan_tc)
    ya = _from_scan(tok, y)
    yb = attend(qn, qr, kfull)
    yc, pool_new = _pool(tok, pc, pool_prev, p["pool_wbd"], p["pool_scale"], pos0)
    x, conv_new = _ffn(tok, x, ya, g, yb, yc, conv_prev, p, gfin, final)
    return x, (kvlat, krope, _state_from_scan(tok, s_new), shift_new, pool_new, conv_new)


def kernel(x_prompt, x_sample, cache_kv_latent, cache_k_rope, page_table, state_wkv, state_shift, state_pool,
           state_conv, norm_mix_g, w_in, mu_shift, decay_w0, decay_w2, iclr_a0, iclr_a2, gate_g2, k_k, k_a, r_k,
           lnx_w, lnx_b, kv_norm_g, w_uk, w_uv, pool_w, pool_scale, w_out, norm_ffn_g, w_up, conv_w, conv_b,
           w_down, norm_final_g):
    Bp, Tp, _ = x_prompt.shape
    Bs, Ts, _ = x_sample.shape
    past_len = page_table.shape[1] * PAGE_SIZE
    tokp = _Tok(False, Bp, Tp, min(512, Tp))
    toks = _Tok(True, Bs, Ts, min(32, Bs))
    cos_p, sin_p = _rope_tables(jnp.arange(Tp, dtype=jnp.int32))
    pos_s = past_len + jnp.repeat(jnp.arange(Ts, dtype=jnp.int32), toks.bb)
    cos_s, sin_s = _rope_tables(pos_s)
    gfin = norm_final_g.reshape(1, -1)
    tm_first = lambda t: jnp.swapaxes(t, 0, 1)
    cache_krt = jnp.swapaxes(cache_k_rope, 2, 3)

    xp = x_prompt
    xs = tm_first(x_sample)
    zeros = lambda *shape: jnp.zeros(shape, F32)
    outs_p, outs_s = [], []
    for l in range(DEPTH):
        p = _layer_params(l, norm_mix_g, w_in, mu_shift, decay_w0, decay_w2, iclr_a0, iclr_a2, gate_g2, k_k, k_a,
                          r_k, lnx_w, lnx_b, kv_norm_g, w_uk, w_uv, pool_w, pool_scale, w_out, norm_ffn_g, w_up,
                          conv_w, conv_b, w_down)
        final = l == DEPTH - 1

        attend_p = lambda qn, qr, kfull: _mla_prompt(qn, qr, kfull, p["wuk"], p["wuv"], min(256, Tp))
        xp, st_p = _layer(tokp, xp, cos_p, sin_p, 0, zeros(Bp, A_COLS),
                          zeros(Bp, A_HEADS, A_HEAD_DIM, A_HEAD_DIM), zeros(Bp, POOL_PREFIX, C_WIDTH),
                          zeros(Bp, CONV_WIDTH - 1, D_FF), p, gfin, final, min(32, Tp), attend_p)

        def attend_s(qn, qr, kfull):
            yb = _mla_decode(page_table, tm_first(qn), tm_first(qr), tm_first(kfull), p["wuk"], p["wuv"],
                             cache_kv_latent, cache_krt, l)
            return tm_first(yb)
        xs, st_s = _layer(toks, xs, cos_s, sin_s, past_len, state_shift[l][None], state_wkv[l],
                          tm_first(state_pool[l]), tm_first(state_conv[l]), p, gfin, final, Ts, attend_s)
        outs_p.append(st_p)
        outs_s.append(st_s)

    stack = lambda outs, i, f: jnp.stack([f(o[i]) for o in outs], axis=0)
    ident = lambda t: t
    return (xp, tm_first(xs),
            stack(outs_p, 0, ident), stack(outs_p, 1, ident), stack(outs_p, 2, ident),
            stack(outs_p, 3, ident), stack(outs_p, 4, ident), stack(outs_p, 5, ident),
            stack(outs_s, 0, tm_first), stack(outs_s, 1, tm_first), stack(outs_s, 2, ident),
            stack(outs_s, 3, lambda t: t[0]), stack(outs_s, 4, tm_first), stack(outs_s, 5, tm_first))
```

```python
import functools
import math

import jax
import jax.numpy as jnp
import numpy as np
from jax import lax
from jax.experimental import pallas as pl
from jax.experimental.pallas import tpu as pltpu

F32 = jnp.float32
BF16 = jnp.bfloat16

D_MODEL = 1024
DEPTH = 2
PAGE_SIZE = 128
A_HEADS = 6
A_HEAD_DIM = 64
A_WIDTH = A_HEADS * A_HEAD_DIM
DECAY_LORA = 64
ICLR_LORA = 64
GATE_LORA = 128
A_COLS = 3 * A_WIDTH + DECAY_LORA + ICLR_LORA + GATE_LORA
B_HEADS = 6
B_NOPE_DIM = 64
B_ROPE_DIM = 32
B_V_DIM = 64
B_WIDTH = B_HEADS * B_V_DIM
KV_RANK = 256
ROPE_BASE = 10000.0
B_COLS = B_HEADS * (B_NOPE_DIM + B_ROPE_DIM) + KV_RANK + B_ROPE_DIM
C_GROUPS = 4
C_GROUP_DIM = 64
C_WIDTH = C_GROUPS * C_GROUP_DIM
POOL_WINDOWS = (2, 4, 8, 16)
POOL_PREFIX = 15
D_FF = 2816
CONV_WIDTH = 3
NORM_EPS = 1e-6
GN_EPS = A_HEAD_DIM * 1e-5

LANES = 128
SUBLANES = 8
HALF_ROPE = B_ROPE_DIM // 2
IN_EXT = A_COLS + B_HEADS * B_NOPE_DIM + KV_RANK + C_WIDTH + 4 * LANES
OFF_QN = A_COLS
OFF_CKV = OFF_QN + B_HEADS * B_NOPE_DIM
OFF_PC = OFF_CKV + KV_RANK
OFF_ROPE = OFF_PC + C_WIDTH
KFULL = KV_RANK + 2 * LANES
MXU_TILE = 256
FF_CHUNKS = (6 * MXU_TILE, 5 * MXU_TILE)
VMEM_LIMIT = 56 * 1024 * 1024


def _cparams(sem):
    return pltpu.CompilerParams(dimension_semantics=sem, vmem_limit_bytes=VMEM_LIMIT)


def _const_spec(shape):
    nd = len(shape)
    return pl.BlockSpec(shape, lambda *_: (0,) * nd, pipeline_mode=pl.Buffered(1))


class _Tok:
    def __init__(self, time_major, B, T, tile):
        self.time_major, self.B, self.T = time_major, B, T
        if time_major:
            self.bb = tile
            self.tm, self.ts = T * tile, tile
            self.grid = (B // tile, 1)
        else:
            self.tm, self.ts = tile, 1
            self.grid = (B, T // tile)
        self.n = B * T

    def shape(self, C):
        return (self.T, self.B, C) if self.time_major else (self.B, self.T, C)

    def spec(self, C):
        if self.time_major:
            return pl.BlockSpec((self.T, self.bb, C), lambda i, j: (0, i, 0))
        return pl.BlockSpec((None, self.tm, C), lambda i, j: (i, j, 0))

    def state_shape(self, steps, C):
        return (steps, self.B, C) if self.time_major else (self.B, steps, C)

    def state_spec(self, steps, C):
        if self.time_major:
            return pl.BlockSpec((steps, self.bb, C), lambda i, j: (0, i, 0))
        return pl.BlockSpec((None, steps, C), lambda i, j: (i, 0, 0))


def _rms(x, g):
    return x * lax.rsqrt(jnp.mean(x * x, axis=-1, keepdims=True) + NORM_EPS) * g


def _sigmoid(x):
    return 1.0 / (1.0 + jnp.exp(-x))


def _div_pow2(x, d):
    assert d & (d - 1) == 0
    return lax.shift_right_logical(x, int(math.log2(d)))


def _mod_pow2(x, d):
    assert d & (d - 1) == 0
    return lax.bitwise_and(x, d - 1)


def _inproj_kernel(x_ref, g_ref, w_ref, kvg_ref, cos_ref, sin_ref,
                   pa_ref, qn_ref, kvlat_ref, pc_ref, qr_ref, kfull_ref, krope_ref, *, tm):
    x = x_ref[...].reshape(tm, D_MODEL)
    xb = _rms(x, g_ref[...]).astype(BF16)

    def proj(lo, hi):
        return jnp.dot(xb, w_ref[:, lo:hi], preferred_element_type=F32)

    assert OFF_CKV % MXU_TILE == 0 and OFF_ROPE % MXU_TILE == 0 and IN_EXT % MXU_TILE == 0
    head = proj(0, OFF_CKV)
    pa_ref[...] = head[:, 0:A_COLS].reshape(pa_ref.shape)
    qn_ref[...] = head[:, OFF_QN:OFF_CKV].astype(BF16).reshape(qn_ref.shape)
    mid = proj(OFF_CKV, OFF_ROPE)
    ckv = _rms(mid[:, 0:KV_RANK], kvg_ref[...])
    kvlat_ref[...] = ckv.reshape(kvlat_ref.shape)
    pc_ref[...] = mid[:, KV_RANK:KV_RANK + C_WIDTH].reshape(pc_ref.shape)
    rr = proj(OFF_ROPE, IN_EXT)
    cos = cos_ref[...]
    sin = sin_ref[...]
    q1, q2 = rr[:, 0:LANES], rr[:, LANES:2 * LANES]
    k1, k2 = rr[:, 2 * LANES:3 * LANES], rr[:, 3 * LANES:4 * LANES]
    o1 = q1 * cos - q2 * sin
    o2 = q1 * sin + q2 * cos
    qr = jnp.concatenate([o1, o2], axis=-1).astype(BF16)
    qr_ref[...] = qr.reshape(qr_ref.shape)
    kf = jnp.concatenate([ckv, k1 * cos - k2 * sin, k1 * sin + k2 * cos], axis=-1).astype(BF16)
    kfull_ref[...] = kf.reshape(kfull_ref.shape)
    kr_lo = B_HEADS * HALF_ROPE
    krope = jnp.concatenate([o1[:, kr_lo:kr_lo + HALF_ROPE], o2[:, kr_lo:kr_lo + HALF_ROPE]], axis=-1)
    krope_ref[...] = krope.reshape(krope_ref.shape)


def _inproj(tok, x, g, w_ext, kvg, cos, sin):
    tm = tok.tm
    n_tab = cos.shape[0] // tm
    tab_spec = pl.BlockSpec((tm, LANES), lambda i, j: (j % n_tab, 0))
    widths = (A_COLS, A_WIDTH, KV_RANK, C_WIDTH, 2 * LANES, KFULL, B_ROPE_DIM)
    dtypes = (F32, BF16, F32, F32, BF16, BF16, F32)
    return pl.pallas_call(
        functools.partial(_inproj_kernel, tm=tm),
        grid=tok.grid,
        in_specs=[tok.spec(D_MODEL), _const_spec((1, D_MODEL)), _const_spec((D_MODEL, IN_EXT)),
                  _const_spec((1, KV_RANK)), tab_spec, tab_spec],
        out_specs=[tok.spec(c) for c in widths],
        out_shape=[jax.ShapeDtypeStruct(tok.shape(c), d) for c, d in zip(widths, dtypes)],
        compiler_params=_cparams(("arbitrary", "arbitrary")),
        name="inproj",
    )(x, g, w_ext, kvg, cos, sin)


def _rwkv_token_math(pa, prev, mu_ref, wcat_ref, w0_ref, a0_ref, g2_ref, kk_ref, ka_ref, ones_ref):
    s = pa + (prev - pa) * mu_ref[...]
    W = A_WIDTH
    r, k, v = s[:, 0:W], s[:, W:2 * W], s[:, 2 * W:3 * W]
    lo_in = s[:, 3 * W:3 * W + LANES]
    gd = s[:, 3 * W + LANES:3 * W + 2 * LANES]
    lane = lax.broadcasted_iota(jnp.int32, lo_in.shape, 1)
    lo_in = jnp.where(lane < DECAY_LORA, jnp.tanh(lo_in), lo_in)
    lo = jnp.dot(lo_in.astype(BF16), wcat_ref[...], preferred_element_type=F32)
    z = -(w0_ref[...] + lo[:, 0:W])
    softplus = jnp.maximum(z, 0.0) + jnp.log(1.0 + jnp.exp(-jnp.abs(z)))
    decay = jnp.exp(-jnp.exp(-softplus - 0.5))
    a = _sigmoid(a0_ref[...] + lo[:, W:2 * W])
    g = jnp.dot(_sigmoid(gd).astype(BF16), g2_ref[...], preferred_element_type=F32)
    kk = k * kk_ref[...]
    kk2 = kk * kk
    hi = kk2.astype(BF16)
    lo2 = (kk2 - hi.astype(F32)).astype(BF16)
    ones = ones_ref[...]
    ss = (jnp.dot(hi, ones, preferred_element_type=F32) + jnp.dot(lo2, ones, preferred_element_type=F32))
    kk = kk * lax.rsqrt(jnp.maximum(ss, 1e-24))
    kmod = k * (1.0 + (a - 1.0) * ka_ref[...])
    return r, decay, kmod, v, -kk, kk * a, g


def _rwkv_prep_kernel(pa_ref, st_ref, mu_ref, wcat_ref, w0_ref, a0_ref, g2_ref, kk_ref, ka_ref, ones_ref,
                      r_ref, w_ref, k_ref, v_ref, a_ref, b_ref, g_ref, newst_ref, buf_ref, *, tm, ts):
    pad = max(SUBLANES, ts)
    it = pl.program_id(1)

    @pl.when(it == 0)
    def _():
        buf_ref[pad - ts:pad, :] = st_ref[...].reshape(ts, A_COLS)

    pa = pa_ref[...].reshape(tm, A_COLS)
    buf_ref[pad:pad + tm, :] = pa
    prev = buf_ref[pad - ts:pad - ts + tm, :]
    last = pa[tm - ts:tm, :]
    buf_ref[pad - ts:pad, :] = last
    newst_ref[...] = last.reshape(newst_ref.shape)
    vals = _rwkv_token_math(pa, prev, mu_ref, wcat_ref, w0_ref, a0_ref, g2_ref, kk_ref, ka_ref, ones_ref)
    for ref, val in zip((r_ref, w_ref, k_ref, v_ref, a_ref, b_ref, g_ref), vals):
        ref[...] = val.reshape(ref.shape)


def _prompt_lane(b, h, nb):
    return (h % 2) * (A_HEADS // 2) * nb + (h // 2) * nb + b


def _rwkv_prep_scan_kernel(pa_ref, st_ref, mu_ref, wcat_ref, w0_ref, a0_ref, g2_ref, kk_ref, ka_ref, ones_ref,
                           r_ref, w_ref, k_ref, v_ref, a_ref, b_ref, g_ref, newst_ref, buf_ref, q_ref, *, nb, tt):
    it = pl.program_id(0)

    @pl.when(it == 0)
    def _():
        buf_ref[0:nb, :] = st_ref[...]

    for t in range(tt):
        buf_ref[(t + 1) * nb:(t + 2) * nb, :] = pa_ref[:, t, :]
    pa = buf_ref[nb:(tt + 1) * nb, :]
    prev = buf_ref[0:tt * nb, :]
    vals = _rwkv_token_math(pa, prev, mu_ref, wcat_ref, w0_ref, a0_ref, g2_ref, kk_ref, ka_ref, ones_ref)
    last = buf_ref[tt * nb:(tt + 1) * nb, :]
    buf_ref[0:nb, :] = last
    newst_ref[...] = last
    g = vals[6]
    for t in range(tt):
        g_ref[:, t, :] = g[t * nb:(t + 1) * nb, :]
    for qi in range(6):
        q_ref[qi] = vals[qi]
    outs = (r_ref, w_ref, k_ref, v_ref, a_ref, b_ref)
    half = A_HEADS // 2
    win = half * nb
    N = A_HEAD_DIM

    def pair(pi, carry):
        for qi, o_ref in enumerate(outs):
            rows = q_ref[qi, pl.ds(pl.multiple_of(pi * 2 * nb, 2 * nb), 2 * nb), :]
            pieces = [rows[s * nb:(s + 1) * nb, hp * LANES:(hp + 1) * LANES] for s in range(2) for hp in range(half)]
            m = jnp.concatenate(pieces + [jnp.zeros((LANES - 2 * win, LANES), F32)], axis=0)
            mt = m.T
            unused = jnp.full((N, LANES - 2 * win), 1.0 if o_ref is w_ref else 0.0, F32)
            for s in range(2):
                tile = jnp.concatenate([mt[0:N, s * win:(s + 1) * win], mt[N:2 * N, s * win:(s + 1) * win], unused],
                                       axis=1)
                o_ref[0, pi * 2 + s] = tile
        return carry

    lax.fori_loop(0, tt // 2, pair, 0, unroll=8)


def _rwkv_prep_scan(pa, shift_prev, p, tt):
    nb, T, _ = pa.shape
    assert 2 * (A_HEADS // 2) * nb <= LANES and tt % 2 == 0
    vec = _const_spec((1, A_WIDTH))
    seq = pl.BlockSpec((1, tt, A_HEAD_DIM, LANES), lambda i: (0, i, 0, 0))
    st = pl.BlockSpec((nb, A_COLS), lambda i: (0, 0))
    return pl.pallas_call(
        functools.partial(_rwkv_prep_scan_kernel, nb=nb, tt=tt),
        grid=(T // tt,),
        in_specs=[pl.BlockSpec((nb, tt, A_COLS), lambda i: (0, i, 0)), st, _const_spec((1, A_COLS)),
                  _const_spec((LANES, 2 * A_WIDTH)), vec, vec, _const_spec((GATE_LORA, A_WIDTH)), vec, vec,
                  _const_spec((A_WIDTH, A_WIDTH))],
        out_specs=[seq] * 6 + [pl.BlockSpec((nb, tt, A_WIDTH), lambda i: (0, i, 0)), st],
        out_shape=[jax.ShapeDtypeStruct((1, T, A_HEAD_DIM, LANES), F32)] * 6
        + [jax.ShapeDtypeStruct((nb, T, A_WIDTH), F32), jax.ShapeDtypeStruct((nb, A_COLS), F32)],
        scratch_shapes=[pltpu.VMEM(((tt + 1) * nb, A_COLS), F32), pltpu.VMEM((6, tt * nb, A_WIDTH), F32)],
        compiler_params=_cparams(("arbitrary",)),
        name="rwkv_prep_scan",
    )(pa, shift_prev, p["mu"], p["lora_cat"], p["w0"], p["a0"], p["g2"], p["k_k"], p["k_a"], p["head_ones"])


def _rwkv_prep(tok, pa, shift_prev, p):
    tm, ts = tok.tm, tok.ts
    pad = max(SUBLANES, ts)
    vec = _const_spec((1, A_WIDTH))
    return pl.pallas_call(
        functools.partial(_rwkv_prep_kernel, tm=tm, ts=ts),
        grid=tok.grid,
        in_specs=[tok.spec(A_COLS), tok.state_spec(1, A_COLS), _const_spec((1, A_COLS)),
                  _const_spec((LANES, 2 * A_WIDTH)), vec, vec, _const_spec((GATE_LORA, A_WIDTH)), vec, vec,
                  _const_spec((A_WIDTH, A_WIDTH))],
        out_specs=[tok.spec(A_WIDTH)] * 7 + [tok.state_spec(1, A_COLS)],
        out_shape=[jax.ShapeDtypeStruct(tok.shape(A_WIDTH), F32)] * 7
        + [jax.ShapeDtypeStruct(tok.state_shape(1, A_COLS), F32)],
        scratch_shapes=[pltpu.VMEM((pad + tm, A_COLS), F32)],
        compiler_params=_cparams(("arbitrary", "arbitrary")),
        name="rwkv_prep",
    )(pa, shift_prev, p["mu"], p["lora_cat"], p["w0"], p["a0"], p["g2"], p["k_k"], p["k_a"], p["head_ones"])


def _rwkv_scan_kernel(w_ref, a_ref, b_ref, k_ref, v_ref, r_ref, s0_ref, lnw_ref, lnb_ref, rk_ref,
                      y_ref, sT_ref, S_ref, op_ref, *, tc):
    N = A_HEAD_DIM
    it = pl.program_id(1)

    @pl.when(it == 0)
    def _():
        S_ref[...] = s0_ref[0]

    def step(t, p_prev):
        w = w_ref[0, t]
        r = r_ref[0, t]
        k = k_ref[0, t]
        v = v_ref[0, t]
        p_new = p_prev * w
        inv = 1.0 / p_new
        rt = r * p_new
        bt = b_ref[0, t] * inv
        kt = k * inv
        op_ref[0] = a_ref[0, t] * p_prev
        op_ref[1] = rt
        op_ref[2] = bt
        op_ref[3] = kt
        sa = [jnp.zeros((N, LANES), F32) for _ in range(2)]
        yp = [jnp.zeros((N, LANES), F32) for _ in range(2)]
        for j in range(N):
            Zj = S_ref[j]
            sa[j % 2] = sa[j % 2] + Zj * op_ref[0, pl.ds(j, 1), :]
            yp[j % 2] = yp[j % 2] + Zj * op_ref[1, pl.ds(j, 1), :]
        sa = sa[0] + sa[1]
        yp = yp[0] + yp[1]
        for j in range(N):
            S_ref[j] = S_ref[j] + sa * op_ref[2, pl.ds(j, 1), :] + v * op_ref[3, pl.ds(j, 1), :]
        br = jnp.sum(bt * rt, axis=0, keepdims=True)
        kr = jnp.sum(kt * rt, axis=0, keepdims=True)
        y = yp + sa * br + v * kr
        mu = jnp.mean(y, axis=0, keepdims=True)
        d = y - mu
        var = jnp.mean(d * d, axis=0, keepdims=True)
        yn = d * lax.rsqrt(var + GN_EPS) * lnw_ref[0] + lnb_ref[0]
        bonus = jnp.sum(r * k * rk_ref[0], axis=0, keepdims=True)
        y_ref[0, t] = yn + bonus * v
        return p_new

    op_ref[0] = lax.fori_loop(0, tc, step, jnp.ones((N, LANES), F32))
    for j in range(N):
        S_ref[j] = S_ref[j] * op_ref[0, pl.ds(j, 1), :]

    @pl.when(it == pl.num_programs(1) - 1)
    def _():
        sT_ref[0] = S_ref[...]


def _rwkv_scan(w, a, b, k, v, r, s0, lnw, lnb, rk, tc):
    G, T, N, _ = w.shape
    seq = pl.BlockSpec((1, tc, N, LANES), lambda g, i: (g, i, 0, 0))
    st = pl.BlockSpec((1, N, N, LANES), lambda g, i: (g, 0, 0, 0))
    cst = pl.BlockSpec((1, N, LANES), lambda g, i: (g, 0, 0))
    return pl.pallas_call(
        functools.partial(_rwkv_scan_kernel, tc=tc),
        grid=(G, T // tc),
        in_specs=[seq] * 6 + [st, cst, cst, cst],
        out_specs=[seq, st],
        out_shape=[jax.ShapeDtypeStruct((G, T, N, LANES), F32), jax.ShapeDtypeStruct((G, N, N, LANES), F32)],
        scratch_shapes=[pltpu.VMEM((N, N, LANES), F32), pltpu.VMEM((4, N, LANES), F32)],
        compiler_params=_cparams(("arbitrary", "arbitrary")),
        name="rwkv_scan",
    )(w, a, b, k, v, r, s0, lnw, lnb, rk)


_NT = (((1,), (1,)), ((), ()))


def _build_queries(qn, qr, wuk_ref, scale, rows):
    lane = lax.broadcasted_iota(jnp.int32, (rows, 2 * LANES), 1)
    slot = _div_pow2(_mod_pow2(lane, LANES), HALF_ROPE)
    qrf = qr.astype(F32) * scale
    parts = []
    for h in range(B_HEADS):
        ql = jnp.dot(qn[:, h * B_NOPE_DIM:(h + 1) * B_NOPE_DIM], wuk_ref[h], preferred_element_type=F32) * scale
        parts.append(jnp.concatenate([ql, jnp.where(slot == h, qrf, 0.0)], axis=-1).astype(BF16))
    return parts


def _mla_prompt_kernel(qn_ref, qr_ref, kf_ref, wuk_ref, wuv_ref, o_ref, q_s, s_s, p_s, m_s, l_s, a_s, acc_s,
                       *, tq, scale, row_block):
    qi = pl.program_id(1)
    parts = _build_queries(qn_ref[...], qr_ref[...], wuk_ref, scale, tq)
    for h in range(B_HEADS):
        q_s[h * tq:(h + 1) * tq, :] = parts[h]
    rows = B_HEADS * tq
    m_s[...] = jnp.full((rows, LANES), -jnp.inf, F32)
    l_s[...] = jnp.zeros((rows, LANES), F32)
    acc_s[...] = jnp.zeros((rows, KV_RANK), F32)

    def keys(kc):
        return pl.ds(pl.multiple_of(kc * tq, tq), tq)

    def scores(kc, slot):
        s_s[slot] = lax.dot_general(q_s[...], kf_ref[keys(kc), :], _NT, preferred_element_type=F32)

    def lane_tiles(s):
        return [s[:, c * LANES:(c + 1) * LANES] for c in range(tq // LANES)]

    def softmax_pv(kc, slot, masked):
        blocks = [slice(rb * row_block, (rb + 1) * row_block) for rb in range(rows // row_block)]
        for rb, blk in enumerate(blocks):
            s = s_s[slot, blk, :]
            if masked:
                qt = _mod_pow2(rb * row_block + lax.broadcasted_iota(jnp.int32, (row_block, tq), 0), tq)
                kpos = lax.broadcasted_iota(jnp.int32, (row_block, tq), 1)
                s = jnp.where(kpos <= qt, s, -jnp.inf)
                s_s[slot, blk, :] = s
            mx = functools.reduce(jnp.maximum, lane_tiles(s))
            m_prev = m_s[blk, :]
            m_new = jnp.maximum(m_prev, jnp.max(mx, axis=-1, keepdims=True))
            a_s[blk, :] = jnp.exp2(m_prev - m_new)
            m_s[blk, :] = m_new
        for blk in blocks:
            m_new = m_s[blk, :]
            ps = [jnp.exp2(t - m_new) for t in lane_tiles(s_s[slot, blk, :])]
            l_s[blk, :] = a_s[blk, :] * l_s[blk, :] + functools.reduce(jnp.add, ps)
            p_s[blk, :] = jnp.concatenate(ps, axis=-1).astype(BF16)
        pv = jnp.dot(p_s[...], kf_ref[keys(kc), 0:KV_RANK], preferred_element_type=F32)
        alpha = a_s[...]
        acc_s[...] = acc_s[...] * jnp.concatenate([alpha] * (KV_RANK // LANES), axis=-1) + pv

    scores(0, 0)

    def pair(i, carry):
        scores(2 * i + 1, 1)
        softmax_pv(2 * i, 0, False)
        scores(2 * i + 2, 0)
        softmax_pv(2 * i + 1, 1, False)
        return carry

    lax.fori_loop(0, qi // 2, pair, 0)

    @pl.when(qi % 2 == 1)
    def _():
        scores(qi, 1)
        softmax_pv(qi - 1, 0, False)
        softmax_pv(qi, 1, True)

    @pl.when(qi % 2 == 0)
    def _():
        softmax_pv(qi, 0, True)

    o = (acc_s[...] * (1.0 / jnp.sum(l_s[...], axis=-1, keepdims=True))).astype(BF16)
    outs = [jnp.dot(o[h * tq:(h + 1) * tq, :], wuv_ref[h], preferred_element_type=F32) for h in range(B_HEADS)]
    o_ref[...] = jnp.concatenate(outs, axis=-1).astype(o_ref.dtype)


def _mla_prompt(qn, qr, kfull, wuk, wuv, tq):
    B, T, _ = qn.shape
    scale = (B_NOPE_DIM + B_ROPE_DIM) ** -0.5 * math.log2(math.e)
    rows = B_HEADS * tq
    tile = lambda c: pl.BlockSpec((None, tq, c), lambda b, i: (b, i, 0))
    stat = pltpu.VMEM((rows, LANES), F32)
    return pl.pallas_call(
        functools.partial(_mla_prompt_kernel, tq=tq, scale=scale, row_block=min(128, tq)),
        grid=(B, T // tq),
        in_specs=[tile(A_WIDTH), tile(2 * LANES), pl.BlockSpec((None, T, KFULL), lambda b, i: (b, 0, 0)),
                  _const_spec((B_HEADS, B_NOPE_DIM, KV_RANK)), _const_spec((B_HEADS, KV_RANK, B_V_DIM))],
        out_specs=tile(B_WIDTH),
        out_shape=jax.ShapeDtypeStruct((B, T, B_WIDTH), BF16),
        scratch_shapes=[pltpu.VMEM((rows, KFULL), BF16), pltpu.VMEM((2, rows, tq), F32),
                        pltpu.VMEM((rows, tq), BF16), stat, stat, stat, pltpu.VMEM((rows, KV_RANK), F32)],
        compiler_params=_cparams(("arbitrary", "arbitrary")),
        name="mla_prompt",
    )(qn, qr, kfull, wuk, wuv)


def _mla_decode_kernel(pt_ref, qn_ref, qr_ref, kfn_ref, wuk_ref, wuv_ref, ckv_hbm, krt_hbm,
                       o_ref, ckv_buf, krt_buf, kv16, s_s, m_s, l_s, acc_s, sem,
                       *, layer, n_pages, T, scale, ck, n_streams):
    i = pl.program_id(0)
    n = pl.num_programs(0)
    L = n_pages * PAGE_SIZE

    def page_copies(sl, p, page):
        rows = pl.ds(pl.multiple_of(p * PAGE_SIZE, PAGE_SIZE), PAGE_SIZE)
        return (pltpu.make_async_copy(ckv_hbm.at[layer, page], ckv_buf.at[sl, rows], sem.at[0, sl]),
                pltpu.make_async_copy(krt_hbm.at[layer, page], krt_buf.at[sl, :, rows], sem.at[1, sl]))

    def issue(bb, sl):
        def body(p, carry):
            for thread, cp in enumerate(page_copies(sl, p, pt_ref[bb, p])):
                cp.start(priority=thread)
            return carry
        lax.fori_loop(0, n_pages, body, 0, unroll=math.gcd(n_pages, 8))

    def wait(sl):
        for p in range(n_pages):
            for cp in page_copies(sl, p, 0):
                cp.wait()

    def attend(sl):
        qr = qr_ref[sl]
        kfn = kfn_ref[sl]
        parts = _build_queries(qn_ref[sl], qr, wuk_ref, scale, T)
        qfull = jnp.concatenate(parts, axis=0)
        qlat = qfull[:, 0:KV_RANK]
        qrs = (qr.astype(F32) * scale)
        qrope = jnp.concatenate(
            [jnp.concatenate([qrs[:, h * HALF_ROPE:(h + 1) * HALF_ROPE],
                              qrs[:, LANES + h * HALF_ROPE:LANES + (h + 1) * HALF_ROPE]], axis=-1)
             for h in range(B_HEADS)], axis=0).astype(BF16)
        nrows = B_HEADS * T
        per_stream = L // n_streams
        n_ck = per_stream // ck
        m_s[...] = jnp.full(m_s.shape, -jnp.inf, F32)
        l_s[...] = jnp.zeros(l_s.shape, F32)
        acc_s[...] = jnp.zeros(acc_s.shape, F32)

        def scores(st, kc, slot):
            keys = slice(st * per_stream + kc * ck, st * per_stream + (kc + 1) * ck)
            kv16[st, slot] = ckv_buf[sl, keys, :].astype(BF16)
            s_s[st, slot] = (lax.dot_general(qlat, kv16[st, slot], _NT, preferred_element_type=F32)
                             + jnp.dot(qrope, krt_buf[sl, :, keys].astype(BF16), preferred_element_type=F32))

        def softmax_pv(st, slot):
            s = s_s[st, slot]
            tiles = [s[:, c * LANES:(c + 1) * LANES] for c in range(ck // LANES)]
            m_prev = m_s[st]
            m_new = jnp.maximum(m_prev, jnp.max(functools.reduce(jnp.maximum, tiles), axis=-1, keepdims=True))
            alpha = jnp.exp(m_prev - m_new)
            ps = [jnp.exp(t - m_new) for t in tiles]
            l_s[st] = alpha * l_s[st] + functools.reduce(jnp.add, ps)
            m_s[st] = m_new
            pv = jnp.dot(jnp.concatenate(ps, axis=-1).astype(BF16), kv16[st, slot], preferred_element_type=F32)
            acc_s[st] = acc_s[st] * jnp.concatenate([alpha] * (KV_RANK // LANES), axis=-1) + pv

        streams = range(n_streams)
        for st in streams:
            scores(st, 0, 0)
        for kc in range(n_ck):
            if kc + 1 < n_ck:
                for st in streams:
                    scores(st, kc + 1, kc + 1)
            for st in streams:
                softmax_pv(st, kc)
        s_new = lax.dot_general(qfull, kfn, _NT, preferred_element_type=F32)
        qt = _mod_pow2(lax.broadcasted_iota(jnp.int32, (nrows, T), 0), T)
        kt = lax.broadcasted_iota(jnp.int32, (nrows, T), 1)
        s_new = jnp.where(kt <= qt, s_new, -jnp.inf)
        m_parts = [m_s[st][:, 0:1] for st in streams] + [jnp.max(s_new, axis=-1, keepdims=True)]
        m = functools.reduce(jnp.maximum, m_parts)
        p_new = jnp.exp(s_new - m)
        l = jnp.sum(p_new, axis=-1, keepdims=True)
        o = jnp.dot(p_new.astype(BF16), kfn[:, 0:KV_RANK], preferred_element_type=F32)
        for st in streams:
            w_st = jnp.exp(m_parts[st] - m)
            l = l + w_st * jnp.sum(l_s[st], axis=-1, keepdims=True)
            o = o + w_st * acc_s[st]
        o = (o / l).astype(BF16)
        outs = [jnp.dot(o[h * T:(h + 1) * T, :], wuv_ref[h], preferred_element_type=F32) for h in range(B_HEADS)]
        o_ref[sl] = jnp.concatenate(outs, axis=-1).astype(o_ref.dtype)

    @pl.when(i == 0)
    def _():
        issue(0, 0)

    issue(2 * i + 1, 1)
    wait(0)
    attend(0)

    @pl.when(i + 1 < n)
    def _():
        issue(2 * i + 2, 0)

    wait(1)
    attend(1)


def _mla_decode(page_table, qn, qr, kfn, wuk, wuv, cache_kv, cache_krt, layer):
    B, T, _ = qn.shape
    assert B % 2 == 0
    n_pages = page_table.shape[1]
    L = n_pages * PAGE_SIZE
    scale = (B_NOPE_DIM + B_ROPE_DIM) ** -0.5
    n_streams = math.gcd(n_pages // 2, 4)
    ck = math.gcd(L // (2 * n_streams), 4 * PAGE_SIZE)
    nrows = B_HEADS * T
    stat = pltpu.VMEM((n_streams, nrows, LANES), F32)
    tile = lambda c: pl.BlockSpec((2, T, c), lambda b, pt: (b, 0, 0))
    cst = lambda shape: pl.BlockSpec(shape, lambda b, pt: (0,) * len(shape))
    grid_spec = pltpu.PrefetchScalarGridSpec(
        num_scalar_prefetch=1,
        grid=(B // 2,),
        in_specs=[tile(A_WIDTH), tile(2 * LANES), tile(KFULL),
                  cst((B_HEADS, B_NOPE_DIM, KV_RANK)), cst((B_HEADS, KV_RANK, B_V_DIM)),
                  pl.BlockSpec(memory_space=pl.ANY), pl.BlockSpec(memory_space=pl.ANY)],
        out_specs=tile(B_WIDTH),
        scratch_shapes=[pltpu.VMEM((2, L, KV_RANK), F32), pltpu.VMEM((2, B_ROPE_DIM, L), F32),
                        pltpu.VMEM((n_streams, L // (n_streams * ck), ck, KV_RANK), BF16),
                        pltpu.VMEM((n_streams, L // (n_streams * ck), nrows, ck), F32),
                        stat, stat, pltpu.VMEM((n_streams, nrows, KV_RANK), F32), pltpu.SemaphoreType.DMA((2, 2))],
    )
    return pl.pallas_call(
        functools.partial(_mla_decode_kernel, layer=layer, n_pages=n_pages, T=T, scale=scale, ck=ck,
                          n_streams=n_streams),
        grid_spec=grid_spec,
        out_shape=jax.ShapeDtypeStruct((B, T, B_WIDTH), BF16),
        compiler_params=_cparams(("arbitrary",)),
        name="mla_decode",
    )(page_table, qn, qr, kfn, wuk, wuv, cache_kv, cache_krt)


def _pool_kernel(pc_ref, st_ref, wbd_ref, scale_ref, y_ref, newst_ref, e_s, s2_s, s4_s, s8_s, *, ts, T, pos0):
    L = T * ts
    P = (POOL_PREFIX + 1) * ts
    pc = pc_ref[...].reshape(L, C_WIDTH)
    e_s[P - POOL_PREFIX * ts:P, :] = st_ref[...].reshape(POOL_PREFIX * ts, C_WIDTH)
    e_s[P:P + L, :] = pc
    newst_ref[...] = e_s[P + L - POOL_PREFIX * ts:P + L, :].reshape(newst_ref.shape)
    s2_s[P - 14 * ts:P + L, :] = e_s[P - 14 * ts:P + L, :] + e_s[P - 15 * ts:P + L - ts, :]
    s4_s[P - 12 * ts:P + L, :] = s2_s[P - 12 * ts:P + L, :] + s2_s[P - 14 * ts:P + L - 2 * ts, :]
    s8_s[P - 8 * ts:P + L, :] = s4_s[P - 8 * ts:P + L, :] + s4_s[P - 12 * ts:P + L - 4 * ts, :]
    s16 = s8_s[P:P + L, :] + s8_s[P - 8 * ts:P + L - 8 * ts, :]
    lane = lax.broadcasted_iota(jnp.int32, (L, C_WIDTH), 1)
    grp = _div_pow2(lane, C_GROUP_DIM)
    win = jnp.where(grp == 0, s2_s[P:P + L, :],
                    jnp.where(grp == 1, s4_s[P:P + L, :], jnp.where(grp == 2, s8_s[P:P + L, :], s16)))
    wsize = jnp.where(grp == 0, 2, jnp.where(grp == 1, 4, jnp.where(grp == 2, 8, 16)))
    pos = pos0 + _div_pow2(lax.broadcasted_iota(jnp.int32, (L, C_WIDTH), 0), ts)
    cnt = jnp.minimum(pos + 1, wsize).astype(F32)
    d = win / cnt - pc
    y = jnp.dot(d.astype(BF16), wbd_ref[...], preferred_element_type=F32) * scale_ref[...]
    y_ref[...] = y.astype(y_ref.dtype).reshape(y_ref.shape)


def _pool(tok, pc, pool_prev, wbd, pscale, pos0):
    ts, T = tok.ts, tok.T
    rows = (POOL_PREFIX + 1 + T) * ts
    if tok.time_major:
        grid, blk = (tok.B // tok.bb, 1), tok.spec(C_WIDTH)
    else:
        grid, blk = (tok.B, 1), pl.BlockSpec((None, T, C_WIDTH), lambda i, j: (i, 0, 0))
    return pl.pallas_call(
        functools.partial(_pool_kernel, ts=ts, T=T, pos0=pos0),
        grid=grid,
        in_specs=[blk, tok.state_spec(POOL_PREFIX, C_WIDTH), _const_spec((C_WIDTH, C_WIDTH)),
                  _const_spec((1, C_WIDTH))],
        out_specs=[blk, tok.state_spec(POOL_PREFIX, C_WIDTH)],
        out_shape=[jax.ShapeDtypeStruct(tok.shape(C_WIDTH), BF16),
                   jax.ShapeDtypeStruct(tok.state_shape(POOL_PREFIX, C_WIDTH), F32)],
        scratch_shapes=[pltpu.VMEM((rows, C_WIDTH), F32)] * 4,
        compiler_params=_cparams(("arbitrary", "arbitrary")),
        name="pool",
    )(pc, pool_prev, wbd, pscale)


def _ffn_kernel(x_ref, ya_ref, g_ref, yb_ref, yc_ref, cst_ref, wout_ref, gffn_ref, wup_ref, cw_ref, cb_ref,
                wdn_ref, gfin_ref, xo_ref, cnew_ref, gbuf_s, *, tm, ts, final):
    taps = CONV_WIDTH - 1
    pad = max(SUBLANES, taps * ts)
    it = pl.program_id(1)

    @pl.when(it == 0)
    def _():
        gbuf_s[pad - taps * ts:pad, :] = cst_ref[...].reshape(taps * ts, D_FF)

    x = x_ref[...].reshape(tm, D_MODEL)
    ya = (ya_ref[...].reshape(tm, A_WIDTH) * g_ref[...].reshape(tm, A_WIDTH)).astype(BF16)
    ycat = jnp.concatenate([ya, yb_ref[...].reshape(tm, B_WIDTH), yc_ref[...].reshape(tm, C_WIDTH)], axis=-1)
    x1 = x + jnp.dot(ycat, wout_ref[...], preferred_element_type=F32)
    xn = _rms(x1, gffn_ref[...]).astype(BF16)
    acc = jnp.zeros((tm, D_MODEL), F32)
    assert sum(FF_CHUNKS) == D_FF
    lo = 0
    for width in FF_CHUNKS:
        hi = lo + width
        up = jnp.dot(xn, wup_ref[:, lo:hi], preferred_element_type=F32)
        gate = jnp.dot(xn, wup_ref[:, D_FF + lo:D_FF + hi], preferred_element_type=F32)
        gbuf_s[pad:pad + tm, lo:hi] = gate
        gc = cb_ref[:, lo:hi] + cw_ref[2:3, lo:hi] * gate
        for j in range(taps):
            off = pad - (taps - j) * ts
            gc = gc + cw_ref[j:j + 1, lo:hi] * gbuf_s[off:off + tm, lo:hi]
        hh = (gc * _sigmoid(gc) * up).astype(BF16)
        acc = acc + jnp.dot(hh, wdn_ref[lo:hi, :], preferred_element_type=F32)
        lo = hi
    last = gbuf_s[pad + tm - taps * ts:pad + tm, :]
    cnew_ref[...] = last.reshape(cnew_ref.shape)
    gbuf_s[pad - taps * ts:pad, :] = last
    x2 = x1 + acc
    if final:
        x2 = _rms(x2, gfin_ref[...])
    xo_ref[...] = x2.reshape(xo_ref.shape)


def _ffn(tok, x, ya, g, yb, yc, conv_prev, p, gfin, final):
    tm, ts = tok.tm, tok.ts
    taps = CONV_WIDTH - 1
    pad = max(SUBLANES, taps * ts)
    return pl.pallas_call(
        functools.partial(_ffn_kernel, tm=tm, ts=ts, final=final),
        grid=tok.grid,
        in_specs=[tok.spec(D_MODEL), tok.spec(A_WIDTH), tok.spec(A_WIDTH), tok.spec(B_WIDTH), tok.spec(C_WIDTH),
                  tok.state_spec(taps, D_FF), _const_spec((D_MODEL, D_MODEL)), _const_spec((1, D_MODEL)),
                  _const_spec((D_MODEL, 2 * D_FF)), _const_spec((CONV_WIDTH, D_FF)), _const_spec((1, D_FF)),
                  _const_spec((D_FF, D_MODEL)), _const_spec((1, D_MODEL))],
        out_specs=[tok.spec(D_MODEL), tok.state_spec(taps, D_FF)],
        out_shape=[jax.ShapeDtypeStruct(tok.shape(D_MODEL), F32),
                   jax.ShapeDtypeStruct(tok.state_shape(taps, D_FF), F32)],
        scratch_shapes=[pltpu.VMEM((pad + tm, D_FF), F32)],
        compiler_params=_cparams(("arbitrary", "arbitrary")),
        name="outproj_ffn",
    )(x, ya, g, yb, yc, conv_prev, p["w_out"], p["g_ffn"], p["w_up"], p["conv_w"], p["conv_b"], p["w_down"], gfin)


def _ext_columns():
    qn0 = A_COLS
    qr0 = qn0 + B_HEADS * B_NOPE_DIM
    ckv0 = qr0 + B_HEADS * B_ROPE_DIM
    kr0 = ckv0 + KV_RANK
    pc0 = A_COLS + B_COLS
    cols = list(range(A_COLS)) + list(range(qn0, qr0)) + list(range(ckv0, kr0)) + list(range(pc0, pc0 + C_WIDTH))
    for half in range(2):
        blk = []
        for h in range(B_HEADS):
            blk += [qr0 + h * B_ROPE_DIM + half * HALF_ROPE + i for i in range(HALF_ROPE)]
        blk += [kr0 + half * HALF_ROPE + i for i in range(HALF_ROPE)]
        cols += blk + [-1] * (LANES - len(blk))
    for half in range(2):
        blk = [kr0 + half * HALF_ROPE + i for _ in range(B_HEADS) for i in range(HALF_ROPE)]
        cols += blk + [-1] * (LANES - len(blk))
    assert len(cols) == IN_EXT
    return np.asarray(cols, np.int32)


def _layer_params(l, norm_mix_g, w_in, mu_shift, decay_w0, decay_w2, iclr_a0, iclr_a2, gate_g2, k_k, k_a, r_k,
                  lnx_w, lnx_b, kv_norm_g, w_uk, w_uv, pool_w, pool_scale, w_out, norm_ffn_g, w_up, conv_w,
                  conv_b, w_down):
    cols = _ext_columns()
    w_ext = jnp.where(cols[None, :] >= 0, w_in[l][:, np.maximum(cols, 0)], 0.0).astype(BF16)
    lora_cat = jnp.zeros((LANES, 2 * A_WIDTH), F32)
    lora_cat = lora_cat.at[0:DECAY_LORA, 0:A_WIDTH].set(decay_w2[l])
    lora_cat = lora_cat.at[DECAY_LORA:DECAY_LORA + ICLR_LORA, A_WIDTH:].set(iclr_a2[l])
    head = np.arange(A_WIDTH) // A_HEAD_DIM
    head_ones = jnp.asarray((head[:, None] == head[None, :]).astype(np.float32), BF16)
    wbd = jnp.zeros((C_WIDTH, C_WIDTH), F32)
    for gi in range(C_GROUPS):
        sl = slice(gi * C_GROUP_DIM, (gi + 1) * C_GROUP_DIM)
        wbd = wbd.at[sl, sl].set(pool_w[l, gi])
    row = lambda v: v.reshape(1, -1)
    return dict(
        g_mix=row(norm_mix_g[l]), w_ext=w_ext, kv_g=row(kv_norm_g[l]),
        mu=row(mu_shift[l]), lora_cat=lora_cat.astype(BF16), w0=row(decay_w0[l]), a0=row(iclr_a0[l]),
        g2=gate_g2[l].astype(BF16), k_k=row(k_k[l]), k_a=row(k_a[l]), head_ones=head_ones,
        r_k=r_k[l], lnx_w=lnx_w[l].reshape(A_HEADS, A_HEAD_DIM), lnx_b=lnx_b[l].reshape(A_HEADS, A_HEAD_DIM),
        wuk=jnp.transpose(w_uk[l], (1, 2, 0)).astype(BF16), wuv=jnp.transpose(w_uv[l], (1, 0, 2)).astype(BF16),
        pool_wbd=wbd.astype(BF16), pool_scale=row(pool_scale[l]),
        w_out=w_out[l].astype(BF16), g_ffn=row(norm_ffn_g[l]), w_up=w_up[l].astype(BF16),
        conv_w=conv_w[l], conv_b=row(conv_b[l]), w_down=w_down[l].astype(BF16),
    )


def _rope_tables(pos):
    inv = ROPE_BASE ** (-jnp.arange(0, B_ROPE_DIM, 2, dtype=F32) / B_ROPE_DIM)
    ang = pos.astype(F32)[:, None] * inv[None, :]
    reps = LANES // HALF_ROPE
    return jnp.tile(jnp.cos(ang), (1, reps)), jnp.tile(jnp.sin(ang), (1, reps))


def _to_scan(tok, x):
    assert tok.time_major
    x = x.reshape(tok.T, tok.B, A_HEADS, A_HEAD_DIM)
    return jnp.transpose(x, (2, 0, 3, 1))


def _lane_maps(nb):
    lane_of = np.asarray([[_prompt_lane(b, h, nb) for h in range(A_HEADS)] for b in range(nb)], np.int32)
    lanes_b = np.zeros(nb * A_HEADS, np.int32)
    lanes_h = np.zeros(nb * A_HEADS, np.int32)
    for b in range(nb):
        for h in range(A_HEADS):
            lanes_b[lane_of[b, h]], lanes_h[lane_of[b, h]] = b, h
    return lane_of, lanes_b, lanes_h


def _from_scan(tok, y):
    H, N = A_HEADS, A_HEAD_DIM
    if tok.time_major:
        return jnp.transpose(y, (1, 3, 0, 2)).reshape(tok.T, tok.B, H * N)
    lane_of, _, _ = _lane_maps(tok.B)
    y = y[0][:, :, lane_of]
    return jnp.transpose(y, (2, 0, 3, 1)).reshape(tok.B, tok.T, H * N)


def _pad_lanes(x):
    return jnp.pad(x, [(0, 0)] * (x.ndim - 1) + [(0, LANES - x.shape[-1])])


def _state_to_scan(tok, s):
    if tok.time_major:
        return jnp.transpose(s, (1, 3, 2, 0))
    _, lanes_b, lanes_h = _lane_maps(tok.B)
    return _pad_lanes(jnp.transpose(s, (3, 2, 0, 1))[:, :, lanes_b, lanes_h])[None]


def _state_from_scan(tok, s):
    if tok.time_major:
        return jnp.transpose(s, (3, 0, 2, 1))
    lane_of, _, _ = _lane_maps(tok.B)
    return jnp.transpose(s[0][:, :, lane_of], (2, 3, 1, 0))


def _head_consts(tok, p):
    def lay(v):
        if tok.time_major:
            return jnp.broadcast_to(v[:, :, None], (A_HEADS, A_HEAD_DIM, LANES))
        _, _, lanes_h = _lane_maps(tok.B)
        return _pad_lanes(v.T[:, lanes_h])[None]
    return lay(p["lnx_w"]), lay(p["lnx_b"]), lay(p["r_k"])


def _layer(tok, x, cos, sin, pos0, shift_prev, wkv_prev, pool_prev, conv_prev, p, gfin, final, scan_tc, attend):
    pa, qn, kvlat, pc, qr, kfull, krope = _inproj(tok, x, p["g_mix"], p["w_ext"], p["kv_g"], cos, sin)
    if tok.time_major:
        r, w, k, v, a, b, g, shift_new = _rwkv_prep(tok, pa, shift_prev, p)
        r, w, k, v, a, b = (_to_scan(tok, t) for t in (r, w, k, v, a, b))
    else:
        r, w, k, v, a, b, g, shift_new = _rwkv_prep_scan(pa, shift_prev, p, scan_tc)
    lnw, lnb, rk = _head_consts(tok, p)
    y, s_new = _rwkv_scan(w, a, b, k, v, r, _state_to_scan(tok, wkv_prev), lnw, lnb, rk, scan_tc)
    ya = _from_scan(tok, y)
    yb = attend(qn, qr, kfull)
    yc, pool_new = _pool(tok, pc, pool_prev, p["pool_wbd"], p["pool_scale"], pos0)
    x, conv_new = _ffn(tok, x, ya, g, yb, yc, conv_prev, p, gfin, final)
    return x, (kvlat, krope, _state_from_scan(tok, s_new), shift_new, pool_new, conv_new)


def kernel(x_prompt, x_sample, cache_kv_latent, cache_k_rope, page_table, state_wkv, state_shift, state_pool,
           state_conv, norm_mix_g, w_in, mu_shift, decay_w0, decay_w2, iclr_a0, iclr_a2, gate_g2, k_k, k_a, r_k,
           lnx_w, lnx_b, kv_norm_g, w_uk, w_uv, pool_w, pool_scale, w_out, norm_ffn_g, w_up, conv_w, conv_b,
           w_down, norm_final_g):
    Bp, Tp, _ = x_prompt.shape
    Bs, Ts, _ = x_sample.shape
    past_len = page_table.shape[1] * PAGE_SIZE
    tokp = _Tok(False, Bp, Tp, min(512, Tp))
    toks = _Tok(True, Bs, Ts, min(32, Bs))
    cos_p, sin_p = _rope_tables(jnp.arange(Tp, dtype=jnp.int32))
    pos_s = past_len + jnp.repeat(jnp.arange(Ts, dtype=jnp.int32), toks.bb)
    cos_s, sin_s = _rope_tables(pos_s)
    gfin = norm_final_g.reshape(1, -1)
    tm_first = lambda t: jnp.swapaxes(t, 0, 1)
    cache_krt = jnp.swapaxes(cache_k_rope, 2, 3)

    xp = x_prompt
    xs = tm_first(x_sample)
    zeros = lambda *shape: jnp.zeros(shape, F32)
    outs_p, outs_s = [], []
    for l in range(DEPTH):
        p = _layer_params(l, norm_mix_g, w_in, mu_shift, decay_w0, decay_w2, iclr_a0, iclr_a2, gate_g2, k_k, k_a,
                          r_k, lnx_w, lnx_b, kv_norm_g, w_uk, w_uv, pool_w, pool_scale, w_out, norm_ffn_g, w_up,
                          conv_w, conv_b, w_down)
        final = l == DEPTH - 1

        attend_p = lambda qn, qr, kfull: _mla_prompt(qn, qr, kfull, p["wuk"], p["wuv"], min(256, Tp))
        xp, st_p = _layer(tokp, xp, cos_p, sin_p, 0, zeros(Bp, A_COLS),
                          zeros(Bp, A_HEADS, A_HEAD_DIM, A_HEAD_DIM), zeros(Bp, POOL_PREFIX, C_WIDTH),
                          zeros(Bp, CONV_WIDTH - 1, D_FF), p, gfin, final, min(32, Tp), attend_p)

        def attend_s(qn, qr, kfull):
            yb = _mla_decode(page_table, tm_first(qn), tm_first(qr), tm_first(kfull), p["wuk"], p["wuv"],
                             cache_kv_latent, cache_krt, l)
            return tm_first(yb)
        xs, st_s = _layer(toks, xs, cos_s, sin_s, past_len, state_shift[l][None], state_wkv[l],
                          tm_first(state_pool[l]), tm_first(state_conv[l]), p, gfin, final, Ts, attend_s)
        outs_p.append(st_p)
        outs_s.append(st_s)

    stack = lambda outs, i, f: jnp.stack([f(o[i]) for o in outs], axis=0)
    ident = lambda t: t
    return (xp, tm_first(xs),
            stack(outs_p, 0, ident), stack(outs_p, 1, ident), stack(outs_p, 2, ident),
            stack(outs_p, 3, ident), stack(outs_p, 4, ident), stack(outs_p, 5, ident),
            stack(outs_s, 0, tm_first), stack(outs_s, 1, tm_first), stack(outs_s, 2, ident),
            stack(outs_s, 3, lambda t: t[0]), stack(outs_s, 4, tm_first), stack(outs_s, 5, tm_first))
```

```python
import functools
import math

import jax
import jax.numpy as jnp
import numpy as np
from jax import lax
from jax.experimental import pallas as pl
from jax.experimental.pallas import tpu as pltpu

F32 = jnp.float32
BF16 = jnp.bfloat16

D_MODEL = 1024
DEPTH = 2
PAGE_SIZE = 128
A_HEADS = 6
A_HEAD_DIM = 64
A_WIDTH = A_HEADS * A_HEAD_DIM
DECAY_LORA = 64
ICLR_LORA = 64
GATE_LORA = 128
A_COLS = 3 * A_WIDTH + DECAY_LORA + ICLR_LORA + GATE_LORA
B_HEADS = 6
B_NOPE_DIM = 64
B_ROPE_DIM = 32
B_V_DIM = 64
B_WIDTH = B_HEADS * B_V_DIM
KV_RANK = 256
ROPE_BASE = 10000.0
B_COLS = B_HEADS * (B_NOPE_DIM + B_ROPE_DIM) + KV_RANK + B_ROPE_DIM
C_GROUPS = 4
C_GROUP_DIM = 64
C_WIDTH = C_GROUPS * C_GROUP_DIM
POOL_WINDOWS = (2, 4, 8, 16)
POOL_PREFIX = 15
D_FF = 2816
CONV_WIDTH = 3
NORM_EPS = 1e-6
GN_EPS = A_HEAD_DIM * 1e-5

LANES = 128
SUBLANES = 8
HALF_ROPE = B_ROPE_DIM // 2
IN_EXT = A_COLS + B_HEADS * B_NOPE_DIM + KV_RANK + C_WIDTH + 4 * LANES
OFF_QN = A_COLS
OFF_CKV = OFF_QN + B_HEADS * B_NOPE_DIM
OFF_PC = OFF_CKV + KV_RANK
OFF_ROPE = OFF_PC + C_WIDTH
KFULL = KV_RANK + 2 * LANES
MXU_TILE = 256
FF_CHUNKS = (6 * MXU_TILE, 5 * MXU_TILE)
VMEM_LIMIT = 56 * 1024 * 1024


def _cparams(sem):
    return pltpu.CompilerParams(dimension_semantics=sem, vmem_limit_bytes=VMEM_LIMIT)


def _const_spec(shape):
    nd = len(shape)
    return pl.BlockSpec(shape, lambda *_: (0,) * nd, pipeline_mode=pl.Buffered(1))


class _Tok:
    def __init__(self, time_major, B, T, tile):
        self.time_major, self.B, self.T = time_major, B, T
        if time_major:
            self.bb = tile
            self.tm, self.ts = T * tile, tile
            self.grid = (B // tile, 1)
        else:
            self.tm, self.ts = tile, 1
            self.grid = (B, T // tile)
        self.n = B * T

    def shape(self, C):
        return (self.T, self.B, C) if self.time_major else (self.B, self.T, C)

    def spec(self, C):
        if self.time_major:
            return pl.BlockSpec((self.T, self.bb, C), lambda i, j: (0, i, 0))
        return pl.BlockSpec((None, self.tm, C), lambda i, j: (i, j, 0))

    def state_shape(self, steps, C):
        return (steps, self.B, C) if self.time_major else (self.B, steps, C)

    def state_spec(self, steps, C):
        if self.time_major:
            return pl.BlockSpec((steps, self.bb, C), lambda i, j: (0, i, 0))
        return pl.BlockSpec((None, steps, C), lambda i, j: (i, 0, 0))


def _rms(x, g):
    return x * lax.rsqrt(jnp.mean(x * x, axis=-1, keepdims=True) + NORM_EPS) * g


def _sigmoid(x):
    return 1.0 / (1.0 + jnp.exp(-x))


def _div_pow2(x, d):
    assert d & (d - 1) == 0
    return lax.shift_right_logical(x, int(math.log2(d)))


def _mod_pow2(x, d):
    assert d & (d - 1) == 0
    return lax.bitwise_and(x, d - 1)


def _inproj_kernel(x_ref, g_ref, w_ref, kvg_ref, cos_ref, sin_ref,
                   pa_ref, qn_ref, kvlat_ref, pc_ref, qr_ref, kfull_ref, krope_ref, *, tm):
    x = x_ref[...].reshape(tm, D_MODEL)
    xb = _rms(x, g_ref[...]).astype(BF16)

    def proj(lo, hi):
        return jnp.dot(xb, w_ref[:, lo:hi], preferred_element_type=F32)

    assert OFF_CKV % MXU_TILE == 0 and OFF_ROPE % MXU_TILE == 0 and IN_EXT % MXU_TILE == 0
    head = proj(0, OFF_CKV)
    pa_ref[...] = head[:, 0:A_COLS].reshape(pa_ref.shape)
    qn_ref[...] = head[:, OFF_QN:OFF_CKV].astype(BF16).reshape(qn_ref.shape)
    mid = proj(OFF_CKV, OFF_ROPE)
    ckv = _rms(mid[:, 0:KV_RANK], kvg_ref[...])
    kvlat_ref[...] = ckv.reshape(kvlat_ref.shape)
    pc_ref[...] = mid[:, KV_RANK:KV_RANK + C_WIDTH].reshape(pc_ref.shape)
    rr = proj(OFF_ROPE, IN_EXT)
    cos = cos_ref[...]
    sin = sin_ref[...]
    q1, q2 = rr[:, 0:LANES], rr[:, LANES:2 * LANES]
    k1, k2 = rr[:, 2 * LANES:3 * LANES], rr[:, 3 * LANES:4 * LANES]
    o1 = q1 * cos - q2 * sin
    o2 = q1 * sin + q2 * cos
    qr = jnp.concatenate([o1, o2], axis=-1).astype(BF16)
    qr_ref[...] = qr.reshape(qr_ref.shape)
    kf = jnp.concatenate([ckv, k1 * cos - k2 * sin, k1 * sin + k2 * cos], axis=-1).astype(BF16)
    kfull_ref[...] = kf.reshape(kfull_ref.shape)
    kr_lo = B_HEADS * HALF_ROPE
    krope = jnp.concatenate([o1[:, kr_lo:kr_lo + HALF_ROPE], o2[:, kr_lo:kr_lo + HALF_ROPE]], axis=-1)
    krope_ref[...] = krope.reshape(krope_ref.shape)


def _inproj(tok, x, g, w_ext, kvg, cos, sin):
    tm = tok.tm
    n_tab = cos.shape[0] // tm
    tab_spec = pl.BlockSpec((tm, LANES), lambda i, j: (j % n_tab, 0))
    widths = (A_COLS, A_WIDTH, KV_RANK, C_WIDTH, 2 * LANES, KFULL, B_ROPE_DIM)
    dtypes = (F32, BF16, F32, F32, BF16, BF16, F32)
    return pl.pallas_call(
        functools.partial(_inproj_kernel, tm=tm),
        grid=tok.grid,
        in_specs=[tok.spec(D_MODEL), _const_spec((1, D_MODEL)), _const_spec((D_MODEL, IN_EXT)),
                  _const_spec((1, KV_RANK)), tab_spec, tab_spec],
        out_specs=[tok.spec(c) for c in widths],
        out_shape=[jax.ShapeDtypeStruct(tok.shape(c), d) for c, d in zip(widths, dtypes)],
        compiler_params=_cparams(("arbitrary", "arbitrary")),
        name="inproj",
    )(x, g, w_ext, kvg, cos, sin)


def _rwkv_token_math(pa, prev, mu_ref, wcat_ref, w0_ref, a0_ref, g2_ref, kk_ref, ka_ref, ones_ref):
    s = pa + (prev - pa) * mu_ref[...]
    W = A_WIDTH
    r, k, v = s[:, 0:W], s[:, W:2 * W], s[:, 2 * W:3 * W]
    lo_in = s[:, 3 * W:3 * W + LANES]
    gd = s[:, 3 * W + LANES:3 * W + 2 * LANES]
    lane = lax.broadcasted_iota(jnp.int32, lo_in.shape, 1)
    lo_in = jnp.where(lane < DECAY_LORA, jnp.tanh(lo_in), lo_in)
    lo = jnp.dot(lo_in.astype(BF16), wcat_ref[...], preferred_element_type=F32)
    z = -(w0_ref[...] + lo[:, 0:W])
    softplus = jnp.maximum(z, 0.0) + jnp.log(1.0 + jnp.exp(-jnp.abs(z)))
    decay = jnp.exp(-jnp.exp(-softplus - 0.5))
    a = _sigmoid(a0_ref[...] + lo[:, W:2 * W])
    g = jnp.dot(_sigmoid(gd).astype(BF16), g2_ref[...], preferred_element_type=F32)
    kk = k * kk_ref[...]
    kk2 = kk * kk
    hi = kk2.astype(BF16)
    lo2 = (kk2 - hi.astype(F32)).astype(BF16)
    ones = ones_ref[...]
    ss = (jnp.dot(hi, ones, preferred_element_type=F32) + jnp.dot(lo2, ones, preferred_element_type=F32))
    kk = kk * lax.rsqrt(jnp.maximum(ss, 1e-24))
    kmod = k * (1.0 + (a - 1.0) * ka_ref[...])
    return r, decay, kmod, v, -kk, kk * a, g


def _rwkv_prep_kernel(pa_ref, st_ref, mu_ref, wcat_ref, w0_ref, a0_ref, g2_ref, kk_ref, ka_ref, ones_ref,
                      r_ref, w_ref, k_ref, v_ref, a_ref, b_ref, g_ref, newst_ref, buf_ref, *, tm, ts):
    pad = max(SUBLANES, ts)
    it = pl.program_id(1)

    @pl.when(it == 0)
    def _():
        buf_ref[pad - ts:pad, :] = st_ref[...].reshape(ts, A_COLS)

    pa = pa_ref[...].reshape(tm, A_COLS)
    buf_ref[pad:pad + tm, :] = pa
    prev = buf_ref[pad - ts:pad - ts + tm, :]
    last = pa[tm - ts:tm, :]
    buf_ref[pad - ts:pad, :] = last
    newst_ref[...] = last.reshape(newst_ref.shape)
    vals = _rwkv_token_math(pa, prev, mu_ref, wcat_ref, w0_ref, a0_ref, g2_ref, kk_ref, ka_ref, ones_ref)
    for ref, val in zip((r_ref, w_ref, k_ref, v_ref, a_ref, b_ref, g_ref), vals):
        ref[...] = val.reshape(ref.shape)


def _prompt_lane(b, h, nb):
    return (h % 2) * (A_HEADS // 2) * nb + (h // 2) * nb + b


def _rwkv_prep_scan_kernel(pa_ref, st_ref, mu_ref, wcat_ref, w0_ref, a0_ref, g2_ref, kk_ref, ka_ref, ones_ref,
                           r_ref, w_ref, k_ref, v_ref, a_ref, b_ref, g_ref, newst_ref, buf_ref, q_ref, *, nb, tt):
    it = pl.program_id(0)

    @pl.when(it == 0)
    def _():
        buf_ref[0:nb, :] = st_ref[...]

    for t in range(tt):
        buf_ref[(t + 1) * nb:(t + 2) * nb, :] = pa_ref[:, t, :]
    pa = buf_ref[nb:(tt + 1) * nb, :]
    prev = buf_ref[0:tt * nb, :]
    vals = _rwkv_token_math(pa, prev, mu_ref, wcat_ref, w0_ref, a0_ref, g2_ref, kk_ref, ka_ref, ones_ref)
    last = buf_ref[tt * nb:(tt + 1) * nb, :]
    buf_ref[0:nb, :] = last
    newst_ref[...] = last
    g = vals[6]
    for t in range(tt):
        g_ref[:, t, :] = g[t * nb:(t + 1) * nb, :]
    for qi in range(6):
        q_ref[qi] = vals[qi]
    outs = (r_ref, w_ref, k_ref, v_ref, a_ref, b_ref)
    half = A_HEADS // 2
    win = half * nb
    N = A_HEAD_DIM

    def pair(pi, carry):
        for qi, o_ref in enumerate(outs):
            rows = q_ref[qi, pl.ds(pl.multiple_of(pi * 2 * nb, 2 * nb), 2 * nb), :]
            pieces = [rows[s * nb:(s + 1) * nb, hp * LANES:(hp + 1) * LANES] for s in range(2) for hp in range(half)]
            m = jnp.concatenate(pieces + [jnp.zeros((LANES - 2 * win, LANES), F32)], axis=0)
            mt = m.T
            unused = jnp.full((N, LANES - 2 * win), 1.0 if o_ref is w_ref else 0.0, F32)
            for s in range(2):
                tile = jnp.concatenate([mt[0:N, s * win:(s + 1) * win], mt[N:2 * N, s * win:(s + 1) * win], unused],
                                       axis=1)
                o_ref[0, pi * 2 + s] = tile
        return carry

    lax.fori_loop(0, tt // 2, pair, 0, unroll=8)


def _rwkv_prep_scan(pa, shift_prev, p, tt):
    nb, T, _ = pa.shape
    assert 2 * (A_HEADS // 2) * nb <= LANES and tt % 2 == 0
    vec = _const_spec((1, A_WIDTH))
    seq = pl.BlockSpec((1, tt, A_HEAD_DIM, LANES), lambda i: (0, i, 0, 0))
    st = pl.BlockSpec((nb, A_COLS), lambda i: (0, 0))
    return pl.pallas_call(
        functools.partial(_rwkv_prep_scan_kernel, nb=nb, tt=tt),
        grid=(T // tt,),
        in_specs=[pl.BlockSpec((nb, tt, A_COLS), lambda i: (0, i, 0)), st, _const_spec((1, A_COLS)),
                  _const_spec((LANES, 2 * A_WIDTH)), vec, vec, _const_spec((GATE_LORA, A_WIDTH)), vec, vec,
                  _const_spec((A_WIDTH, A_WIDTH))],
        out_specs=[seq] * 6 + [pl.BlockSpec((nb, tt, A_WIDTH), lambda i: (0, i, 0)), st],
        out_shape=[jax.ShapeDtypeStruct((1, T, A_HEAD_DIM, LANES), F32)] * 6
        + [jax.ShapeDtypeStruct((nb, T, A_WIDTH), F32), jax.ShapeDtypeStruct((nb, A_COLS), F32)],
        scratch_shapes=[pltpu.VMEM(((tt + 1) * nb, A_COLS), F32), pltpu.VMEM((6, tt * nb, A_WIDTH), F32)],
        compiler_params=_cparams(("arbitrary",)),
        name="rwkv_prep_scan",
    )(pa, shift_prev, p["mu"], p["lora_cat"], p["w0"], p["a0"], p["g2"], p["k_k"], p["k_a"], p["head_ones"])


def _rwkv_prep(tok, pa, shift_prev, p):
    tm, ts = tok.tm, tok.ts
    pad = max(SUBLANES, ts)
    vec = _const_spec((1, A_WIDTH))
    return pl.pallas_call(
        functools.partial(_rwkv_prep_kernel, tm=tm, ts=ts),
        grid=tok.grid,
        in_specs=[tok.spec(A_COLS), tok.state_spec(1, A_COLS), _const_spec((1, A_COLS)),
                  _const_spec((LANES, 2 * A_WIDTH)), vec, vec, _const_spec((GATE_LORA, A_WIDTH)), vec, vec,
                  _const_spec((A_WIDTH, A_WIDTH))],
        out_specs=[tok.spec(A_WIDTH)] * 7 + [tok.state_spec(1, A_COLS)],
        out_shape=[jax.ShapeDtypeStruct(tok.shape(A_WIDTH), F32)] * 7
        + [jax.ShapeDtypeStruct(tok.state_shape(1, A_COLS), F32)],
        scratch_shapes=[pltpu.VMEM((pad + tm, A_COLS), F32)],
        compiler_params=_cparams(("arbitrary", "arbitrary")),
        name="rwkv_prep",
    )(pa, shift_prev, p["mu"], p["lora_cat"], p["w0"], p["a0"], p["g2"], p["k_k"], p["k_a"], p["head_ones"])


def _rwkv_scan_kernel(w_ref, a_ref, b_ref, k_ref, v_ref, r_ref, s0_ref, lnw_ref, lnb_ref, rk_ref,
                      y_ref, sT_ref, S_ref, op_ref, *, tc):
    N = A_HEAD_DIM
    it = pl.program_id(1)

    @pl.when(it == 0)
    def _():
        S_ref[...] = s0_ref[0]

    def step(t, p_prev):
        p_new = p_prev * w_ref[0, t]
        inv = 1.0 / p_new
        r = r_ref[0, t]
        k = k_ref[0, t]
        rt = r * p_new
        bt = b_ref[0, t] * inv
        kt = k * inv
        op_ref[0] = a_ref[0, t] * p_prev
        op_ref[1] = rt
        op_ref[2] = bt
        op_ref[3] = kt
        br = jnp.sum(bt * rt, axis=0, keepdims=True)
        kr = jnp.sum(kt * rt, axis=0, keepdims=True)
        bonus = jnp.sum(r * k * rk_ref[0], axis=0, keepdims=True)
        sa = [jnp.zeros((N, LANES), F32) for _ in range(2)]
        yp = [jnp.zeros((N, LANES), F32) for _ in range(2)]
        for j in range(N):
            Zj = S_ref[j]
            sa[j % 2] = sa[j % 2] + Zj * op_ref[0, pl.ds(j, 1), :]
            yp[j % 2] = yp[j % 2] + Zj * op_ref[1, pl.ds(j, 1), :]
        sa = sa[0] + sa[1]
        yp = yp[0] + yp[1]
        v = v_ref[0, t]
        for j in range(N):
            S_ref[j] = S_ref[j] + sa * op_ref[2, pl.ds(j, 1), :] + v * op_ref[3, pl.ds(j, 1), :]
        y = yp + sa * br + v * kr
        mu = jnp.mean(y, axis=0, keepdims=True)
        d = y - mu
        var = jnp.mean(d * d, axis=0, keepdims=True)
        yn = d * lax.rsqrt(var + GN_EPS) * lnw_ref[0] + lnb_ref[0]
        y_ref[0, t] = yn + bonus * v
        return p_new

    op_ref[0] = lax.fori_loop(0, tc, step, jnp.ones((N, LANES), F32))
    for j in range(N):
        S_ref[j] = S_ref[j] * op_ref[0, pl.ds(j, 1), :]

    @pl.when(it == pl.num_programs(1) - 1)
    def _():
        sT_ref[0] = S_ref[...]


def _rwkv_scan(w, a, b, k, v, r, s0, lnw, lnb, rk, tc):
    G, T, N, _ = w.shape
    seq = pl.BlockSpec((1, tc, N, LANES), lambda g, i: (g, i, 0, 0))
    st = pl.BlockSpec((1, N, N, LANES), lambda g, i: (g, 0, 0, 0))
    cst = pl.BlockSpec((1, N, LANES), lambda g, i: (g, 0, 0))
    return pl.pallas_call(
        functools.partial(_rwkv_scan_kernel, tc=tc),
        grid=(G, T // tc),
        in_specs=[seq] * 6 + [st, cst, cst, cst],
        out_specs=[seq, st],
        out_shape=[jax.ShapeDtypeStruct((G, T, N, LANES), F32), jax.ShapeDtypeStruct((G, N, N, LANES), F32)],
        scratch_shapes=[pltpu.VMEM((N, N, LANES), F32), pltpu.VMEM((4, N, LANES), F32)],
        compiler_params=_cparams(("arbitrary", "arbitrary")),
        name="rwkv_scan",
    )(w, a, b, k, v, r, s0, lnw, lnb, rk)


_NT = (((1,), (1,)), ((), ()))


def _build_queries(qn, qr, wuk_ref, scale, rows):
    lane = lax.broadcasted_iota(jnp.int32, (rows, 2 * LANES), 1)
    slot = _div_pow2(_mod_pow2(lane, LANES), HALF_ROPE)
    qrf = qr.astype(F32) * scale
    parts = []
    for h in range(B_HEADS):
        ql = jnp.dot(qn[:, h * B_NOPE_DIM:(h + 1) * B_NOPE_DIM], wuk_ref[h], preferred_element_type=F32) * scale
        parts.append(jnp.concatenate([ql, jnp.where(slot == h, qrf, 0.0)], axis=-1).astype(BF16))
    return parts


def _mla_prompt_kernel(qn_ref, qr_ref, kf_ref, wuk_ref, wuv_ref, o_ref, q_s, s_s, p_s, m_s, l_s, a_s, acc_s,
                       *, tq, scale, row_block):
    qi = pl.program_id(1)
    parts = _build_queries(qn_ref[...], qr_ref[...], wuk_ref, scale, tq)
    for h in range(B_HEADS):
        q_s[h * tq:(h + 1) * tq, :] = parts[h]
    rows = B_HEADS * tq
    m_s[...] = jnp.full((rows, LANES), -jnp.inf, F32)
    l_s[...] = jnp.zeros((rows, LANES), F32)
    acc_s[...] = jnp.zeros((rows, KV_RANK), F32)

    def keys(kc):
        return pl.ds(pl.multiple_of(kc * tq, tq), tq)

    def scores(kc, slot):
        s_s[slot] = lax.dot_general(q_s[...], kf_ref[keys(kc), :], _NT, preferred_element_type=F32)

    def lane_tiles(s):
        return [s[:, c * LANES:(c + 1) * LANES] for c in range(tq // LANES)]

    def softmax_pv(kc, slot, masked):
        blocks = [slice(rb * row_block, (rb + 1) * row_block) for rb in range(rows // row_block)]
        for rb, blk in enumerate(blocks):
            s = s_s[slot, blk, :]
            if masked:
                qt = _mod_pow2(rb * row_block + lax.broadcasted_iota(jnp.int32, (row_block, tq), 0), tq)
                kpos = lax.broadcasted_iota(jnp.int32, (row_block, tq), 1)
                s = jnp.where(kpos <= qt, s, -jnp.inf)
                s_s[slot, blk, :] = s
            mx = functools.reduce(jnp.maximum, lane_tiles(s))
            m_prev = m_s[blk, :]
            m_new = jnp.maximum(m_prev, jnp.max(mx, axis=-1, keepdims=True))
            a_s[blk, :] = jnp.exp2(m_prev - m_new)
            m_s[blk, :] = m_new
        for blk in blocks:
            m_new = m_s[blk, :]
            ps = [jnp.exp2(t - m_new) for t in lane_tiles(s_s[slot, blk, :])]
            l_s[blk, :] = a_s[blk, :] * l_s[blk, :] + functools.reduce(jnp.add, ps)
            p_s[blk, :] = jnp.concatenate(ps, axis=-1).astype(BF16)
        pv = jnp.dot(p_s[...], kf_ref[keys(kc), 0:KV_RANK], preferred_element_type=F32)
        alpha = a_s[...]
        acc_s[...] = acc_s[...] * jnp.concatenate([alpha] * (KV_RANK // LANES), axis=-1) + pv

    scores(0, 0)

    def pair(i, carry):
        scores(2 * i + 1, 1)
        softmax_pv(2 * i, 0, False)
        scores(2 * i + 2, 0)
        softmax_pv(2 * i + 1, 1, False)
        return carry

    lax.fori_loop(0, qi // 2, pair, 0)

    @pl.when(qi % 2 == 1)
    def _():
        scores(qi, 1)
        softmax_pv(qi - 1, 0, False)
        softmax_pv(qi, 1, True)

    @pl.when(qi % 2 == 0)
    def _():
        softmax_pv(qi, 0, True)

    o = (acc_s[...] * (1.0 / jnp.sum(l_s[...], axis=-1, keepdims=True))).astype(BF16)
    outs = [jnp.dot(o[h * tq:(h + 1) * tq, :], wuv_ref[h], preferred_element_type=F32) for h in range(B_HEADS)]
    o_ref[...] = jnp.concatenate(outs, axis=-1).astype(o_ref.dtype)


def _mla_prompt(qn, qr, kfull, wuk, wuv, tq):
    B, T, _ = qn.shape
    scale = (B_NOPE_DIM + B_ROPE_DIM) ** -0.5 * math.log2(math.e)
    rows = B_HEADS * tq
    tile = lambda c: pl.BlockSpec((None, tq, c), lambda b, i: (b, i, 0))
    stat = pltpu.VMEM((rows, LANES), F32)
    return pl.pallas_call(
        functools.partial(_mla_prompt_kernel, tq=tq, scale=scale, row_block=min(128, tq)),
        grid=(B, T // tq),
        in_specs=[tile(A_WIDTH), tile(2 * LANES), pl.BlockSpec((None, T, KFULL), lambda b, i: (b, 0, 0)),
                  _const_spec((B_HEADS, B_NOPE_DIM, KV_RANK)), _const_spec((B_HEADS, KV_RANK, B_V_DIM))],
        out_specs=tile(B_WIDTH),
        out_shape=jax.ShapeDtypeStruct((B, T, B_WIDTH), BF16),
        scratch_shapes=[pltpu.VMEM((rows, KFULL), BF16), pltpu.VMEM((2, rows, tq), F32),
                        pltpu.VMEM((rows, tq), BF16), stat, stat, stat, pltpu.VMEM((rows, KV_RANK), F32)],
        compiler_params=_cparams(("arbitrary", "arbitrary")),
        name="mla_prompt",
    )(qn, qr, kfull, wuk, wuv)


def _mla_decode_kernel(pt_ref, qn_ref, qr_ref, kfn_ref, wuk_ref, wuv_ref, ckv_hbm, krt_hbm,
                       o_ref, ckv_buf, krt_buf, kv16, s_s, m_s, l_s, acc_s, sem,
                       *, layer, n_pages, T, scale, ck, n_streams):
    i = pl.program_id(0)
    n = pl.num_programs(0)
    L = n_pages * PAGE_SIZE

    def page_copies(sl, p, page):
        rows = pl.ds(pl.multiple_of(p * PAGE_SIZE, PAGE_SIZE), PAGE_SIZE)
        return (pltpu.make_async_copy(ckv_hbm.at[layer, page], ckv_buf.at[sl, rows], sem.at[0, sl]),
                pltpu.make_async_copy(krt_hbm.at[layer, page], krt_buf.at[sl, :, rows], sem.at[1, sl]))

    def issue(bb, sl):
        def body(p, carry):
            for cp in page_copies(sl, p, pt_ref[bb, p]):
                cp.start()
            return carry
        lax.fori_loop(0, n_pages, body, 0, unroll=math.gcd(n_pages, 8))

    def wait(sl):
        for p in range(n_pages):
            for cp in page_copies(sl, p, 0):
                cp.wait()

    def attend(sl):
        qr = qr_ref[sl]
        kfn = kfn_ref[sl]
        parts = _build_queries(qn_ref[sl], qr, wuk_ref, scale, T)
        qfull = jnp.concatenate(parts, axis=0)
        qlat = qfull[:, 0:KV_RANK]
        qrs = (qr.astype(F32) * scale)
        qrope = jnp.concatenate(
            [jnp.concatenate([qrs[:, h * HALF_ROPE:(h + 1) * HALF_ROPE],
                              qrs[:, LANES + h * HALF_ROPE:LANES + (h + 1) * HALF_ROPE]], axis=-1)
             for h in range(B_HEADS)], axis=0).astype(BF16)
        nrows = B_HEADS * T
        per_stream = L // n_streams
        n_ck = per_stream // ck
        m_s[...] = jnp.full(m_s.shape, -jnp.inf, F32)
        l_s[...] = jnp.zeros(l_s.shape, F32)
        acc_s[...] = jnp.zeros(acc_s.shape, F32)

        def scores(st, kc, slot):
            keys = slice(st * per_stream + kc * ck, st * per_stream + (kc + 1) * ck)
            kv16[st, slot] = ckv_buf[sl, keys, :].astype(BF16)
            s_s[st, slot] = (lax.dot_general(qlat, kv16[st, slot], _NT, preferred_element_type=F32)
                             + jnp.dot(qrope, krt_buf[sl, :, keys].astype(BF16), preferred_element_type=F32))

        def softmax_pv(st, slot):
            s = s_s[st, slot]
            tiles = [s[:, c * LANES:(c + 1) * LANES] for c in range(ck // LANES)]
            m_prev = m_s[st]
            m_new = jnp.maximum(m_prev, jnp.max(functools.reduce(jnp.maximum, tiles), axis=-1, keepdims=True))
            alpha = jnp.exp(m_prev - m_new)
            ps = [jnp.exp(t - m_new) for t in tiles]
            l_s[st] = alpha * l_s[st] + functools.reduce(jnp.add, ps)
            m_s[st] = m_new
            pv = jnp.dot(jnp.concatenate(ps, axis=-1).astype(BF16), kv16[st, slot], preferred_element_type=F32)
            acc_s[st] = acc_s[st] * jnp.concatenate([alpha] * (KV_RANK // LANES), axis=-1) + pv

        streams = range(n_streams)
        for st in streams:
            scores(st, 0, 0)
        for kc in range(n_ck):
            if kc + 1 < n_ck:
                for st in streams:
                    scores(st, kc + 1, kc + 1)
            for st in streams:
                softmax_pv(st, kc)
        s_new = lax.dot_general(qfull, kfn, _NT, preferred_element_type=F32)
        qt = _mod_pow2(lax.broadcasted_iota(jnp.int32, (nrows, T), 0), T)
        kt = lax.broadcasted_iota(jnp.int32, (nrows, T), 1)
        s_new = jnp.where(kt <= qt, s_new, -jnp.inf)
        m_parts = [m_s[st][:, 0:1] for st in streams] + [jnp.max(s_new, axis=-1, keepdims=True)]
        m = functools.reduce(jnp.maximum, m_parts)
        p_new = jnp.exp(s_new - m)
        l = jnp.sum(p_new, axis=-1, keepdims=True)
        o = jnp.dot(p_new.astype(BF16), kfn[:, 0:KV_RANK], preferred_element_type=F32)
        for st in streams:
            w_st = jnp.exp(m_parts[st] - m)
            l = l + w_st * jnp.sum(l_s[st], axis=-1, keepdims=True)
            o = o + w_st * acc_s[st]
        o = (o / l).astype(BF16)
        outs = [jnp.dot(o[h * T:(h + 1) * T, :], wuv_ref[h], preferred_element_type=F32) for h in range(B_HEADS)]
        o_ref[sl] = jnp.concatenate(outs, axis=-1).astype(o_ref.dtype)

    @pl.when(i == 0)
    def _():
        issue(0, 0)

    issue(2 * i + 1, 1)
    wait(0)
    attend(0)

    @pl.when(i + 1 < n)
    def _():
        issue(2 * i + 2, 0)

    wait(1)
    attend(1)


def _mla_decode(page_table, qn, qr, kfn, wuk, wuv, cache_kv, cache_krt, layer):
    B, T, _ = qn.shape
    assert B % 2 == 0
    n_pages = page_table.shape[1]
    L = n_pages * PAGE_SIZE
    scale = (B_NOPE_DIM + B_ROPE_DIM) ** -0.5
    n_streams = math.gcd(n_pages // 2, 4)
    ck = math.gcd(L // (2 * n_streams), 4 * PAGE_SIZE)
    nrows = B_HEADS * T
    stat = pltpu.VMEM((n_streams, nrows, LANES), F32)
    tile = lambda c: pl.BlockSpec((2, T, c), lambda b, pt: (b, 0, 0))
    cst = lambda shape: pl.BlockSpec(shape, lambda b, pt: (0,) * len(shape))
    grid_spec = pltpu.PrefetchScalarGridSpec(
        num_scalar_prefetch=1,
        grid=(B // 2,),
        in_specs=[tile(A_WIDTH), tile(2 * LANES), tile(KFULL),
                  cst((B_HEADS, B_NOPE_DIM, KV_RANK)), cst((B_HEADS, KV_RANK, B_V_DIM)),
                  pl.BlockSpec(memory_space=pl.ANY), pl.BlockSpec(memory_space=pl.ANY)],
        out_specs=tile(B_WIDTH),
        scratch_shapes=[pltpu.VMEM((2, L, KV_RANK), F32), pltpu.VMEM((2, B_ROPE_DIM, L), F32),
                        pltpu.VMEM((n_streams, L // (n_streams * ck), ck, KV_RANK), BF16),
                        pltpu.VMEM((n_streams, L // (n_streams * ck), nrows, ck), F32),
                        stat, stat, pltpu.VMEM((n_streams, nrows, KV_RANK), F32), pltpu.SemaphoreType.DMA((2, 2))],
    )
    return pl.pallas_call(
        functools.partial(_mla_decode_kernel, layer=layer, n_pages=n_pages, T=T, scale=scale, ck=ck,
                          n_streams=n_streams),
        grid_spec=grid_spec,
        out_shape=jax.ShapeDtypeStruct((B, T, B_WIDTH), BF16),
        compiler_params=_cparams(("arbitrary",)),
        name="mla_decode",
    )(page_table, qn, qr, kfn, wuk, wuv, cache_kv, cache_krt)


def _pool_kernel(pc_ref, st_ref, wbd_ref, scale_ref, y_ref, newst_ref, e_s, s2_s, s4_s, s8_s, *, ts, T, pos0):
    L = T * ts
    P = (POOL_PREFIX + 1) * ts
    pc = pc_ref[...].reshape(L, C_WIDTH)
    e_s[P - POOL_PREFIX * ts:P, :] = st_ref[...].reshape(POOL_PREFIX * ts, C_WIDTH)
    e_s[P:P + L, :] = pc
    newst_ref[...] = e_s[P + L - POOL_PREFIX * ts:P + L, :].reshape(newst_ref.shape)
    s2_s[P - 14 * ts:P + L, :] = e_s[P - 14 * ts:P + L, :] + e_s[P - 15 * ts:P + L - ts, :]
    s4_s[P - 12 * ts:P + L, :] = s2_s[P - 12 * ts:P + L, :] + s2_s[P - 14 * ts:P + L - 2 * ts, :]
    s8_s[P - 8 * ts:P + L, :] = s4_s[P - 8 * ts:P + L, :] + s4_s[P - 12 * ts:P + L - 4 * ts, :]
    s16 = s8_s[P:P + L, :] + s8_s[P - 8 * ts:P + L - 8 * ts, :]
    lane = lax.broadcasted_iota(jnp.int32, (L, C_WIDTH), 1)
    grp = _div_pow2(lane, C_GROUP_DIM)
    win = jnp.where(grp == 0, s2_s[P:P + L, :],
                    jnp.where(grp == 1, s4_s[P:P + L, :], jnp.where(grp == 2, s8_s[P:P + L, :], s16)))
    wsize = jnp.where(grp == 0, 2, jnp.where(grp == 1, 4, jnp.where(grp == 2, 8, 16)))
    pos = pos0 + _div_pow2(lax.broadcasted_iota(jnp.int32, (L, C_WIDTH), 0), ts)
    cnt = jnp.minimum(pos + 1, wsize).astype(F32)
    d = win / cnt - pc
    y = jnp.dot(d.astype(BF16), wbd_ref[...], preferred_element_type=F32) * scale_ref[...]
    y_ref[...] = y.astype(y_ref.dtype).reshape(y_ref.shape)


def _pool(tok, pc, pool_prev, wbd, pscale, pos0):
    ts, T = tok.ts, tok.T
    rows = (POOL_PREFIX + 1 + T) * ts
    if tok.time_major:
        grid, blk = (tok.B // tok.bb, 1), tok.spec(C_WIDTH)
    else:
        grid, blk = (tok.B, 1), pl.BlockSpec((None, T, C_WIDTH), lambda i, j: (i, 0, 0))
    return pl.pallas_call(
        functools.partial(_pool_kernel, ts=ts, T=T, pos0=pos0),
        grid=grid,
        in_specs=[blk, tok.state_spec(POOL_PREFIX, C_WIDTH), _const_spec((C_WIDTH, C_WIDTH)),
                  _const_spec((1, C_WIDTH))],
        out_specs=[blk, tok.state_spec(POOL_PREFIX, C_WIDTH)],
        out_shape=[jax.ShapeDtypeStruct(tok.shape(C_WIDTH), BF16),
                   jax.ShapeDtypeStruct(tok.state_shape(POOL_PREFIX, C_WIDTH), F32)],
        scratch_shapes=[pltpu.VMEM((rows, C_WIDTH), F32)] * 4,
        compiler_params=_cparams(("arbitrary", "arbitrary")),
        name="pool",
    )(pc, pool_prev, wbd, pscale)


def _ffn_kernel(x_ref, ya_ref, g_ref, yb_ref, yc_ref, cst_ref, wout_ref, gffn_ref, wup_ref, cw_ref, cb_ref,
                wdn_ref, gfin_ref, xo_ref, cnew_ref, gbuf_s, *, tm, ts, final):
    taps = CONV_WIDTH - 1
    pad = max(SUBLANES, taps * ts)
    it = pl.program_id(1)

    @pl.when(it == 0)
    def _():
        gbuf_s[pad - taps * ts:pad, :] = cst_ref[...].reshape(taps * ts, D_FF)

    x = x_ref[...].reshape(tm, D_MODEL)
    ya = (ya_ref[...].reshape(tm, A_WIDTH) * g_ref[...].reshape(tm, A_WIDTH)).astype(BF16)
    ycat = jnp.concatenate([ya, yb_ref[...].reshape(tm, B_WIDTH), yc_ref[...].reshape(tm, C_WIDTH)], axis=-1)
    x1 = x + jnp.dot(ycat, wout_ref[...], preferred_element_type=F32)
    xn = _rms(x1, gffn_ref[...]).astype(BF16)
    acc = jnp.zeros((tm, D_MODEL), F32)
    assert sum(FF_CHUNKS) == D_FF
    lo = 0
    for width in FF_CHUNKS:
        hi = lo + width
        up = jnp.dot(xn, wup_ref[:, lo:hi], preferred_element_type=F32)
        gate = jnp.dot(xn, wup_ref[:, D_FF + lo:D_FF + hi], preferred_element_type=F32)
        gbuf_s[pad:pad + tm, lo:hi] = gate
        gc = cb_ref[:, lo:hi] + cw_ref[2:3, lo:hi] * gate
        for j in range(taps):
            off = pad - (taps - j) * ts
            gc = gc + cw_ref[j:j + 1, lo:hi] * gbuf_s[off:off + tm, lo:hi]
        hh = (gc * _sigmoid(gc) * up).astype(BF16)
        acc = acc + jnp.dot(hh, wdn_ref[lo:hi, :], preferred_element_type=F32)
        lo = hi
    last = gbuf_s[pad + tm - taps * ts:pad + tm, :]
    cnew_ref[...] = last.reshape(cnew_ref.shape)
    gbuf_s[pad - taps * ts:pad, :] = last
    x2 = x1 + acc
    if final:
        x2 = _rms(x2, gfin_ref[...])
    xo_ref[...] = x2.reshape(xo_ref.shape)


def _ffn(tok, x, ya, g, yb, yc, conv_prev, p, gfin, final):
    tm, ts = tok.tm, tok.ts
    taps = CONV_WIDTH - 1
    pad = max(SUBLANES, taps * ts)
    return pl.pallas_call(
        functools.partial(_ffn_kernel, tm=tm, ts=ts, final=final),
        grid=tok.grid,
        in_specs=[tok.spec(D_MODEL), tok.spec(A_WIDTH), tok.spec(A_WIDTH), tok.spec(B_WIDTH), tok.spec(C_WIDTH),
                  tok.state_spec(taps, D_FF), _const_spec((D_MODEL, D_MODEL)), _const_spec((1, D_MODEL)),
                  _const_spec((D_MODEL, 2 * D_FF)), _const_spec((CONV_WIDTH, D_FF)), _const_spec((1, D_FF)),
                  _const_spec((D_FF, D_MODEL)), _const_spec((1, D_MODEL))],
        out_specs=[tok.spec(D_MODEL), tok.state_spec(taps, D_FF)],
        out_shape=[jax.ShapeDtypeStruct(tok.shape(D_MODEL), F32),
                   jax.ShapeDtypeStruct(tok.state_shape(taps, D_FF), F32)],
        scratch_shapes=[pltpu.VMEM((pad + tm, D_FF), F32)],
        compiler_params=_cparams(("arbitrary", "arbitrary")),
        name="outproj_ffn",
    )(x, ya, g, yb, yc, conv_prev, p["w_out"], p["g_ffn"], p["w_up"], p["conv_w"], p["conv_b"], p["w_down"], gfin)


def _ext_columns():
    qn0 = A_COLS
    qr0 = qn0 + B_HEADS * B_NOPE_DIM
    ckv0 = qr0 + B_HEADS * B_ROPE_DIM
    kr0 = ckv0 + KV_RANK
    pc0 = A_COLS + B_COLS
    cols = list(range(A_COLS)) + list(range(qn0, qr0)) + list(range(ckv0, kr0)) + list(range(pc0, pc0 + C_WIDTH))
    for half in range(2):
        blk = []
        for h in range(B_HEADS):
            blk += [qr0 + h * B_ROPE_DIM + half * HALF_ROPE + i for i in range(HALF_ROPE)]
        blk += [kr0 + half * HALF_ROPE + i for i in range(HALF_ROPE)]
        cols += blk + [-1] * (LANES - len(blk))
    for half in range(2):
        blk = [kr0 + half * HALF_ROPE + i for _ in range(B_HEADS) for i in range(HALF_ROPE)]
        cols += blk + [-1] * (LANES - len(blk))
    assert len(cols) == IN_EXT
    return np.asarray(cols, np.int32)


def _layer_params(l, norm_mix_g, w_in, mu_shift, decay_w0, decay_w2, iclr_a0, iclr_a2, gate_g2, k_k, k_a, r_k,
                  lnx_w, lnx_b, kv_norm_g, w_uk, w_uv, pool_w, pool_scale, w_out, norm_ffn_g, w_up, conv_w,
                  conv_b, w_down):
    cols = _ext_columns()
    w_ext = jnp.where(cols[None, :] >= 0, w_in[l][:, np.maximum(cols, 0)], 0.0).astype(BF16)
    lora_cat = jnp.zeros((LANES, 2 * A_WIDTH), F32)
    lora_cat = lora_cat.at[0:DECAY_LORA, 0:A_WIDTH].set(decay_w2[l])
    lora_cat = lora_cat.at[DECAY_LORA:DECAY_LORA + ICLR_LORA, A_WIDTH:].set(iclr_a2[l])
    head = np.arange(A_WIDTH) // A_HEAD_DIM
    head_ones = jnp.asarray((head[:, None] == head[None, :]).astype(np.float32), BF16)
    wbd = jnp.zeros((C_WIDTH, C_WIDTH), F32)
    for gi in range(C_GROUPS):
        sl = slice(gi * C_GROUP_DIM, (gi + 1) * C_GROUP_DIM)
        wbd = wbd.at[sl, sl].set(pool_w[l, gi])
    row = lambda v: v.reshape(1, -1)
    return dict(
        g_mix=row(norm_mix_g[l]), w_ext=w_ext, kv_g=row(kv_norm_g[l]),
        mu=row(mu_shift[l]), lora_cat=lora_cat.astype(BF16), w0=row(decay_w0[l]), a0=row(iclr_a0[l]),
        g2=gate_g2[l].astype(BF16), k_k=row(k_k[l]), k_a=row(k_a[l]), head_ones=head_ones,
        r_k=r_k[l], lnx_w=lnx_w[l].reshape(A_HEADS, A_HEAD_DIM), lnx_b=lnx_b[l].reshape(A_HEADS, A_HEAD_DIM),
        wuk=jnp.transpose(w_uk[l], (1, 2, 0)).astype(BF16), wuv=jnp.transpose(w_uv[l], (1, 0, 2)).astype(BF16),
        pool_wbd=wbd.astype(BF16), pool_scale=row(pool_scale[l]),
        w_out=w_out[l].astype(BF16), g_ffn=row(norm_ffn_g[l]), w_up=w_up[l].astype(BF16),
        conv_w=conv_w[l], conv_b=row(conv_b[l]), w_down=w_down[l].astype(BF16),
    )


def _rope_tables(pos):
    inv = ROPE_BASE ** (-jnp.arange(0, B_ROPE_DIM, 2, dtype=F32) / B_ROPE_DIM)
    ang = pos.astype(F32)[:, None] * inv[None, :]
    reps = LANES // HALF_ROPE
    return jnp.tile(jnp.cos(ang), (1, reps)), jnp.tile(jnp.sin(ang), (1, reps))


def _to_scan(tok, x):
    assert tok.time_major
    x = x.reshape(tok.T, tok.B, A_HEADS, A_HEAD_DIM)
    return jnp.transpose(x, (2, 0, 3, 1))


def _lane_maps(nb):
    lane_of = np.asarray([[_prompt_lane(b, h, nb) for h in range(A_HEADS)] for b in range(nb)], np.int32)
    lanes_b = np.zeros(nb * A_HEADS, np.int32)
    lanes_h = np.zeros(nb * A_HEADS, np.int32)
    for b in range(nb):
        for h in range(A_HEADS):
            lanes_b[lane_of[b, h]], lanes_h[lane_of[b, h]] = b, h
    return lane_of, lanes_b, lanes_h


def _from_scan(tok, y):
    H, N = A_HEADS, A_HEAD_DIM
    if tok.time_major:
        return jnp.transpose(y, (1, 3, 0, 2)).reshape(tok.T, tok.B, H * N)
    lane_of, _, _ = _lane_maps(tok.B)
    y = y[0][:, :, lane_of]
    return jnp.transpose(y, (2, 0, 3, 1)).reshape(tok.B, tok.T, H * N)


def _pad_lanes(x):
    return jnp.pad(x, [(0, 0)] * (x.ndim - 1) + [(0, LANES - x.shape[-1])])


def _state_to_scan(tok, s):
    if tok.time_major:
        return jnp.transpose(s, (1, 3, 2, 0))
    _, lanes_b, lanes_h = _lane_maps(tok.B)
    return _pad_lanes(jnp.transpose(s, (3, 2, 0, 1))[:, :, lanes_b, lanes_h])[None]


def _state_from_scan(tok, s):
    if tok.time_major:
        return jnp.transpose(s, (3, 0, 2, 1))
    lane_of, _, _ = _lane_maps(tok.B)
    return jnp.transpose(s[0][:, :, lane_of], (2, 3, 1, 0))


def _head_consts(tok, p):
    def lay(v):
        if tok.time_major:
            return jnp.broadcast_to(v[:, :, None], (A_HEADS, A_HEAD_DIM, LANES))
        _, _, lanes_h = _lane_maps(tok.B)
        return _pad_lanes(v.T[:, lanes_h])[None]
    return lay(p["lnx_w"]), lay(p["lnx_b"]), lay(p["r_k"])


def _layer(tok, x, cos, sin, pos0, shift_prev, wkv_prev, pool_prev, conv_prev, p, gfin, final, scan_tc, attend):
    pa, qn, kvlat, pc, qr, kfull, krope = _inproj(tok, x, p["g_mix"], p["w_ext"], p["kv_g"], cos, sin)
    if tok.time_major:
        r, w, k, v, a, b, g, shift_new = _rwkv_prep(tok, pa, shift_prev, p)
        r, w, k, v, a, b = (_to_scan(tok, t) for t in (r, w, k, v, a, b))
    else:
        r, w, k, v, a, b, g, shift_new = _rwkv_prep_scan(pa, shift_prev, p, scan_tc)
    lnw, lnb, rk = _head_consts(tok, p)
    y, s_new = _rwkv_scan(w, a, b, k, v, r, _state_to_scan(tok, wkv_prev), lnw, lnb, rk, scan_tc)
    ya = _from_scan(tok, y)
    yb = attend(qn, qr, kfull)
    yc, pool_new = _pool(tok, pc, pool_prev, p["pool_wbd"], p["pool_scale"], pos0)
    x, conv_new = _ffn(tok, x, ya, g, yb, yc, conv_prev, p, gfin, final)
    return x, (kvlat, krope, _state_from_scan(tok, s_new), shift_new, pool_new, conv_new)


def kernel(x_prompt, x_sample, cache_kv_latent, cache_k_rope, page_table, state_wkv, state_shift, state_pool,
           state_conv, norm_mix_g, w_in, mu_shift, decay_w0, decay_w2, iclr_a0, iclr_a2, gate_g2, k_k, k_a, r_k,
           lnx_w, lnx_b, kv_norm_g, w_uk, w_uv, pool_w, pool_scale, w_out, norm_ffn_g, w_up, conv_w, conv_b,
           w_down, norm_final_g):
    Bp, Tp, _ = x_prompt.shape
    Bs, Ts, _ = x_sample.shape
    past_len = page_table.shape[1] * PAGE_SIZE
    tokp = _Tok(False, Bp, Tp, min(512, Tp))
    toks = _Tok(True, Bs, Ts, min(32, Bs))
    cos_p, sin_p = _rope_tables(jnp.arange(Tp, dtype=jnp.int32))
    pos_s = past_len + jnp.repeat(jnp.arange(Ts, dtype=jnp.int32), toks.bb)
    cos_s, sin_s = _rope_tables(pos_s)
    gfin = norm_final_g.reshape(1, -1)
    tm_first = lambda t: jnp.swapaxes(t, 0, 1)
    cache_krt = jnp.swapaxes(cache_k_rope, 2, 3)

    xp = x_prompt
    xs = tm_first(x_sample)
    zeros = lambda *shape: jnp.zeros(shape, F32)
    outs_p, outs_s = [], []
    for l in range(DEPTH):
        p = _layer_params(l, norm_mix_g, w_in, mu_shift, decay_w0, decay_w2, iclr_a0, iclr_a2, gate_g2, k_k, k_a,
                          r_k, lnx_w, lnx_b, kv_norm_g, w_uk, w_uv, pool_w, pool_scale, w_out, norm_ffn_g, w_up,
                          conv_w, conv_b, w_down)
        final = l == DEPTH - 1

        attend_p = lambda qn, qr, kfull: _mla_prompt(qn, qr, kfull, p["wuk"], p["wuv"], min(256, Tp))
        xp, st_p = _layer(tokp, xp, cos_p, sin_p, 0, zeros(Bp, A_COLS),
                          zeros(Bp, A_HEADS, A_HEAD_DIM, A_HEAD_DIM), zeros(Bp, POOL_PREFIX, C_WIDTH),
                          zeros(Bp, CONV_WIDTH - 1, D_FF), p, gfin, final, min(32, Tp), attend_p)

        def attend_s(qn, qr, kfull):
            yb = _mla_decode(page_table, tm_first(qn), tm_first(qr), tm_first(kfull), p["wuk"], p["wuv"],
                             cache_kv_latent, cache_krt, l)
            return tm_first(yb)
        xs, st_s = _layer(toks, xs, cos_s, sin_s, past_len, state_shift[l][None], state_wkv[l],
                          tm_first(state_pool[l]), tm_first(state_conv[l]), p, gfin, final, Ts, attend_s)
        outs_p.append(st_p)
        outs_s.append(st_s)

    stack = lambda outs, i, f: jnp.stack([f(o[i]) for o in outs], axis=0)
    ident = lambda t: t
    return (xp, tm_first(xs),
            stack(outs_p, 0, ident), stack(outs_p, 1, ident), stack(outs_p, 2, ident),
            stack(outs_p, 3, ident), stack(outs_p, 4, ident), stack(outs_p, 5, ident),
            stack(outs_s, 0, tm_first), stack(outs_s, 1, tm_first), stack(outs_s, 2, ident),
            stack(outs_s, 3, lambda t: t[0]), stack(outs_s, 4, tm_first), stack(outs_s, 5, tm_first))
```
